```python
import jax, jax.numpy as jnp
from jax import lax
import numpy as np

D_MODEL = 2048
BATCH = 8
SEQ = 4096
DEPTH = 1
DEC_BATCH = 8
DEC_SEQ = 32
PAST_LEN = 4096

CHUNK = 64
BAND_CHUNKS = 8
BAND_PAST = BAND_CHUNKS * CHUNK
A_HEADS = 8
A_HEAD_DIM = 128
A_WIDTH = A_HEADS * A_HEAD_DIM
REL_CLIP = 128
MLA_HEADS = 8
Q_LORA = 768
KV_LORA = 512
NOPE_DIM = 128
ROPE_DIM = 64
V_DIM = 128
MLA_WIDTH = MLA_HEADS * V_DIM
MLA_SCALE = (NOPE_DIM + ROPE_DIM) ** -0.5
ROPE_THETA = 10000.0
D_FF = 5632
PLE_DIM = 256
Q_BLOCK = 128
EPS = 1e-6
NEG = -1e30
IN_SPLIT_SIZES = (A_WIDTH, A_WIDTH, A_WIDTH, Q_LORA, KV_LORA, ROPE_DIM, D_MODEL, D_MODEL)
N_IN = sum(IN_SPLIT_SIZES)

kernel_name = 'streaming_chunkband_mla_hybrid'


def rmsnorm(x, g):
    xf = x.astype(jnp.float32)
    y = xf * lax.rsqrt(jnp.mean(xf * xf, axis=-1, keepdims=True) + EPS)
    return (y * g.astype(jnp.float32)).astype(x.dtype)


def swiglu(x, w_gate, w_up, w_down):
    return (jax.nn.silu(x @ w_gate) * (x @ w_up)) @ w_down


def rope(x, pos):
    half = ROPE_DIM // 2
    inv = ROPE_THETA ** (-jnp.arange(half, dtype=jnp.float32) / half)
    ang = pos.astype(jnp.float32)[:, None] * inv[None, :]
    shape = (1, pos.shape[0]) + (1,) * (x.ndim - 3) + (half,)
    cos = jnp.cos(ang).reshape(shape)
    sin = jnp.sin(ang).reshape(shape)
    xf = x.astype(jnp.float32)
    x1, x2 = xf[..., :half], xf[..., half:]
    return jnp.concatenate([x1 * cos - x2 * sin, x2 * cos + x1 * sin], axis=-1).astype(x.dtype)


def mixer_inputs(u, pos, w_in, g_cq, w_uq, w_uk, g_ckv):
    B, T, _ = u.shape
    z = u @ w_in
    idx = np.cumsum(IN_SPLIT_SIZES)[:-1].tolist()
    qa, ka, va, cq, ckv, kr, ga, gb = jnp.split(z, idx, axis=-1)
    qa = qa.reshape(B, T, A_HEADS, A_HEAD_DIM)
    ka = ka.reshape(B, T, A_HEADS, A_HEAD_DIM)
    va = va.reshape(B, T, A_HEADS, A_HEAD_DIM)
    q = (rmsnorm(cq, g_cq) @ w_uq).reshape(B, T, MLA_HEADS, NOPE_DIM + ROPE_DIM)
    q_rope = rope(q[..., NOPE_DIM:], pos)
    q_lat = jnp.einsum('bthd,hdc->bthc', q[..., :NOPE_DIM], w_uk)
    ckv = rmsnorm(ckv, g_ckv)
    kr = rope(kr, pos)
    return qa, ka, va, q_lat, q_rope, ckv, kr, jax.nn.sigmoid(ga), jax.nn.sigmoid(gb)


def band_attend(q, k, v, q_pos, k_pos, k_valid, rel_bias):
    s = jnp.einsum('bqhd,bkhd->bhqk', q, k).astype(jnp.float32) * (A_HEAD_DIM ** -0.5)
    rel = jnp.clip(k_pos[None, :] - q_pos[:, None], -REL_CLIP, REL_CLIP) + REL_CLIP
    s = s + rel_bias.astype(jnp.float32)[:, rel][None]
    s = jnp.where(k_valid[None, None, None, :], s, NEG)
    p = jax.nn.softmax(s, axis=-1).astype(v.dtype)
    return jnp.einsum('bhqk,bkhd->bqhd', p, v)


def chunk_band_prompt(q, k, v, rel_bias):
    B, S = q.shape[:2]
    nc = S // CHUNK
    band = BAND_PAST + CHUNK
    pad = ((0, 0), (BAND_PAST, 0), (0, 0), (0, 0))
    kp = jnp.pad(k, pad)
    vp = jnp.pad(v, pad)
    qc = jnp.moveaxis(q.reshape(B, nc, CHUNK, A_HEADS, A_HEAD_DIM), 1, 0)

    def one(args):
        c, qi = args
        start = c * CHUNK
        kb = lax.dynamic_slice_in_dim(kp, start, band, axis=1)
        vb = lax.dynamic_slice_in_dim(vp, start, band, axis=1)
        k_pos = start - BAND_PAST + jnp.arange(band)
        q_pos = start + jnp.arange(CHUNK)
        return band_attend(qi, kb, vb, q_pos, k_pos, k_pos >= 0, rel_bias)

    o = lax.map(one, (jnp.arange(nc), qc))
    return jnp.moveaxis(o, 0, 1).reshape(B, S, A_HEADS, A_HEAD_DIM)


def mla_attend(q_lat, q_rope, ckv, krope, mask, w_uv):
    s = (jnp.einsum('bqhc,bkc->bhqk', q_lat, ckv)
         + jnp.einsum('bqhr,bkr->bhqk', q_rope, krope)).astype(jnp.float32) * MLA_SCALE
    s = jnp.where(mask[None, None], s, NEG)
    p = jax.nn.softmax(s, axis=-1).astype(ckv.dtype)
    o_lat = jnp.einsum('bhqk,bkc->bqhc', p, ckv)
    return jnp.einsum('bqhc,hcv->bqhv', o_lat, w_uv)


def mla_prompt(q_lat, q_rope, ckv, krope, w_uv):
    B, S = q_lat.shape[:2]
    nb = S // Q_BLOCK
    ql = jnp.moveaxis(q_lat.reshape(B, nb, Q_BLOCK, MLA_HEADS, KV_LORA), 1, 0)
    qr = jnp.moveaxis(q_rope.reshape(B, nb, Q_BLOCK, MLA_HEADS, ROPE_DIM), 1, 0)
    k_chunk = jnp.arange(S) // CHUNK

    def one(args):
        b, qli, qri = args
        q_chunk = (b * Q_BLOCK + jnp.arange(Q_BLOCK)) // CHUNK
        mask = q_chunk[:, None] >= k_chunk[None, :]
        return mla_attend(qli, qri, ckv, krope, mask, w_uv)

    o = lax.map(one, (jnp.arange(nb), ql, qr))
    return jnp.moveaxis(o, 0, 1).reshape(B, S, MLA_HEADS, V_DIM)


def merge(oa, ob, ga, gb, w_oa, w_ob, w_out):
    B, T = oa.shape[:2]
    m = ga * (oa.reshape(B, T, A_WIDTH) @ w_oa) + gb * (ob.reshape(B, T, MLA_WIDTH) @ w_ob)
    return m @ w_out


def setup_inputs(seed: int = 0) -> dict:
    key = jax.random.key(seed)
    ks = iter(jax.random.split(key, 40))
    a_cache = min(BAND_PAST, PAST_LEN)
    f32 = jnp.float32

    def nrm(shape, scale=1.0):
        return jax.random.normal(next(ks), shape, f32) * scale

    def dense(shape, fan_in):
        return nrm(shape, fan_in ** -0.5)

    def gain(n):
        return 1.0 + nrm((DEPTH, n), 0.05)

    return {
        'x_prompt': nrm((BATCH, SEQ, D_MODEL)),
        'x_sample': nrm((DEC_BATCH, DEC_SEQ, D_MODEL)),
        'cache_a_k': nrm((DEPTH, DEC_BATCH, a_cache, A_HEADS, A_HEAD_DIM)),
        'cache_a_v': nrm((DEPTH, DEC_BATCH, a_cache, A_HEADS, A_HEAD_DIM)),
        'cache_mla_ckv': nrm((DEPTH, DEC_BATCH, PAST_LEN, KV_LORA)),
        'cache_mla_krope': nrm((DEPTH, DEC_BATCH, PAST_LEN, ROPE_DIM)),
        'p_prompt': nrm((DEPTH, BATCH, SEQ, PLE_DIM)),
        'p_sample': nrm((DEPTH, DEC_BATCH, DEC_SEQ, PLE_DIM)),
        'g_ffn1': gain(D_MODEL),
        'w_ffn1_gate': dense((DEPTH, D_MODEL, D_FF), D_MODEL),
        'w_ffn1_up': dense((DEPTH, D_MODEL, D_FF), D_MODEL),
        'w_ffn1_down': dense((DEPTH, D_FF, D_MODEL), D_FF),
        'g_mix': gain(D_MODEL),
        'w_in': dense((DEPTH, D_MODEL, N_IN), D_MODEL),
        'g_cq': gain(Q_LORA),
        'w_uq': dense((DEPTH, Q_LORA, MLA_HEADS * (NOPE_DIM + ROPE_DIM)), Q_LORA),
        'w_uk': dense((DEPTH, MLA_HEADS, NOPE_DIM, KV_LORA), KV_LORA),
        'w_uv': dense((DEPTH, MLA_HEADS, KV_LORA, V_DIM), KV_LORA),
        'g_ckv': gain(KV_LORA),
        'rel_bias': nrm((DEPTH, A_HEADS, 2 * REL_CLIP + 1), 0.2),
        'w_oa': dense((DEPTH, A_WIDTH, D_MODEL), A_WIDTH),
        'w_ob': dense((DEPTH, MLA_WIDTH, D_MODEL), MLA_WIDTH),
        'w_out': dense((DEPTH, D_MODEL, D_MODEL), D_MODEL),
        'g_ffn2': gain(D_MODEL),
        'w_ffn2_gate': dense((DEPTH, D_MODEL, D_FF), D_MODEL),
        'w_ffn2_up': dense((DEPTH, D_MODEL, D_FF), D_MODEL),
        'w_ffn2_down': dense((DEPTH, D_FF, D_MODEL), D_FF),
        'g_ple': gain(D_MODEL),
        'w_ple_gate': dense((DEPTH, D_MODEL, D_MODEL), D_MODEL),
        'w_ple': dense((DEPTH, PLE_DIM, D_MODEL), PLE_DIM),
        'g_final': 1.0 + nrm((D_MODEL,), 0.05),
    }


def reference(x_prompt, x_sample, cache_a_k, cache_a_v, cache_mla_ckv, cache_mla_krope,
              p_prompt, p_sample, g_ffn1, w_ffn1_gate, w_ffn1_up, w_ffn1_down, g_mix, w_in,
              g_cq, w_uq, w_uk, w_uv, g_ckv, rel_bias, w_oa, w_ob, w_out, g_ffn2,
              w_ffn2_gate, w_ffn2_up, w_ffn2_down, g_ple, w_ple_gate, w_ple, g_final):
    S = x_prompt.shape[1]
    T = x_sample.shape[1]
    past = cache_mla_ckv.shape[2]
    a_len = cache_a_k.shape[2]
    a_keep = min(BAND_PAST, S)
    pos_p = jnp.arange(S)
    pos_s = past + jnp.arange(T)
    ka_pos_s = past - a_len + jnp.arange(a_len + T)
    ka_valid_s = jnp.ones((a_len + T,), dtype=bool)
    mla_mask_s = jnp.ones((T, past + T), dtype=bool)
    hp, hs = x_prompt, x_sample
    akp, avp, ckvp, krp = [], [], [], []
    aks, avs, ckvs, krs = [], [], [], []
    for i in range(DEPTH):
        hp = hp + 0.5 * swiglu(rmsnorm(hp, g_ffn1[i]), w_ffn1_gate[i], w_ffn1_up[i], w_ffn1_down[i])
        hs = hs + 0.5 * swiglu(rmsnorm(hs, g_ffn1[i]), w_ffn1_gate[i], w_ffn1_up[i], w_ffn1_down[i])

        qa, ka, va, q_lat, q_rope, ckv, kr, ga, gb = mixer_inputs(
            rmsnorm(hp, g_mix[i]), pos_p, w_in[i], g_cq[i], w_uq[i], w_uk[i], g_ckv[i])
        oa = chunk_band_prompt(qa, ka, va, rel_bias[i])
        ob = mla_prompt(q_lat, q_rope, ckv, kr, w_uv[i])
        hp = hp + merge(oa, ob, ga, gb, w_oa[i], w_ob[i], w_out[i])
        akp.append(ka[:, S - a_keep:])
        avp.append(va[:, S - a_keep:])
        ckvp.append(ckv)
        krp.append(kr)

        qa, ka, va, q_lat, q_rope, ckv, kr, ga, gb = mixer_inputs(
            rmsnorm(hs, g_mix[i]), pos_s, w_in[i], g_cq[i], w_uq[i], w_uk[i], g_ckv[i])
        k_all = jnp.concatenate([cache_a_k[i].astype(ka.dtype), ka], axis=1)
        v_all = jnp.concatenate([cache_a_v[i].astype(va.dtype), va], axis=1)
        oa = band_attend(qa, k_all, v_all, pos_s, ka_pos_s, ka_valid_s, rel_bias[i])
        ckv_all = jnp.concatenate([cache_mla_ckv[i].astype(ckv.dtype), ckv], axis=1)
        kr_all = jnp.concatenate([cache_mla_krope[i].astype(kr.dtype), kr], axis=1)
        ob = mla_attend(q_lat, q_rope, ckv_all, kr_all, mla_mask_s, w_uv[i])
        hs = hs + merge(oa, ob, ga, gb, w_oa[i], w_ob[i], w_out[i])
        aks.append(ka)
        avs.append(va)
        ckvs.append(ckv)
        krs.append(kr)

        hp = hp + 0.5 * swiglu(rmsnorm(hp, g_ffn2[i]), w_ffn2_gate[i], w_ffn2_up[i], w_ffn2_down[i])
        hs = hs + 0.5 * swiglu(rmsnorm(hs, g_ffn2[i]), w_ffn2_gate[i], w_ffn2_up[i], w_ffn2_down[i])

        hp = hp + jax.nn.sigmoid(rmsnorm(hp, g_ple[i]) @ w_ple_gate[i]) * (p_prompt[i] @ w_ple[i])
        hs = hs + jax.nn.sigmoid(rmsnorm(hs, g_ple[i]) @ w_ple_gate[i]) * (p_sample[i] @ w_ple[i])

    y_prompt = rmsnorm(hp, g_final)
    y_sample = rmsnorm(hs, g_final)
    return (y_prompt, y_sample,
            jnp.stack(akp), jnp.stack(avp), jnp.stack(ckvp), jnp.stack(krp),
            jnp.stack(aks), jnp.stack(avs), jnp.stack(ckvs), jnp.stack(krs))
```

```python
import functools

import jax
import jax.numpy as jnp
from jax import lax
from jax.experimental import pallas as pl
from jax.experimental.pallas import tpu as pltpu

F32 = jnp.float32
BF16 = jnp.bfloat16

CHUNK = 64
BAND_PAST = 512
A_HEADS = 8
A_HEAD_DIM = 128
A_WIDTH = A_HEADS * A_HEAD_DIM
REL_CLIP = 128
MLA_HEADS = 8
Q_LORA = 768
KV_LORA = 512
NOPE_DIM = 128
ROPE_DIM = 64
V_DIM = 128
MLA_WIDTH = MLA_HEADS * V_DIM
MLA_SCALE = (NOPE_DIM + ROPE_DIM) ** -0.5
A_SCALE = A_HEAD_DIM ** -0.5
ROPE_THETA = 10000.0
EPS = 1e-6
NEG = -1e30

LANES = 128
V7X_VMEM_BYTES = 64 * 1024 * 1024
VMEM_CEILING = V7X_VMEM_BYTES - 6 * 1024 * 1024

ROW_TILE = 512
INPROJ_ROW_TILE = 256
COL_TILE = 512
BAND_Q = 2 * CHUNK
BAND_K = BAND_PAST + BAND_Q
TOEPLITZ_W = 768
MLA_TILE = 256


def _dot(a, b):
    return jnp.dot(a, b, preferred_element_type=F32)


def _dot_nt(a, b):
    return lax.dot_general(a, b, (((1,), (1,)), ((), ())), preferred_element_type=F32)


def _rms(x, g):
    return x * lax.rsqrt(jnp.mean(x * x, axis=-1, keepdims=True) + EPS) * g


def _vmem_limit(block_bytes, scratch_bytes, temp_bytes):
    return int(min(2 * block_bytes + scratch_bytes + temp_bytes + (4 << 20), VMEM_CEILING))


def _nbytes(shape, dtype):
    n = 1
    for s in shape:
        n *= s
    return n * jnp.dtype(dtype).itemsize


def _resident(shape):
    nd = len(shape)
    return pl.BlockSpec(shape, lambda *_: (0,) * nd, pipeline_mode=pl.Buffered(1))


def _col_tile(n):
    for t in (COL_TILE, 256, LANES):
        if n % t == 0:
            return t
    raise ValueError(f"width {n} is not a multiple of {LANES}")


def _ffn_body(x_ref, g_ref, wg_ref, wu_ref, wd_ref, o_ref, u_ref, acc_ref):
    j = pl.program_id(1)

    @pl.when(j == 0)
    def _():
        u_ref[...] = _rms(x_ref[...], g_ref[...]).astype(BF16)
        acc_ref[...] = jnp.zeros_like(acc_ref)

    u = u_ref[...]
    gate = _dot(u, wg_ref[...])
    up = _dot(u, wu_ref[...])
    act = (gate * jax.nn.sigmoid(gate) * up).astype(BF16)
    acc_ref[...] += _dot(act, wd_ref[...])

    @pl.when(j == pl.num_programs(1) - 1)
    def _():
        o_ref[...] = x_ref[...] + 0.5 * acc_ref[...]


def _ffn(x, g, wg, wu, wd):
    n, d = x.shape
    f = wg.shape[1]
    tm = min(ROW_TILE, n)
    tf = _col_tile(f)
    blocks = _nbytes((tm, d), F32) * 2 + 3 * _nbytes((d, tf), BF16)
    scratch = _nbytes((tm, d), BF16) + _nbytes((tm, d), F32)
    temps = 4 * _nbytes((tm, tf), F32) + _nbytes((tm, d), F32)
    return pl.pallas_call(
        _ffn_body,
        grid=(n // tm, f // tf),
        in_specs=[
            pl.BlockSpec((tm, d), lambda i, j: (i, 0)),
            pl.BlockSpec((1, d), lambda i, j: (0, 0)),
            pl.BlockSpec((d, tf), lambda i, j: (0, j)),
            pl.BlockSpec((d, tf), lambda i, j: (0, j)),
            pl.BlockSpec((tf, d), lambda i, j: (j, 0)),
        ],
        out_specs=pl.BlockSpec((tm, d), lambda i, j: (i, 0)),
        out_shape=jax.ShapeDtypeStruct((n, d), F32),
        scratch_shapes=[pltpu.VMEM((tm, d), BF16), pltpu.VMEM((tm, d), F32)],
        compiler_params=pltpu.CompilerParams(
            dimension_semantics=("arbitrary", "arbitrary"),
            vmem_limit_bytes=_vmem_limit(blocks, scratch, temps)),
        name="ffn",
    )(x, g, wg, wu, wd)


def _inproj_body(x_ref, g_ref, cs_ref, sn_ref, wa_ref, wc_ref, wk_ref, gcq_ref, gckv_ref, wuq_ref, wkv_ref,
                 qa_o, ka_o, va_o, qn_o, qr_o, kn_o, vm_o, krp_o, ckv_o, kr_o, kat_o, vat_o):
    u = _rms(x_ref[...], g_ref[...]).astype(BF16)

    qa_o[...] = (_dot(u, wa_ref[:, 0:A_WIDTH]) * A_SCALE).astype(BF16)
    ka = _dot(u, wa_ref[:, A_WIDTH:2 * A_WIDTH])
    ka_o[...] = ka.astype(BF16)
    kat_o[...] = ka
    va = _dot(u, wa_ref[:, 2 * A_WIDTH:3 * A_WIDTH])
    va_o[...] = va.astype(BF16)
    vat_o[...] = va

    cs = cs_ref[...]
    sn = sn_ref[...]

    cqn = _rms(_dot(u, wc_ref[...]), gcq_ref[...]).astype(BF16)
    qn_o[...] = (_dot(cqn, wuq_ref[:, 0:MLA_WIDTH]) * MLA_SCALE).astype(BF16)
    q_rot = _dot(cqn, wuq_ref[:, MLA_WIDTH:2 * MLA_WIDTH])
    q_swp = _dot(cqn, wuq_ref[:, 2 * MLA_WIDTH:3 * MLA_WIDTH])
    cs_h = jnp.concatenate([cs] * MLA_HEADS, axis=1)
    sn_h = jnp.concatenate([sn] * MLA_HEADS, axis=1)
    qr_o[...] = ((q_rot * cs_h + q_swp * sn_h) * MLA_SCALE).astype(BF16)

    zc = _dot(u, wk_ref[...])
    ckvn = _rms(zc[:, 0:KV_LORA], gckv_ref[...])
    ckv_o[...] = ckvn
    krp = zc[:, KV_LORA:KV_LORA + LANES] * cs + zc[:, KV_LORA + LANES:KV_LORA + 2 * LANES] * sn
    krp_o[...] = krp.astype(BF16)
    kr_o[...] = krp[:, 0:ROPE_DIM]

    kv = _dot(ckvn.astype(BF16), wkv_ref[...])
    kn_o[...] = kv[:, 0:MLA_WIDTH].astype(BF16)
    vm_o[...] = kv[:, MLA_WIDTH:2 * MLA_WIDTH].astype(BF16)


def _inproj(x, g, cs, sn, wa, wc, wk, gcq, gckv, wuq, wkv, *, rows_per_seq, tail_rows):
    n, d = x.shape
    tm = min(INPROJ_ROW_TILE, n)
    n_tiles = n // tm
    tab_tiles = cs.shape[0] // tm
    if tm <= rows_per_seq:
        tiles_per_seq = rows_per_seq // tm
        tail_tiles = tail_rows // tm
        n_seq = n // rows_per_seq

        def tail_map(i):
            return (i // tiles_per_seq) * tail_tiles + jnp.maximum(i % tiles_per_seq - (tiles_per_seq - tail_tiles), 0), 0

        tail_n = n_seq * tail_rows
    else:
        assert tail_rows == rows_per_seq

        def tail_map(i):
            return i, 0

        tail_n = n

    def row(i):
        return i, 0

    def tab(i):
        return i % tab_tiles, 0

    wide = pl.BlockSpec((tm, A_WIDTH), row)
    out_shapes = [jax.ShapeDtypeStruct((n, A_WIDTH), BF16)] * 7 + [
        jax.ShapeDtypeStruct((n, LANES), BF16),
        jax.ShapeDtypeStruct((n, KV_LORA), F32),
        jax.ShapeDtypeStruct((n, ROPE_DIM), F32),
        jax.ShapeDtypeStruct((tail_n, A_WIDTH), F32),
        jax.ShapeDtypeStruct((tail_n, A_WIDTH), F32),
    ]
    out_specs = [wide] * 7 + [
        pl.BlockSpec((tm, LANES), row),
        pl.BlockSpec((tm, KV_LORA), row),
        pl.BlockSpec((tm, ROPE_DIM), row),
        pl.BlockSpec((tm, A_WIDTH), tail_map),
        pl.BlockSpec((tm, A_WIDTH), tail_map),
    ]
    weights = [wa, wc, wk, gcq, gckv, wuq, wkv]
    blocks = (_nbytes((tm, d), F32) + 7 * _nbytes((tm, A_WIDTH), BF16) + 3 * _nbytes((tm, A_WIDTH), F32)
              + 4 * _nbytes((tm, LANES), F32))
    resident = sum(_nbytes(w.shape, w.dtype) for w in weights)
    temps = 6 * _nbytes((tm, 2 * A_WIDTH), F32)
    return pl.pallas_call(
        _inproj_body,
        grid=(n_tiles,),
        in_specs=[
            pl.BlockSpec((tm, d), row),
            _resident((1, d)),
            pl.BlockSpec((tm, LANES), tab),
            pl.BlockSpec((tm, LANES), tab),
        ] + [_resident(w.shape) for w in weights],
        out_specs=out_specs,
        out_shape=out_shapes,
        compiler_params=pltpu.CompilerParams(
            dimension_semantics=("arbitrary",),
            vmem_limit_bytes=_vmem_limit(blocks, resident, temps)),
        name="inproj",
    )(x, g, cs, sn, *weights)


def _toeplitz(row, nrows):
    x = jnp.broadcast_to(row, (nrows, TOEPLITZ_W))
    left = (LANES - 1) - lax.broadcasted_iota(jnp.int32, (nrows, TOEPLITZ_W), 0)
    for b in range(7):
        rolled = pltpu.roll(x, TOEPLITZ_W - (1 << b), axis=1)
        x = jnp.where(((left >> b) & 1) == 1, rolled, x)
    return x


def _softmax_pv(s, v):
    m = jnp.max(s, axis=-1, keepdims=True)
    p = jnp.exp(s - m)
    l = jnp.sum(p, axis=-1, keepdims=True)
    return _dot(p.astype(BF16), v) / l


def _band_body(rb_ref, q_ref, k_ref, v_ref, o_ref, t_ref, *, n_tiles):
    @pl.when(pl.program_id(1) == 0)
    def _():
        t = _toeplitz(rb_ref[...], BAND_Q)[:, 0:BAND_K]
        i = lax.broadcasted_iota(jnp.int32, (BAND_Q, BAND_K), 0)
        j = lax.broadcasted_iota(jnp.int32, (BAND_Q, BAND_K), 1)
        visible = ((i < CHUNK) & (j < BAND_PAST + CHUNK)) | ((i >= CHUNK) & (j >= CHUNK))
        t_ref[...] = jnp.where(visible, t, NEG)

    past_tiles = BAND_PAST // BAND_Q
    for t in range(past_tiles):
        nk = (t + 1) * BAND_Q
        s = _dot_nt(q_ref[t * BAND_Q:(t + 1) * BAND_Q, :], k_ref[0:nk, :]) + t_ref[:, BAND_K - nk:BAND_K]
        o_ref[t * BAND_Q:(t + 1) * BAND_Q, :] = _softmax_pv(s, v_ref[0:nk, :]).astype(BF16)

    def tile(t, carry):
        r0 = pl.multiple_of(t * BAND_Q, BAND_Q)
        k0 = pl.multiple_of(t * BAND_Q - BAND_PAST, BAND_Q)
        s = _dot_nt(q_ref[pl.ds(r0, BAND_Q), :], k_ref[pl.ds(k0, BAND_K), :]) + t_ref[...]
        o_ref[pl.ds(r0, BAND_Q), :] = _softmax_pv(s, v_ref[pl.ds(k0, BAND_K), :]).astype(BF16)
        return carry

    lax.fori_loop(past_tiles, n_tiles, tile, 0)


def _band_prompt(q, k, v, rb_rows):
    b, s, _ = q.shape
    assert s % BAND_Q == 0 and s >= BAND_K
    head = pl.BlockSpec((None, s, A_HEAD_DIM), lambda h, i: (i, 0, h))
    blocks = 4 * _nbytes((s, A_HEAD_DIM), BF16)
    scratch = _nbytes((BAND_Q, BAND_K), F32)
    temps = 16 * _nbytes((BAND_Q, TOEPLITZ_W), F32)
    return pl.pallas_call(
        functools.partial(_band_body, n_tiles=s // BAND_Q),
        grid=(A_HEADS, b),
        in_specs=[pl.BlockSpec((None, 1, TOEPLITZ_W), lambda h, i: (h, 0, 0)), head, head, head],
        out_specs=head,
        out_shape=jax.ShapeDtypeStruct(q.shape, BF16),
        scratch_shapes=[pltpu.VMEM((BAND_Q, BAND_K), F32)],
        compiler_params=pltpu.CompilerParams(
            dimension_semantics=("arbitrary", "arbitrary"),
            vmem_limit_bytes=_vmem_limit(blocks, scratch, temps)),
        name="band_prompt",
    )(rb_rows, q, k, v)


def _band_sample_body(rb_ref, q_ref, kn_ref, vn_ref, ck_ref, cv_ref, o_ref):
    t_new = q_ref.shape[0]
    n_cache = ck_ref.shape[0]
    for h in range(A_HEADS):
        cols = slice(h * A_HEAD_DIM, (h + 1) * A_HEAD_DIM)
        bias = _toeplitz(rb_ref[h], t_new)
        q = q_ref[:, cols]
        s_c = _dot_nt(q, ck_ref[:, cols].astype(BF16)) + bias[:, BAND_PAST - n_cache:BAND_PAST]
        s_n = _dot_nt(q, kn_ref[:, cols]) + bias[:, BAND_PAST:BAND_PAST + t_new]
        m = jnp.maximum(jnp.max(s_c, axis=-1, keepdims=True), jnp.max(s_n, axis=-1, keepdims=True))
        p_c = jnp.exp(s_c - m)
        p_n = jnp.exp(s_n - m)
        l = jnp.sum(p_c, axis=-1, keepdims=True) + jnp.sum(p_n, axis=-1, keepdims=True)
        o = _dot(p_c.astype(BF16), cv_ref[:, cols].astype(BF16)) + _dot(p_n.astype(BF16), vn_ref[:, cols])
        o_ref[:, cols] = (o / l).astype(BF16)


def _band_sample(q, k_new, v_new, cache_k, cache_v, rb_rows):
    b, t, w = q.shape
    a = cache_k.shape[1]
    assert t <= LANES and a <= BAND_PAST
    new = pl.BlockSpec((None, t, w), lambda i: (i, 0, 0))
    cache = pl.BlockSpec((None, a, w), lambda i: (i, 0, 0))
    blocks = 4 * _nbytes((t, w), BF16) + 2 * _nbytes((a, w), F32)
    temps = 2 * _nbytes((a, w), F32) + 16 * _nbytes((t, TOEPLITZ_W), F32)
    return pl.pallas_call(
        _band_sample_body,
        grid=(b,),
        in_specs=[_resident(rb_rows.shape), new, new, new, cache, cache],
        out_specs=new,
        out_shape=jax.ShapeDtypeStruct(q.shape, BF16),
        compiler_params=pltpu.CompilerParams(
            dimension_semantics=("arbitrary",),
            vmem_limit_bytes=_vmem_limit(blocks, _nbytes(rb_rows.shape, F32), temps)),
        name="band_sample",
    )(rb_rows, q, k_new, v_new, cache_k, cache_v)


def _mla_body(qn_ref, qr_ref, kn_ref, kr_ref, v_ref, o_ref, *, n_tiles):
    r = lax.broadcasted_iota(jnp.int32, (MLA_TILE, MLA_TILE), 0)
    c = lax.broadcasted_iota(jnp.int32, (MLA_TILE, MLA_TILE), 1)
    diag_visible = (c // CHUNK) <= (r // CHUNK)

    def keys(c0):
        return jnp.concatenate([kn_ref[pl.ds(c0, MLA_TILE), :], kr_ref[pl.ds(c0, MLA_TILE), :]], axis=1)

    def q_tile(i, carry):
        r0 = pl.multiple_of(i * MLA_TILE, MLA_TILE)
        q = jnp.concatenate([qn_ref[pl.ds(r0, MLA_TILE), :], qr_ref[pl.ds(r0, MLA_TILE), :]], axis=1)

        s = jnp.where(diag_visible, _dot_nt(q, keys(r0)), NEG)
        m = jnp.max(s, axis=-1, keepdims=True)
        p = jnp.exp(s - m)
        l = jnp.sum(p, axis=-1, keepdims=True)
        acc = _dot(p.astype(BF16), v_ref[pl.ds(r0, MLA_TILE), :])

        def kv_tile(j, state):
            m, l, acc = state
            c0 = pl.multiple_of(j * MLA_TILE, MLA_TILE)
            s = _dot_nt(q, keys(c0))
            m_new = jnp.maximum(m, jnp.max(s, axis=-1, keepdims=True))
            alpha = jnp.exp(m - m_new)
            p = jnp.exp(s - m_new)
            l = alpha * l + jnp.sum(p, axis=-1, keepdims=True)
            acc = alpha * acc + _dot(p.astype(BF16), v_ref[pl.ds(c0, MLA_TILE), :])
            return m_new, l, acc

        m, l, acc = lax.fori_loop(0, i, kv_tile, (m, l, acc))
        o_ref[pl.ds(r0, MLA_TILE), :] = (acc / l).astype(BF16)
        return carry

    lax.fori_loop(0, n_tiles, q_tile, 0)


def _mla_prompt(qn, qr, kn, kr, v):
    b, s, _ = qn.shape
    assert s % MLA_TILE == 0
    head = pl.BlockSpec((None, s, LANES), lambda i, h: (i, 0, h))
    shared = pl.BlockSpec((None, s, LANES), lambda i, h: (i, 0, 0))
    blocks = 6 * _nbytes((s, LANES), BF16)
    temps = 16 * _nbytes((MLA_TILE, MLA_TILE), F32)
    return pl.pallas_call(
        functools.partial(_mla_body, n_tiles=s // MLA_TILE),
        grid=(b, MLA_HEADS),
        in_specs=[head, head, head, shared, head],
        out_specs=head,
        out_shape=jax.ShapeDtypeStruct(qn.shape, BF16),
        compiler_params=pltpu.CompilerParams(
            dimension_semantics=("arbitrary", "arbitrary"),
            vmem_limit_bytes=_vmem_limit(blocks, 0, temps)),
        name="mla_prompt",
    )(qn, qr, kn, kr, v)


def _mla_sample_body(qn_ref, qr_ref, cn_ref, krn_ref, cc_ref, ckr_ref, wuk_ref, wuv_ref, o_ref):
    t_new = qn_ref.shape[0]
    q_lat = jnp.concatenate(
        [_dot(qn_ref[:, h * NOPE_DIM:(h + 1) * NOPE_DIM], wuk_ref[h]).astype(BF16) for h in range(MLA_HEADS)], axis=0)
    q_rot = jnp.concatenate(
        [qr_ref[:, h * LANES:h * LANES + ROPE_DIM] for h in range(MLA_HEADS)], axis=0)
    cache = cc_ref[...].astype(BF16)
    new = cn_ref[...].astype(BF16)
    s_c = _dot_nt(q_lat, cache) + _dot_nt(q_rot, ckr_ref[...].astype(BF16))
    s_n = _dot_nt(q_lat, new) + _dot_nt(q_rot, krn_ref[:, 0:ROPE_DIM])
    m = jnp.maximum(jnp.max(s_c, axis=-1, keepdims=True), jnp.max(s_n, axis=-1, keepdims=True))
    p_c = jnp.exp(s_c - m)
    p_n = jnp.exp(s_n - m)
    l = jnp.sum(p_c, axis=-1, keepdims=True) + jnp.sum(p_n, axis=-1, keepdims=True)
    o_lat = ((_dot(p_c.astype(BF16), cache) + _dot(p_n.astype(BF16), new)) / l).astype(BF16)
    for h in range(MLA_HEADS):
        o_ref[:, h * V_DIM:(h + 1) * V_DIM] = _dot(o_lat[h * t_new:(h + 1) * t_new, :], wuv_ref[h]).astype(BF16)


def _mla_sample(qn, qr, ckv_new, kr_new, cache_ckv, cache_kr, wuk, wuv):
    b, t, w = qn.shape
    past = cache_ckv.shape[1]

    def per_batch(shape):
        return pl.BlockSpec((None,) + shape, lambda i: (i, 0, 0))

    blocks = (2 * _nbytes((t, w), BF16) + _nbytes((t, KV_LORA), F32) + _nbytes((t, LANES), BF16)
              + _nbytes((past, KV_LORA), F32) + _nbytes((past, ROPE_DIM), F32) + _nbytes((t, w), BF16))
    resident = _nbytes(wuk.shape, BF16) + _nbytes(wuv.shape, BF16)
    temps = _nbytes((past, KV_LORA), F32) + 4 * _nbytes((MLA_HEADS * t, past), F32)
    return pl.pallas_call(
        _mla_sample_body,
        grid=(b,),
        in_specs=[per_batch((t, w)), per_batch((t, w)), per_batch((t, KV_LORA)), per_batch((t, LANES)),
                  per_batch((past, KV_LORA)), per_batch((past, ROPE_DIM)),
                  _resident(wuk.shape), _resident(wuv.shape)],
        out_specs=per_batch((t, w)),
        out_shape=jax.ShapeDtypeStruct(qn.shape, BF16),
        compiler_params=pltpu.CompilerParams(
            dimension_semantics=("arbitrary",),
            vmem_limit_bytes=_vmem_limit(blocks, resident, temps)),
        name="mla_sample",
    )(qn, qr, ckv_new, kr_new, cache_ckv, cache_kr, wuk, wuv)


def _merge_body(h_ref, g_ref, oa_ref, ob_ref, wga_ref, wgb_ref, woa_ref, wob_ref, wout_ref, o_ref, u_ref, acc_ref):
    j = pl.program_id(1)

    @pl.when(j == 0)
    def _():
        u_ref[...] = _rms(h_ref[...], g_ref[...]).astype(BF16)
        acc_ref[...] = jnp.zeros_like(acc_ref)

    u = u_ref[...]
    m = (jax.nn.sigmoid(_dot(u, wga_ref[...])) * _dot(oa_ref[...], woa_ref[...])
         + jax.nn.sigmoid(_dot(u, wgb_ref[...])) * _dot(ob_ref[...], wob_ref[...]))
    acc_ref[...] += _dot(m.astype(BF16), wout_ref[...])

    @pl.when(j == pl.num_programs(1) - 1)
    def _():
        o_ref[...] = h_ref[...] + acc_ref[...]


def _merge(h, g, oa, ob, wga, wgb, woa, wob, wout):
    n, d = h.shape
    tm = min(ROW_TILE, n)
    tn = _col_tile(d)
    blocks = (2 * _nbytes((tm, d), F32) + 2 * _nbytes((tm, A_WIDTH), BF16) + 2 * _nbytes((d, tn), BF16)
              + 2 * _nbytes((A_WIDTH, tn), BF16) + _nbytes((tn, d), BF16))
    scratch = _nbytes((tm, d), BF16) + _nbytes((tm, d), F32)
    temps = 6 * _nbytes((tm, tn), F32) + _nbytes((tm, d), F32)
    return pl.pallas_call(
        _merge_body,
        grid=(n // tm, d // tn),
        in_specs=[
            pl.BlockSpec((tm, d), lambda i, j: (i, 0)),
            pl.BlockSpec((1, d), lambda i, j: (0, 0)),
            pl.BlockSpec((tm, A_WIDTH), lambda i, j: (i, 0)),
            pl.BlockSpec((tm, MLA_WIDTH), lambda i, j: (i, 0)),
            pl.BlockSpec((d, tn), lambda i, j: (0, j)),
            pl.BlockSpec((d, tn), lambda i, j: (0, j)),
            pl.BlockSpec((A_WIDTH, tn), lambda i, j: (0, j)),
            pl.BlockSpec((MLA_WIDTH, tn), lambda i, j: (0, j)),
            pl.BlockSpec((tn, d), lambda i, j: (j, 0)),
        ],
        out_specs=pl.BlockSpec((tm, d), lambda i, j: (i, 0)),
        out_shape=jax.ShapeDtypeStruct((n, d), F32),
        scratch_shapes=[pltpu.VMEM((tm, d), BF16), pltpu.VMEM((tm, d), F32)],
        compiler_params=pltpu.CompilerParams(
            dimension_semantics=("arbitrary", "arbitrary"),
            vmem_limit_bytes=_vmem_limit(blocks, scratch, temps)),
        name="merge",
    )(h, g, oa, ob, wga, wgb, woa, wob, wout)


def _ple_body(h_ref, p_ref, gp_ref, wpg_ref, wp_ref, gf_ref, y_ref, *, final_norm):
    h = h_ref[...]
    gate = jax.nn.sigmoid(_dot(_rms(h, gp_ref[...]).astype(BF16), wpg_ref[...]))
    h = h + gate * _dot(p_ref[...].astype(BF16), wp_ref[...])
    y_ref[...] = _rms(h, gf_ref[...]) if final_norm else h


def _ple(h, p, gp, wpg, wp, gf, *, final_norm):
    n, d = h.shape
    e = p.shape[1]
    tm = min(ROW_TILE, n)
    blocks = 2 * _nbytes((tm, d), F32) + _nbytes((tm, e), F32)
    resident = _nbytes(wpg.shape, BF16) + _nbytes(wp.shape, BF16)
    temps = 4 * _nbytes((tm, d), F32)
    return pl.pallas_call(
        functools.partial(_ple_body, final_norm=final_norm),
        grid=(n // tm,),
        in_specs=[
            pl.BlockSpec((tm, d), lambda i: (i, 0)),
            pl.BlockSpec((tm, e), lambda i: (i, 0)),
            _resident((1, d)),
            _resident(wpg.shape),
            _resident(wp.shape),
            _resident((1, d)),
        ],
        out_specs=pl.BlockSpec((tm, d), lambda i: (i, 0)),
        out_shape=jax.ShapeDtypeStruct((n, d), F32),
        compiler_params=pltpu.CompilerParams(
            dimension_semantics=("arbitrary",),
            vmem_limit_bytes=_vmem_limit(blocks, resident, temps)),
        name="ple",
    )(h, p, gp, wpg, wp, gf)


def _rope_tables(pos):
    half = ROPE_DIM // 2
    inv = ROPE_THETA ** (-jnp.arange(half, dtype=F32) / half)
    ang = pos.astype(F32)[:, None] * inv[None, :]
    cos, sin = jnp.cos(ang), jnp.sin(ang)
    zero = jnp.zeros((pos.shape[0], LANES - ROPE_DIM), F32)
    return jnp.concatenate([cos, cos, zero], axis=1), jnp.concatenate([-sin, sin, zero], axis=1)


def _pad_swap(w, lead):
    half = ROPE_DIM // 2
    x1, x2 = w[..., :half], w[..., half:]
    zero = jnp.zeros(lead + (LANES - ROPE_DIM,), w.dtype)
    return jnp.concatenate([x1, x2, zero], axis=-1), jnp.concatenate([x2, x1, zero], axis=-1)


def _layer_weights(i, d, g_ffn1, w_ffn1_gate, w_ffn1_up, w_ffn1_down, g_mix, w_in, g_cq, w_uq, w_uk, w_uv, g_ckv,
                   rel_bias, w_oa, w_ob, w_out, g_ffn2, w_ffn2_gate, w_ffn2_up, w_ffn2_down, g_ple, w_ple_gate, w_ple):
    bf = lambda a: a.astype(BF16)
    row = lambda a: a.reshape(1, -1).astype(F32)
    w = w_in[i]
    c0 = 3 * A_WIDTH
    c1 = c0 + Q_LORA
    c2 = c1 + KV_LORA
    c3 = c2 + ROPE_DIM
    kr_rot, kr_swp = _pad_swap(w[:, c2:c3], (d,))
    uq = w_uq[i].reshape(Q_LORA, MLA_HEADS, NOPE_DIM + ROPE_DIM)
    uq_rot, uq_swp = _pad_swap(uq[:, :, NOPE_DIM:], (Q_LORA, MLA_HEADS))
    rel = jnp.clip(jnp.arange(TOEPLITZ_W) - (BAND_PAST + LANES - 1), -REL_CLIP, REL_CLIP) + REL_CLIP
    return dict(
        ffn1=(row(g_ffn1[i]), bf(w_ffn1_gate[i]), bf(w_ffn1_up[i]), bf(w_ffn1_down[i])),
        ffn2=(row(g_ffn2[i]), bf(w_ffn2_gate[i]), bf(w_ffn2_up[i]), bf(w_ffn2_down[i])),
        g_mix=row(g_mix[i]),
        inproj=(
            bf(w[:, 0:c0]),
            bf(w[:, c0:c1]),
            bf(jnp.concatenate([w[:, c1:c2], kr_rot, kr_swp], axis=1)),
            row(g_cq[i]),
            row(g_ckv[i]),
            bf(jnp.concatenate([uq[:, :, :NOPE_DIM].reshape(Q_LORA, MLA_WIDTH), uq_rot.reshape(Q_LORA, MLA_WIDTH),
                                uq_swp.reshape(Q_LORA, MLA_WIDTH)], axis=1)),
            bf(jnp.concatenate([jnp.transpose(w_uk[i], (2, 0, 1)).reshape(KV_LORA, MLA_WIDTH),
                                jnp.transpose(w_uv[i], (1, 0, 2)).reshape(KV_LORA, MLA_WIDTH)], axis=1)),
        ),
        rb_rows=rel_bias[i][:, rel].reshape(A_HEADS, 1, TOEPLITZ_W).astype(F32),
        w_uk=bf(w_uk[i]),
        w_uv=bf(w_uv[i]),
        merge=(bf(w[:, c3:c3 + d]), bf(w[:, c3 + d:c3 + 2 * d]), bf(w_oa[i]), bf(w_ob[i]), bf(w_out[i])),
        ple=(row(g_ple[i]), bf(w_ple_gate[i]), bf(w_ple[i])),
    )


def kernel(x_prompt, x_sample, cache_a_k, cache_a_v, cache_mla_ckv, cache_mla_krope, p_prompt, p_sample, g_ffn1, w_ffn1_gate, w_ffn1_up, w_ffn1_down, g_mix, w_in, g_cq, w_uq, w_uk, w_uv, g_ckv, rel_bias, w_oa, w_ob, w_out, g_ffn2, w_ffn2_gate, w_ffn2_up, w_ffn2_down, g_ple, w_ple_gate, w_ple, g_final):
    b, s, d = x_prompt.shape
    bs, t, _ = x_sample.shape
    depth = g_ffn1.shape[0]
    past = cache_mla_ckv.shape[2]
    a_len = cache_a_k.shape[2]
    a_keep = min(BAND_PAST, s)

    cs_p, sn_p = _rope_tables(jnp.arange(s))
    cs_s, sn_s = _rope_tables(past + jnp.arange(t))
    cs_s, sn_s = jnp.tile(cs_s, (bs, 1)), jnp.tile(sn_s, (bs, 1))
    g_fin = g_final.reshape(1, d).astype(F32)

    hp = x_prompt.reshape(b * s, d)
    hs = x_sample.reshape(bs * t, d)
    outs = [[] for _ in range(8)]
    for i in range(depth):
        lw = _layer_weights(i, d, g_ffn1, w_ffn1_gate, w_ffn1_up, w_ffn1_down, g_mix, w_in, g_cq, w_uq, w_uk, w_uv,
                            g_ckv, rel_bias, w_oa, w_ob, w_out, g_ffn2, w_ffn2_gate, w_ffn2_up, w_ffn2_down, g_ple,
                            w_ple_gate, w_ple)

        hp = _ffn(hp, *lw["ffn1"])
        qa, ka, va, qn, qr, kn, vm, krp, ckv, kr, ka_tail, va_tail = _inproj(
            hp, lw["g_mix"], cs_p, sn_p, *lw["inproj"], rows_per_seq=s, tail_rows=a_keep)
        seq = lambda a: a.reshape(b, s, a.shape[-1])
        oa = _band_prompt(seq(qa), seq(ka), seq(va), lw["rb_rows"])
        ob = _mla_prompt(seq(qn), seq(qr), seq(kn), seq(krp), seq(vm))
        hp = _merge(hp, lw["g_mix"], oa.reshape(b * s, A_WIDTH), ob.reshape(b * s, MLA_WIDTH), *lw["merge"])
        outs[0].append(ka_tail.reshape(b, a_keep, A_HEADS, A_HEAD_DIM))
        outs[1].append(va_tail.reshape(b, a_keep, A_HEADS, A_HEAD_DIM))
        outs[2].append(ckv.reshape(b, s, KV_LORA))
        outs[3].append(kr.reshape(b, s, ROPE_DIM))

        hs = _ffn(hs, *lw["ffn1"])
        qa, ka, va, qn, qr, _, _, krp, ckv, kr, ka_new, va_new = _inproj(
            hs, lw["g_mix"], cs_s, sn_s, *lw["inproj"], rows_per_seq=t, tail_rows=t)
        new = lambda a: a.reshape(bs, t, a.shape[-1])
        oa = _band_sample(new(qa), new(ka), new(va), cache_a_k[i].reshape(bs, a_len, A_WIDTH),
                          cache_a_v[i].reshape(bs, a_len, A_WIDTH), lw["rb_rows"])
        ob = _mla_sample(new(qn), new(qr), new(ckv), new(krp), cache_mla_ckv[i], cache_mla_krope[i],
                         lw["w_uk"], lw["w_uv"])
        hs = _merge(hs, lw["g_mix"], oa.reshape(bs * t, A_WIDTH), ob.reshape(bs * t, MLA_WIDTH), *lw["merge"])
        outs[4].append(ka_new.reshape(bs, t, A_HEADS, A_HEAD_DIM))
        outs[5].append(va_new.reshape(bs, t, A_HEADS, A_HEAD_DIM))
        outs[6].append(ckv.reshape(bs, t, KV_LORA))
        outs[7].append(kr.reshape(bs, t, ROPE_DIM))

        hp = _ffn(hp, *lw["ffn2"])
        hs = _ffn(hs, *lw["ffn2"])
        last = i == depth - 1
        hp = _ple(hp, p_prompt[i].reshape(b * s, -1), *lw["ple"], g_fin, final_norm=last)
        hs = _ple(hs, p_sample[i].reshape(bs * t, -1), *lw["ple"], g_fin, final_norm=last)

    return (hp.reshape(b, s, d), hs.reshape(bs, t, d), *[jnp.stack(o) for o in outs])
```

```python
import functools

import jax
import jax.numpy as jnp
from jax import lax
from jax.experimental import pallas as pl
from jax.experimental.pallas import tpu as pltpu

F32 = jnp.float32
BF16 = jnp.bfloat16

CHUNK = 64
BAND_PAST = 512
A_HEADS = 8
A_HEAD_DIM = 128
A_WIDTH = A_HEADS * A_HEAD_DIM
REL_CLIP = 128
MLA_HEADS = 8
Q_LORA = 768
KV_LORA = 512
NOPE_DIM = 128
ROPE_DIM = 64
V_DIM = 128
MLA_WIDTH = MLA_HEADS * V_DIM
MLA_SCALE = (NOPE_DIM + ROPE_DIM) ** -0.5
A_SCALE = A_HEAD_DIM ** -0.5
ROPE_THETA = 10000.0
EPS = 1e-6
NEG = -1e30

LANES = 128
V7X_VMEM_BYTES = 64 * 1024 * 1024
VMEM_CEILING = V7X_VMEM_BYTES - 6 * 1024 * 1024

ROW_TILE = 512
INPROJ_ROW_TILE = 256
COL_TILE = 512
BAND_Q = 2 * CHUNK
BAND_K = BAND_PAST + BAND_Q
TOEPLITZ_W = 768
MLA_TILE = 512
BAND_UNROLL = 4


def _dot(a, b):
    return jnp.dot(a, b, preferred_element_type=F32)


def _dot_nt(a, b):
    return lax.dot_general(a, b, (((1,), (1,)), ((), ())), preferred_element_type=F32)


def _rms(x, g):
    return x * lax.rsqrt(jnp.mean(x * x, axis=-1, keepdims=True) + EPS) * g


def _vmem_limit(block_bytes, scratch_bytes, temp_bytes):
    return int(min(2 * block_bytes + scratch_bytes + temp_bytes + (4 << 20), VMEM_CEILING))


def _nbytes(shape, dtype):
    n = 1
    for s in shape:
        n *= s
    return n * jnp.dtype(dtype).itemsize


def _resident(shape):
    nd = len(shape)
    return pl.BlockSpec(shape, lambda *_: (0,) * nd, pipeline_mode=pl.Buffered(1))


def _col_tile(n):
    for t in (COL_TILE, 256, LANES):
        if n % t == 0:
            return t
    raise ValueError(f"width {n} is not a multiple of {LANES}")


def _ffn_body(x_ref, g_ref, wg_ref, wu_ref, wd_ref, o_ref, u_ref, acc_ref):
    j = pl.program_id(1)

    @pl.when(j == 0)
    def _():
        u_ref[...] = _rms(x_ref[...], g_ref[...]).astype(BF16)
        acc_ref[...] = jnp.zeros_like(acc_ref)

    u = u_ref[...]
    gate = _dot(u, wg_ref[...])
    up = _dot(u, wu_ref[...])
    act = (gate * jax.nn.sigmoid(gate) * up).astype(BF16)
    acc_ref[...] += _dot(act, wd_ref[...])

    @pl.when(j == pl.num_programs(1) - 1)
    def _():
        o_ref[...] = x_ref[...] + 0.5 * acc_ref[...]


def _ffn(x, g, wg, wu, wd):
    n, d = x.shape
    f = wg.shape[1]
    tm = min(ROW_TILE, n)
    tf = _col_tile(f)
    blocks = _nbytes((tm, d), F32) * 2 + 3 * _nbytes((d, tf), BF16)
    scratch = _nbytes((tm, d), BF16) + _nbytes((tm, d), F32)
    temps = 4 * _nbytes((tm, tf), F32) + _nbytes((tm, d), F32)
    return pl.pallas_call(
        _ffn_body,
        grid=(n // tm, f // tf),
        in_specs=[
            pl.BlockSpec((tm, d), lambda i, j: (i, 0)),
            pl.BlockSpec((1, d), lambda i, j: (0, 0)),
            pl.BlockSpec((d, tf), lambda i, j: (0, j)),
            pl.BlockSpec((d, tf), lambda i, j: (0, j)),
            pl.BlockSpec((tf, d), lambda i, j: (j, 0)),
        ],
        out_specs=pl.BlockSpec((tm, d), lambda i, j: (i, 0)),
        out_shape=jax.ShapeDtypeStruct((n, d), F32),
        scratch_shapes=[pltpu.VMEM((tm, d), BF16), pltpu.VMEM((tm, d), F32)],
        compiler_params=pltpu.CompilerParams(
            dimension_semantics=("arbitrary", "arbitrary"),
            vmem_limit_bytes=_vmem_limit(blocks, scratch, temps)),
        name="ffn",
    )(x, g, wg, wu, wd)


def _inproj_body(x_ref, g_ref, cs_ref, sn_ref, wa_ref, wc_ref, wk_ref, gcq_ref, gckv_ref, wuq_ref, wkv_ref,
                 qa_o, ka_o, va_o, qn_o, qr_o, kn_o, vm_o, krp_o, ckv_o, kr_o, kat_o, vat_o):
    u = _rms(x_ref[...], g_ref[...]).astype(BF16)

    qa_o[...] = (_dot(u, wa_ref[:, 0:A_WIDTH]) * A_SCALE).astype(BF16)
    ka = _dot(u, wa_ref[:, A_WIDTH:2 * A_WIDTH])
    ka_o[...] = ka.astype(BF16)
    kat_o[...] = ka
    va = _dot(u, wa_ref[:, 2 * A_WIDTH:3 * A_WIDTH])
    va_o[...] = va.astype(BF16)
    vat_o[...] = va

    cs = cs_ref[...]
    sn = sn_ref[...]

    cqn = _rms(_dot(u, wc_ref[...]), gcq_ref[...]).astype(BF16)
    qn_o[...] = (_dot(cqn, wuq_ref[:, 0:MLA_WIDTH]) * MLA_SCALE).astype(BF16)
    q_rot = _dot(cqn, wuq_ref[:, MLA_WIDTH:2 * MLA_WIDTH])
    q_swp = _dot(cqn, wuq_ref[:, 2 * MLA_WIDTH:3 * MLA_WIDTH])
    cs_h = jnp.concatenate([cs] * MLA_HEADS, axis=1)
    sn_h = jnp.concatenate([sn] * MLA_HEADS, axis=1)
    qr_o[...] = ((q_rot * cs_h + q_swp * sn_h) * MLA_SCALE).astype(BF16)

    zc = _dot(u, wk_ref[...])
    ckvn = _rms(zc[:, 0:KV_LORA], gckv_ref[...])
    ckv_o[...] = ckvn
    krp = zc[:, KV_LORA:KV_LORA + LANES] * cs + zc[:, KV_LORA + LANES:KV_LORA + 2 * LANES] * sn
    krp_o[...] = krp.astype(BF16)
    kr_o[...] = krp[:, 0:ROPE_DIM]

    kv = _dot(ckvn.astype(BF16), wkv_ref[...])
    kn_o[...] = kv[:, 0:MLA_WIDTH].astype(BF16)
    vm_o[...] = kv[:, MLA_WIDTH:2 * MLA_WIDTH].astype(BF16)


def _inproj(x, g, cs, sn, wa, wc, wk, gcq, gckv, wuq, wkv, *, rows_per_seq, tail_rows):
    n, d = x.shape
    tm = min(INPROJ_ROW_TILE, n)
    n_tiles = n // tm
    tab_tiles = cs.shape[0] // tm
    if tm <= rows_per_seq:
        tiles_per_seq = rows_per_seq // tm
        tail_tiles = tail_rows // tm
        n_seq = n // rows_per_seq

        def tail_map(i):
            return (i // tiles_per_seq) * tail_tiles + jnp.maximum(i % tiles_per_seq - (tiles_per_seq - tail_tiles), 0), 0

        tail_n = n_seq * tail_rows
    else:
        assert tail_rows == rows_per_seq

        def tail_map(i):
            return i, 0

        tail_n = n

    def row(i):
        return i, 0

    def tab(i):
        return i % tab_tiles, 0

    wide = pl.BlockSpec((tm, A_WIDTH), row)
    out_shapes = [jax.ShapeDtypeStruct((n, A_WIDTH), BF16)] * 7 + [
        jax.ShapeDtypeStruct((n, LANES), BF16),
        jax.ShapeDtypeStruct((n, KV_LORA), F32),
        jax.ShapeDtypeStruct((n, ROPE_DIM), F32),
        jax.ShapeDtypeStruct((tail_n, A_WIDTH), F32),
        jax.ShapeDtypeStruct((tail_n, A_WIDTH), F32),
    ]
    out_specs = [wide] * 7 + [
        pl.BlockSpec((tm, LANES), row),
        pl.BlockSpec((tm, KV_LORA), row),
        pl.BlockSpec((tm, ROPE_DIM), row),
        pl.BlockSpec((tm, A_WIDTH), tail_map),
        pl.BlockSpec((tm, A_WIDTH), tail_map),
    ]
    weights = [wa, wc, wk, gcq, gckv, wuq, wkv]
    blocks = (_nbytes((tm, d), F32) + 7 * _nbytes((tm, A_WIDTH), BF16) + 3 * _nbytes((tm, A_WIDTH), F32)
              + 4 * _nbytes((tm, LANES), F32))
    resident = sum(_nbytes(w.shape, w.dtype) for w in weights)
    temps = 6 * _nbytes((tm, 2 * A_WIDTH), F32)
    return pl.pallas_call(
        _inproj_body,
        grid=(n_tiles,),
        in_specs=[
            pl.BlockSpec((tm, d), row),
            _resident((1, d)),
            pl.BlockSpec((tm, LANES), tab),
            pl.BlockSpec((tm, LANES), tab),
        ] + [_resident(w.shape) for w in weights],
        out_specs=out_specs,
        out_shape=out_shapes,
        compiler_params=pltpu.CompilerParams(
            dimension_semantics=("arbitrary",),
            vmem_limit_bytes=_vmem_limit(blocks, resident, temps)),
        name="inproj",
    )(x, g, cs, sn, *weights)


def _toeplitz(row, nrows):
    x = jnp.broadcast_to(row, (nrows, TOEPLITZ_W))
    left = (LANES - 1) - lax.broadcasted_iota(jnp.int32, (nrows, TOEPLITZ_W), 0)
    for b in range(7):
        rolled = pltpu.roll(x, TOEPLITZ_W - (1 << b), axis=1)
        x = jnp.where(((left >> b) & 1) == 1, rolled, x)
    return x


def _softmax_pv(s, v):
    m = jnp.max(s, axis=-1, keepdims=True)
    p = jnp.exp(s - m)
    l = jnp.sum(p, axis=-1, keepdims=True)
    return _dot(p.astype(BF16), v) / l


def _band_body(rb_ref, q_ref, k_ref, v_ref, o_ref, t_ref, *, n_tiles):
    @pl.when(pl.program_id(1) == 0)
    def _():
        t = _toeplitz(rb_ref[...], BAND_Q)[:, 0:BAND_K]
        i = lax.broadcasted_iota(jnp.int32, (BAND_Q, BAND_K), 0)
        j = lax.broadcasted_iota(jnp.int32, (BAND_Q, BAND_K), 1)
        visible = ((i < CHUNK) & (j < BAND_PAST + CHUNK)) | ((i >= CHUNK) & (j >= CHUNK))
        t_ref[...] = jnp.where(visible, t, NEG)

    past_tiles = BAND_PAST // BAND_Q
    for t in range(past_tiles):
        nk = (t + 1) * BAND_Q
        s = _dot_nt(q_ref[t * BAND_Q:(t + 1) * BAND_Q, :], k_ref[0:nk, :]) + t_ref[:, BAND_K - nk:BAND_K]
        o_ref[t * BAND_Q:(t + 1) * BAND_Q, :] = _softmax_pv(s, v_ref[0:nk, :]).astype(BF16)

    def tile(t, carry):
        r0 = pl.multiple_of(t * BAND_Q, BAND_Q)
        k0 = pl.multiple_of(t * BAND_Q - BAND_PAST, BAND_Q)
        s = _dot_nt(q_ref[pl.ds(r0, BAND_Q), :], k_ref[pl.ds(k0, BAND_K), :]) + t_ref[...]
        o_ref[pl.ds(r0, BAND_Q), :] = _softmax_pv(s, v_ref[pl.ds(k0, BAND_K), :]).astype(BF16)
        return carry

    lax.fori_loop(past_tiles, n_tiles, tile, 0, unroll=BAND_UNROLL)


def _band_prompt(q, k, v, rb_rows):
    b, s, _ = q.shape
    assert s % BAND_Q == 0 and s >= BAND_K
    head = pl.BlockSpec((None, s, A_HEAD_DIM), lambda h, i: (i, 0, h))
    blocks = 4 * _nbytes((s, A_HEAD_DIM), BF16)
    scratch = _nbytes((BAND_Q, BAND_K), F32)
    temps = 16 * _nbytes((BAND_Q, TOEPLITZ_W), F32)
    return pl.pallas_call(
        functools.partial(_band_body, n_tiles=s // BAND_Q),
        grid=(A_HEADS, b),
        in_specs=[pl.BlockSpec((None, 1, TOEPLITZ_W), lambda h, i: (h, 0, 0)), head, head, head],
        out_specs=head,
        out_shape=jax.ShapeDtypeStruct(q.shape, BF16),
        scratch_shapes=[pltpu.VMEM((BAND_Q, BAND_K), F32)],
        compiler_params=pltpu.CompilerParams(
            dimension_semantics=("arbitrary", "arbitrary"),
            vmem_limit_bytes=_vmem_limit(blocks, scratch, temps)),
        name="band_prompt",
    )(rb_rows, q, k, v)


def _band_sample_body(rb_ref, q_ref, kn_ref, vn_ref, ck_ref, cv_ref, o_ref):
    t_new = q_ref.shape[0]
    n_cache = ck_ref.shape[0]
    for h in range(A_HEADS):
        cols = slice(h * A_HEAD_DIM, (h + 1) * A_HEAD_DIM)
        bias = _toeplitz(rb_ref[h], t_new)
        q = q_ref[:, cols]
        s_c = _dot_nt(q, ck_ref[:, cols].astype(BF16)) + bias[:, BAND_PAST - n_cache:BAND_PAST]
        s_n = _dot_nt(q, kn_ref[:, cols]) + bias[:, BAND_PAST:BAND_PAST + t_new]
        m = jnp.maximum(jnp.max(s_c, axis=-1, keepdims=True), jnp.max(s_n, axis=-1, keepdims=True))
        p_c = jnp.exp(s_c - m)
        p_n = jnp.exp(s_n - m)
        l = jnp.sum(p_c, axis=-1, keepdims=True) + jnp.sum(p_n, axis=-1, keepdims=True)
        o = _dot(p_c.astype(BF16), cv_ref[:, cols].astype(BF16)) + _dot(p_n.astype(BF16), vn_ref[:, cols])
        o_ref[:, cols] = (o / l).astype(BF16)


def _band_sample(q, k_new, v_new, cache_k, cache_v, layer, rb_rows):
    b, t, w = q.shape
    a = cache_k.shape[2]
    assert t <= LANES and a <= BAND_PAST
    new = pl.BlockSpec((None, t, w), lambda i: (i, 0, 0))
    cache = pl.BlockSpec((None, None, a, w), lambda i: (layer, i, 0, 0))
    blocks = 4 * _nbytes((t, w), BF16) + 2 * _nbytes((a, w), F32)
    temps = 2 * _nbytes((a, w), F32) + 16 * _nbytes((t, TOEPLITZ_W), F32)
    return pl.pallas_call(
        _band_sample_body,
        grid=(b,),
        in_specs=[_resident(rb_rows.shape), new, new, new, cache, cache],
        out_specs=new,
        out_shape=jax.ShapeDtypeStruct(q.shape, BF16),
        compiler_params=pltpu.CompilerParams(
            dimension_semantics=("arbitrary",),
            vmem_limit_bytes=_vmem_limit(blocks, _nbytes(rb_rows.shape, F32), temps)),
        name="band_sample",
    )(rb_rows, q, k_new, v_new, cache_k, cache_v)


def _mla_body(qn_ref, qr_ref, kn_ref, kr_ref, v_ref, o_ref, *, n_tiles):
    r = lax.broadcasted_iota(jnp.int32, (MLA_TILE, MLA_TILE), 0)
    c = lax.broadcasted_iota(jnp.int32, (MLA_TILE, MLA_TILE), 1)
    chunk_bits = CHUNK.bit_length() - 1
    diag_visible = lax.shift_right_logical(c, chunk_bits) <= lax.shift_right_logical(r, chunk_bits)

    def keys(c0):
        return jnp.concatenate([kn_ref[pl.ds(c0, MLA_TILE), :], kr_ref[pl.ds(c0, MLA_TILE), :]], axis=1)

    def q_tile(i, carry):
        r0 = pl.multiple_of(i * MLA_TILE, MLA_TILE)
        q = jnp.concatenate([qn_ref[pl.ds(r0, MLA_TILE), :], qr_ref[pl.ds(r0, MLA_TILE), :]], axis=1)

        s = jnp.where(diag_visible, _dot_nt(q, keys(r0)), NEG)
        m = jnp.max(s, axis=-1, keepdims=True)
        p = jnp.exp(s - m)
        l = jnp.sum(p, axis=-1, keepdims=True)
        acc = _dot(p.astype(BF16), v_ref[pl.ds(r0, MLA_TILE), :])

        def kv_tile(j, state):
            m, l, acc = state
            c0 = pl.multiple_of(j * MLA_TILE, MLA_TILE)
            s = _dot_nt(q, keys(c0))
            m_new = jnp.maximum(m, jnp.max(s, axis=-1, keepdims=True))
            alpha = jnp.exp(m - m_new)
            p = jnp.exp(s - m_new)
            l = alpha * l + jnp.sum(p, axis=-1, keepdims=True)
            acc = alpha * acc + _dot(p.astype(BF16), v_ref[pl.ds(c0, MLA_TILE), :])
            return m_new, l, acc

        m, l, acc = lax.fori_loop(0, i, kv_tile, (m, l, acc))
        o_ref[pl.ds(r0, MLA_TILE), :] = (acc / l).astype(BF16)
        return carry

    lax.fori_loop(0, n_tiles, q_tile, 0)


def _mla_prompt(qn, qr, kn, kr, v):
    b, s, _ = qn.shape
    assert s % MLA_TILE == 0
    head = pl.BlockSpec((None, s, LANES), lambda i, h: (i, 0, h))
    shared = pl.BlockSpec((None, s, LANES), lambda i, h: (i, 0, 0))
    blocks = 6 * _nbytes((s, LANES), BF16)
    temps = 16 * _nbytes((MLA_TILE, MLA_TILE), F32)
    return pl.pallas_call(
        functools.partial(_mla_body, n_tiles=s // MLA_TILE),
        grid=(b, MLA_HEADS),
        in_specs=[head, head, head, shared, head],
        out_specs=head,
        out_shape=jax.ShapeDtypeStruct(qn.shape, BF16),
        compiler_params=pltpu.CompilerParams(
            dimension_semantics=("arbitrary", "arbitrary"),
            vmem_limit_bytes=_vmem_limit(blocks, 0, temps)),
        name="mla_prompt",
    )(qn, qr, kn, kr, v)


def _mla_sample_body(qn_ref, qr_ref, cn_ref, krn_ref, cc_ref, ckr_ref, wuk_ref, wuv_ref, o_ref):
    t_new = qn_ref.shape[0]
    q_lat = jnp.concatenate(
        [_dot(qn_ref[:, h * NOPE_DIM:(h + 1) * NOPE_DIM], wuk_ref[h]).astype(BF16) for h in range(MLA_HEADS)], axis=0)
    q_rot = jnp.concatenate(
        [qr_ref[:, h * LANES:h * LANES + ROPE_DIM] for h in range(MLA_HEADS)], axis=0)
    cache = cc_ref[...].astype(BF16)
    new = cn_ref[...].astype(BF16)
    s_c = _dot_nt(q_lat, cache) + _dot_nt(q_rot, ckr_ref[...].astype(BF16))
    s_n = _dot_nt(q_lat, new) + _dot_nt(q_rot, krn_ref[:, 0:ROPE_DIM])
    m = jnp.maximum(jnp.max(s_c, axis=-1, keepdims=True), jnp.max(s_n, axis=-1, keepdims=True))
    p_c = jnp.exp(s_c - m)
    p_n = jnp.exp(s_n - m)
    l = jnp.sum(p_c, axis=-1, keepdims=True) + jnp.sum(p_n, axis=-1, keepdims=True)
    o_lat = ((_dot(p_c.astype(BF16), cache) + _dot(p_n.astype(BF16), new)) / l).astype(BF16)
    for h in range(MLA_HEADS):
        o_ref[:, h * V_DIM:(h + 1) * V_DIM] = _dot(o_lat[h * t_new:(h + 1) * t_new, :], wuv_ref[h]).astype(BF16)


def _mla_sample(qn, qr, ckv_new, kr_new, cache_ckv, cache_kr, layer, wuk, wuv):
    b, t, w = qn.shape
    past = cache_ckv.shape[2]

    def per_batch(shape):
        return pl.BlockSpec((None,) + shape, lambda i: (i, 0, 0))

    def cached(shape):
        return pl.BlockSpec((None, None) + shape, lambda i: (layer, i, 0, 0))

    blocks = (2 * _nbytes((t, w), BF16) + _nbytes((t, KV_LORA), F32) + _nbytes((t, LANES), BF16)
              + _nbytes((past, KV_LORA), F32) + _nbytes((past, ROPE_DIM), F32) + _nbytes((t, w), BF16))
    resident = _nbytes(wuk.shape, BF16) + _nbytes(wuv.shape, BF16)
    temps = _nbytes((past, KV_LORA), F32) + 4 * _nbytes((MLA_HEADS * t, past), F32)
    return pl.pallas_call(
        _mla_sample_body,
        grid=(b,),
        in_specs=[per_batch((t, w)), per_batch((t, w)), per_batch((t, KV_LORA)), per_batch((t, LANES)),
                  cached((past, KV_LORA)), cached((past, ROPE_DIM)),
                  _resident(wuk.shape), _resident(wuv.shape)],
        out_specs=per_batch((t, w)),
        out_shape=jax.ShapeDtypeStruct(qn.shape, BF16),
        compiler_params=pltpu.CompilerParams(
            dimension_semantics=("arbitrary",),
            vmem_limit_bytes=_vmem_limit(blocks, resident, temps)),
        name="mla_sample",
    )(qn, qr, ckv_new, kr_new, cache_ckv, cache_kr, wuk, wuv)


def _merge_body(h_ref, g_ref, oa_ref, ob_ref, wga_ref, wgb_ref, woa_ref, wob_ref, wout_ref, o_ref, u_ref, acc_ref):
    j = pl.program_id(1)

    @pl.when(j == 0)
    def _():
        u_ref[...] = _rms(h_ref[...], g_ref[...]).astype(BF16)
        acc_ref[...] = jnp.zeros_like(acc_ref)

    u = u_ref[...]
    m = (jax.nn.sigmoid(_dot(u, wga_ref[...])) * _dot(oa_ref[...], woa_ref[...])
         + jax.nn.sigmoid(_dot(u, wgb_ref[...])) * _dot(ob_ref[...], wob_ref[...]))
    acc_ref[...] += _dot(m.astype(BF16), wout_ref[...])

    @pl.when(j == pl.num_programs(1) - 1)
    def _():
        o_ref[...] = h_ref[...] + acc_ref[...]


def _merge(h, g, oa, ob, wga, wgb, woa, wob, wout):
    n, d = h.shape
    tm = min(ROW_TILE, n)
    tn = _col_tile(d)
    blocks = (2 * _nbytes((tm, d), F32) + 2 * _nbytes((tm, A_WIDTH), BF16) + 2 * _nbytes((d, tn), BF16)
              + 2 * _nbytes((A_WIDTH, tn), BF16) + _nbytes((tn, d), BF16))
    scratch = _nbytes((tm, d), BF16) + _nbytes((tm, d), F32)
    temps = 6 * _nbytes((tm, tn), F32) + _nbytes((tm, d), F32)
    return pl.pallas_call(
        _merge_body,
        grid=(n // tm, d // tn),
        in_specs=[
            pl.BlockSpec((tm, d), lambda i, j: (i, 0)),
            pl.BlockSpec((1, d), lambda i, j: (0, 0)),
            pl.BlockSpec((tm, A_WIDTH), lambda i, j: (i, 0)),
            pl.BlockSpec((tm, MLA_WIDTH), lambda i, j: (i, 0)),
            pl.BlockSpec((d, tn), lambda i, j: (0, j)),
            pl.BlockSpec((d, tn), lambda i, j: (0, j)),
            pl.BlockSpec((A_WIDTH, tn), lambda i, j: (0, j)),
            pl.BlockSpec((MLA_WIDTH, tn), lambda i, j: (0, j)),
            pl.BlockSpec((tn, d), lambda i, j: (j, 0)),
        ],
        out_specs=pl.BlockSpec((tm, d), lambda i, j: (i, 0)),
        out_shape=jax.ShapeDtypeStruct((n, d), F32),
        scratch_shapes=[pltpu.VMEM((tm, d), BF16), pltpu.VMEM((tm, d), F32)],
        compiler_params=pltpu.CompilerParams(
            dimension_semantics=("arbitrary", "arbitrary"),
            vmem_limit_bytes=_vmem_limit(blocks, scratch, temps)),
        name="merge",
    )(h, g, oa, ob, wga, wgb, woa, wob, wout)


def _ple_body(h_ref, p_ref, gp_ref, wpg_ref, wp_ref, gf_ref, y_ref, *, final_norm):
    h = h_ref[...]
    gate = jax.nn.sigmoid(_dot(_rms(h, gp_ref[...]).astype(BF16), wpg_ref[...]))
    h = h + gate * _dot(p_ref[...].astype(BF16), wp_ref[...])
    y_ref[...] = _rms(h, gf_ref[...]) if final_norm else h


def _ple(h, p, layer, gp, wpg, wp, gf, *, final_norm):
    n, d = h.shape
    e = p.shape[2]
    tm = min(ROW_TILE, n)
    blocks = 2 * _nbytes((tm, d), F32) + _nbytes((tm, e), F32)
    resident = _nbytes(wpg.shape, BF16) + _nbytes(wp.shape, BF16)
    temps = 4 * _nbytes((tm, d), F32)
    return pl.pallas_call(
        functools.partial(_ple_body, final_norm=final_norm),
        grid=(n // tm,),
        in_specs=[
            pl.BlockSpec((tm, d), lambda i: (i, 0)),
            pl.BlockSpec((None, tm, e), lambda i: (layer, i, 0)),
            _resident((1, d)),
            _resident(wpg.shape),
            _resident(wp.shape),
            _resident((1, d)),
        ],
        out_specs=pl.BlockSpec((tm, d), lambda i: (i, 0)),
        out_shape=jax.ShapeDtypeStruct((n, d), F32),
        compiler_params=pltpu.CompilerParams(
            dimension_semantics=("arbitrary",),
            vmem_limit_bytes=_vmem_limit(blocks, resident, temps)),
        name="ple",
    )(h, p, gp, wpg, wp, gf)


def _rope_tables(pos):
    half = ROPE_DIM // 2
    inv = ROPE_THETA ** (-jnp.arange(half, dtype=F32) / half)
    ang = pos.astype(F32)[:, None] * inv[None, :]
    cos, sin = jnp.cos(ang), jnp.sin(ang)
    zero = jnp.zeros((pos.shape[0], LANES - ROPE_DIM), F32)
    return jnp.concatenate([cos, cos, zero], axis=1), jnp.concatenate([-sin, sin, zero], axis=1)


def _pad_swap(w, lead):
    half = ROPE_DIM // 2
    x1, x2 = w[..., :half], w[..., half:]
    zero = jnp.zeros(lead + (LANES - ROPE_DIM,), w.dtype)
    return jnp.concatenate([x1, x2, zero], axis=-1), jnp.concatenate([x2, x1, zero], axis=-1)


def _layer_weights(i, d, g_ffn1, w_ffn1_gate, w_ffn1_up, w_ffn1_down, g_mix, w_in, g_cq, w_uq, w_uk, w_uv, g_ckv,
                   rel_bias, w_oa, w_ob, w_out, g_ffn2, w_ffn2_gate, w_ffn2_up, w_ffn2_down, g_ple, w_ple_gate, w_ple):
    bf = lambda a: a.astype(BF16)
    row = lambda a: a.reshape(1, -1).astype(F32)
    w = w_in[i]
    c0 = 3 * A_WIDTH
    c1 = c0 + Q_LORA
    c2 = c1 + KV_LORA
    c3 = c2 + ROPE_DIM
    kr_rot, kr_swp = _pad_swap(w[:, c2:c3], (d,))
    uq = w_uq[i].reshape(Q_LORA, MLA_HEADS, NOPE_DIM + ROPE_DIM)
    uq_rot, uq_swp = _pad_swap(uq[:, :, NOPE_DIM:], (Q_LORA, MLA_HEADS))
    rel = jnp.clip(jnp.arange(TOEPLITZ_W) - (BAND_PAST + LANES - 1), -REL_CLIP, REL_CLIP) + REL_CLIP
    return dict(
        ffn1=(row(g_ffn1[i]), bf(w_ffn1_gate[i]), bf(w_ffn1_up[i]), bf(w_ffn1_down[i])),
        ffn2=(row(g_ffn2[i]), bf(w_ffn2_gate[i]), bf(w_ffn2_up[i]), bf(w_ffn2_down[i])),
        g_mix=row(g_mix[i]),
        inproj=(
            bf(w[:, 0:c0]),
            bf(w[:, c0:c1]),
            bf(jnp.concatenate([w[:, c1:c2], kr_rot, kr_swp], axis=1)),
            row(g_cq[i]),
            row(g_ckv[i]),
            bf(jnp.concatenate([uq[:, :, :NOPE_DIM].reshape(Q_LORA, MLA_WIDTH), uq_rot.reshape(Q_LORA, MLA_WIDTH),
                                uq_swp.reshape(Q_LORA, MLA_WIDTH)], axis=1)),
            bf(jnp.concatenate([jnp.transpose(w_uk[i], (2, 0, 1)).reshape(KV_LORA, MLA_WIDTH),
                                jnp.transpose(w_uv[i], (1, 0, 2)).reshape(KV_LORA, MLA_WIDTH)], axis=1)),
        ),
        rb_rows=rel_bias[i][:, rel].reshape(A_HEADS, 1, TOEPLITZ_W).astype(F32),
        w_uk=bf(w_uk[i]),
        w_uv=bf(w_uv[i]),
        merge=(bf(w[:, c3:c3 + d]), bf(w[:, c3 + d:c3 + 2 * d]), bf(w_oa[i]), bf(w_ob[i]), bf(w_out[i])),
        ple=(row(g_ple[i]), bf(w_ple_gate[i]), bf(w_ple[i])),
    )


def kernel(x_prompt, x_sample, cache_a_k, cache_a_v, cache_mla_ckv, cache_mla_krope, p_prompt, p_sample, g_ffn1, w_ffn1_gate, w_ffn1_up, w_ffn1_down, g_mix, w_in, g_cq, w_uq, w_uk, w_uv, g_ckv, rel_bias, w_oa, w_ob, w_out, g_ffn2, w_ffn2_gate, w_ffn2_up, w_ffn2_down, g_ple, w_ple_gate, w_ple, g_final):
    b, s, d = x_prompt.shape
    bs, t, _ = x_sample.shape
    depth = g_ffn1.shape[0]
    past = cache_mla_ckv.shape[2]
    a_len = cache_a_k.shape[2]
    a_keep = min(BAND_PAST, s)

    cs_p, sn_p = _rope_tables(jnp.arange(s))
    cs_s, sn_s = _rope_tables(past + jnp.arange(t))
    cs_s, sn_s = jnp.tile(cs_s, (bs, 1)), jnp.tile(sn_s, (bs, 1))
    g_fin = g_final.reshape(1, d).astype(F32)

    hp = x_prompt.reshape(b * s, d)
    hs = x_sample.reshape(bs * t, d)
    outs = [[] for _ in range(8)]
    for i in range(depth):
        lw = _layer_weights(i, d, g_ffn1, w_ffn1_gate, w_ffn1_up, w_ffn1_down, g_mix, w_in, g_cq, w_uq, w_uk, w_uv,
                            g_ckv, rel_bias, w_oa, w_ob, w_out, g_ffn2, w_ffn2_gate, w_ffn2_up, w_ffn2_down, g_ple,
                            w_ple_gate, w_ple)

        hp = _ffn(hp, *lw["ffn1"])
        qa, ka, va, qn, qr, kn, vm, krp, ckv, kr, ka_tail, va_tail = _inproj(
            hp, lw["g_mix"], cs_p, sn_p, *lw["inproj"], rows_per_seq=s, tail_rows=a_keep)
        seq = lambda a: a.reshape(b, s, a.shape[-1])
        oa = _band_prompt(seq(qa), seq(ka), seq(va), lw["rb_rows"])
        ob = _mla_prompt(seq(qn), seq(qr), seq(kn), seq(krp), seq(vm))
        hp = _merge(hp, lw["g_mix"], oa.reshape(b * s, A_WIDTH), ob.reshape(b * s, MLA_WIDTH), *lw["merge"])
        outs[0].append(ka_tail.reshape(b, a_keep, A_HEADS, A_HEAD_DIM))
        outs[1].append(va_tail.reshape(b, a_keep, A_HEADS, A_HEAD_DIM))
        outs[2].append(ckv.reshape(b, s, KV_LORA))
        outs[3].append(kr.reshape(b, s, ROPE_DIM))

        hs = _ffn(hs, *lw["ffn1"])
        qa, ka, va, qn, qr, _, _, krp, ckv, kr, ka_new, va_new = _inproj(
            hs, lw["g_mix"], cs_s, sn_s, *lw["inproj"], rows_per_seq=t, tail_rows=t)
        new = lambda a: a.reshape(bs, t, a.shape[-1])
        oa = _band_sample(new(qa), new(ka), new(va), cache_a_k.reshape(depth, bs, a_len, A_WIDTH),
                          cache_a_v.reshape(depth, bs, a_len, A_WIDTH), i, lw["rb_rows"])
        ob = _mla_sample(new(qn), new(qr), new(ckv), new(krp), cache_mla_ckv, cache_mla_krope, i,
                         lw["w_uk"], lw["w_uv"])
        hs = _merge(hs, lw["g_mix"], oa.reshape(bs * t, A_WIDTH), ob.reshape(bs * t, MLA_WIDTH), *lw["merge"])
        outs[4].append(ka_new.reshape(bs, t, A_HEADS, A_HEAD_DIM))
        outs[5].append(va_new.reshape(bs, t, A_HEADS, A_HEAD_DIM))
        outs[6].append(ckv.reshape(bs, t, KV_LORA))
        outs[7].append(kr.reshape(bs, t, ROPE_DIM))

        hp = _ffn(hp, *lw["ffn2"])
        hs = _ffn(hs, *lw["ffn2"])
        last = i == depth - 1
        hp = _ple(hp, p_prompt.reshape(depth, b * s, -1), i, *lw["ple"], g_fin, final_norm=last)
        hs = _ple(hs, p_sample.reshape(depth, bs * t, -1), i, *lw["ple"], g_fin, final_norm=last)

    return (hp.reshape(b, s, d), hs.reshape(bs, t, d), *[jnp.stack(o) for o in outs])
```

```python
import functools

import jax
import jax.numpy as jnp
from jax import lax
from jax.experimental import pallas as pl
from jax.experimental.pallas import tpu as pltpu

F32 = jnp.float32
BF16 = jnp.bfloat16

CHUNK = 64
BAND_PAST = 512
A_HEADS = 8
A_HEAD_DIM = 128
A_WIDTH = A_HEADS * A_HEAD_DIM
REL_CLIP = 128
MLA_HEADS = 8
Q_LORA = 768
KV_LORA = 512
NOPE_DIM = 128
ROPE_DIM = 64
V_DIM = 128
MLA_WIDTH = MLA_HEADS * V_DIM
LOG2E = 1.4426950408889634
MLA_SCALE = (NOPE_DIM + ROPE_DIM) ** -0.5 * LOG2E
A_SCALE = A_HEAD_DIM ** -0.5 * LOG2E
ROPE_THETA = 10000.0
EPS = 1e-6
NEG = -1e30

LANES = 128
V7X_VMEM_BYTES = 64 * 1024 * 1024
VMEM_CEILING = V7X_VMEM_BYTES - 6 * 1024 * 1024

ROW_TILE = 512
INPROJ_ROW_TILE = 256
COL_TILE = 512
BAND_Q = 2 * CHUNK
BAND_K = BAND_PAST + BAND_Q
TOEPLITZ_W = 768
MLA_TILE = 512

def _dot(a, b):
    return jnp.dot(a, b, preferred_element_type=F32)


def _dot_nt(a, b):
    return lax.dot_general(a, b, (((1,), (1,)), ((), ())), preferred_element_type=F32)


def _rms(x, g):
    return x * lax.rsqrt(jnp.mean(x * x, axis=-1, keepdims=True) + EPS) * g


def _vmem_limit(block_bytes, scratch_bytes, temp_bytes):
    return int(min(2 * block_bytes + scratch_bytes + temp_bytes + (4 << 20), VMEM_CEILING))


def _nbytes(shape, dtype):
    n = 1
    for s in shape:
        n *= s
    return n * jnp.dtype(dtype).itemsize


def _resident(shape):
    nd = len(shape)
    return pl.BlockSpec(shape, lambda *_: (0,) * nd, pipeline_mode=pl.Buffered(1))


def _col_tile(n):
    for t in (COL_TILE, 256, LANES):
        if n % t == 0:
            return t
    raise ValueError(f"width {n} is not a multiple of {LANES}")


def _ffn_body(x_ref, g_ref, wg_ref, wu_ref, wd_ref, o_ref, u_ref, acc_ref):
    j = pl.program_id(1)

    @pl.when(j == 0)
    def _():
        u_ref[...] = _rms(x_ref[...], g_ref[...]).astype(BF16)
        acc_ref[...] = jnp.zeros_like(acc_ref)

    u = u_ref[...]
    gate = _dot(u, wg_ref[...])
    up = _dot(u, wu_ref[...])
    act = (gate * jax.nn.sigmoid(gate) * up).astype(BF16)
    acc_ref[...] += _dot(act, wd_ref[...])

    @pl.when(j == pl.num_programs(1) - 1)
    def _():
        o_ref[...] = x_ref[...] + 0.5 * acc_ref[...]


def _ffn(x, g, wg, wu, wd):
    n, d = x.shape
    f = wg.shape[1]
    tm = min(ROW_TILE, n)
    tf = _col_tile(f)
    blocks = _nbytes((tm, d), F32) * 2 + 3 * _nbytes((d, tf), BF16)
    scratch = _nbytes((tm, d), BF16) + _nbytes((tm, d), F32)
    temps = 4 * _nbytes((tm, tf), F32) + _nbytes((tm, d), F32)
    return pl.pallas_call(
        _ffn_body,
        grid=(n // tm, f // tf),
        in_specs=[
            pl.BlockSpec((tm, d), lambda i, j: (i, 0)),
            pl.BlockSpec((1, d), lambda i, j: (0, 0)),
            pl.BlockSpec((d, tf), lambda i, j: (0, j)),
            pl.BlockSpec((d, tf), lambda i, j: (0, j)),
            pl.BlockSpec((tf, d), lambda i, j: (j, 0)),
        ],
        out_specs=pl.BlockSpec((tm, d), lambda i, j: (i, 0)),
        out_shape=jax.ShapeDtypeStruct((n, d), F32),
        scratch_shapes=[pltpu.VMEM((tm, d), BF16), pltpu.VMEM((tm, d), F32)],
        compiler_params=pltpu.CompilerParams(
            dimension_semantics=("arbitrary", "arbitrary"),
            vmem_limit_bytes=_vmem_limit(blocks, scratch, temps)),
        name="ffn",
    )(x, g, wg, wu, wd)


def _inproj_body(x_ref, g_ref, cs_ref, sn_ref, wa_ref, wc_ref, wk_ref, gcq_ref, gckv_ref, wuq_ref, wkv_ref,
                 qa_o, ka_o, va_o, qn_o, qr_o, kn_o, vm_o, krp_o, ckv_o, kr_o, kat_o, vat_o):
    u = _rms(x_ref[...], g_ref[...]).astype(BF16)

    qa_o[...] = (_dot(u, wa_ref[:, 0:A_WIDTH]) * A_SCALE).astype(BF16)
    ka = _dot(u, wa_ref[:, A_WIDTH:2 * A_WIDTH])
    ka_o[...] = ka.astype(BF16)
    kat_o[...] = ka
    va = _dot(u, wa_ref[:, 2 * A_WIDTH:3 * A_WIDTH])
    va_o[...] = va.astype(BF16)
    vat_o[...] = va

    cs = cs_ref[...]
    sn = sn_ref[...]

    cqn = _rms(_dot(u, wc_ref[...]), gcq_ref[...]).astype(BF16)
    qn_o[...] = (_dot(cqn, wuq_ref[:, 0:MLA_WIDTH]) * MLA_SCALE).astype(BF16)
    q_rot = _dot(cqn, wuq_ref[:, MLA_WIDTH:2 * MLA_WIDTH])
    q_swp = _dot(cqn, wuq_ref[:, 2 * MLA_WIDTH:3 * MLA_WIDTH])
    cs_h = jnp.concatenate([cs] * MLA_HEADS, axis=1)
    sn_h = jnp.concatenate([sn] * MLA_HEADS, axis=1)
    qr_o[...] = ((q_rot * cs_h + q_swp * sn_h) * MLA_SCALE).astype(BF16)

    zc = _dot(u, wk_ref[...])
    ckvn = _rms(zc[:, 0:KV_LORA], gckv_ref[...])
    ckv_o[...] = ckvn
    krp = zc[:, KV_LORA:KV_LORA + LANES] * cs + zc[:, KV_LORA + LANES:KV_LORA + 2 * LANES] * sn
    krp_o[...] = krp.astype(BF16)
    kr_o[...] = krp[:, 0:ROPE_DIM]

    kv = _dot(ckvn.astype(BF16), wkv_ref[...])
    kn_o[...] = kv[:, 0:MLA_WIDTH].astype(BF16)
    vm_o[...] = kv[:, MLA_WIDTH:2 * MLA_WIDTH].astype(BF16)


def _inproj(x, g, cs, sn, wa, wc, wk, gcq, gckv, wuq, wkv, *, rows_per_seq, tail_rows):
    n, d = x.shape
    tm = min(INPROJ_ROW_TILE, n)
    n_tiles = n // tm
    tab_tiles = cs.shape[0] // tm
    if tm <= rows_per_seq:
        tiles_per_seq = rows_per_seq // tm
        tail_tiles = tail_rows // tm
        n_seq = n // rows_per_seq

        def tail_map(i):
            return (i // tiles_per_seq) * tail_tiles + jnp.maximum(i % tiles_per_seq - (tiles_per_seq - tail_tiles), 0), 0

        tail_n = n_seq * tail_rows
    else:
        assert tail_rows == rows_per_seq

        def tail_map(i):
            return i, 0

        tail_n = n

    def row(i):
        return i, 0

    def tab(i):
        return i % tab_tiles, 0

    wide = pl.BlockSpec((tm, A_WIDTH), row)
    out_shapes = [jax.ShapeDtypeStruct((n, A_WIDTH), BF16)] * 7 + [
        jax.ShapeDtypeStruct((n, LANES), BF16),
        jax.ShapeDtypeStruct((n, KV_LORA), F32),
        jax.ShapeDtypeStruct((n, ROPE_DIM), F32),
        jax.ShapeDtypeStruct((tail_n, A_WIDTH), F32),
        jax.ShapeDtypeStruct((tail_n, A_WIDTH), F32),
    ]
    out_specs = [wide] * 7 + [
        pl.BlockSpec((tm, LANES), row),
        pl.BlockSpec((tm, KV_LORA), row),
        pl.BlockSpec((tm, ROPE_DIM), row),
        pl.BlockSpec((tm, A_WIDTH), tail_map),
        pl.BlockSpec((tm, A_WIDTH), tail_map),
    ]
    weights = [wa, wc, wk, gcq, gckv, wuq, wkv]
    blocks = (_nbytes((tm, d), F32) + 7 * _nbytes((tm, A_WIDTH), BF16) + 3 * _nbytes((tm, A_WIDTH), F32)
              + 4 * _nbytes((tm, LANES), F32))
    resident = sum(_nbytes(w.shape, w.dtype) for w in weights)
    temps = 6 * _nbytes((tm, 2 * A_WIDTH), F32)
    return pl.pallas_call(
        _inproj_body,
        grid=(n_tiles,),
        in_specs=[
            pl.BlockSpec((tm, d), row),
            _resident((1, d)),
            pl.BlockSpec((tm, LANES), tab),
            pl.BlockSpec((tm, LANES), tab),
        ] + [_resident(w.shape) for w in weights],
        out_specs=out_specs,
        out_shape=out_shapes,
        compiler_params=pltpu.CompilerParams(
            dimension_semantics=("arbitrary",),
            vmem_limit_bytes=_vmem_limit(blocks, resident, temps)),
        name="inproj",
    )(x, g, cs, sn, *weights)


def _toeplitz(row, nrows):
    x = jnp.broadcast_to(row * LOG2E, (nrows, TOEPLITZ_W))
    left = (LANES - 1) - lax.broadcasted_iota(jnp.int32, (nrows, TOEPLITZ_W), 0)
    for b in range(7):
        rolled = pltpu.roll(x, TOEPLITZ_W - (1 << b), axis=1)
        x = jnp.where(((left >> b) & 1) == 1, rolled, x)
    return x


def _band_body(rb_ref, q_ref, k_ref, v_ref, o_ref, t_ref, sa_ref, sb_ref, *, n_tiles):
    @pl.when(pl.program_id(1) == 0)
    def _():
        t = _toeplitz(rb_ref[...], BAND_Q)[:, 0:BAND_K]
        i = lax.broadcasted_iota(jnp.int32, (BAND_Q, BAND_K), 0)
        j = lax.broadcasted_iota(jnp.int32, (BAND_Q, BAND_K), 1)
        visible = ((i < CHUNK) & (j < BAND_PAST + CHUNK)) | ((i >= CHUNK) & (j >= CHUNK))
        t_ref[...] = jnp.where(visible, t, NEG)

    ones = jnp.ones((BAND_K, LANES), BF16)

    def finish(s, row_max, k0, nk, r0):
        p = jnp.exp2(s - row_max).astype(BF16)
        acc = _dot(p, jnp.concatenate([v_ref[pl.ds(k0, nk), :], ones[0:nk]], axis=1))
        o_ref[pl.ds(r0, BAND_Q), :] = (acc[:, 0:A_HEAD_DIM] / acc[:, A_HEAD_DIM:]).astype(BF16)

    past_tiles = BAND_PAST // BAND_Q
    for t in range(past_tiles):
        nk = (t + 1) * BAND_Q
        s = _dot_nt(q_ref[t * BAND_Q:(t + 1) * BAND_Q, :], k_ref[0:nk, :]) + t_ref[:, BAND_K - nk:BAND_K]
        finish(s, jnp.max(s, axis=-1, keepdims=True), 0, nk, t * BAND_Q)

    def rows(t):
        return pl.multiple_of(t * BAND_Q, BAND_Q)

    def band(t):
        return pl.multiple_of(t * BAND_Q - BAND_PAST, BAND_Q)

    def scores_into(s_ref, t):
        s = _dot_nt(q_ref[pl.ds(rows(t), BAND_Q), :], k_ref[pl.ds(band(t), BAND_K), :]) + t_ref[...]
        s_ref[...] = s
        return jnp.max(s, axis=-1, keepdims=True)

    def pair(step, max_a):
        t_a = past_tiles + 2 * step
        max_b = scores_into(sb_ref, t_a + 1)
        finish(sa_ref[...], max_a, band(t_a), BAND_K, rows(t_a))
        max_next = scores_into(sa_ref, jnp.minimum(t_a + 2, n_tiles - 1))
        finish(sb_ref[...], max_b, band(t_a + 1), BAND_K, rows(t_a + 1))
        return max_next

    lax.fori_loop(0, (n_tiles - past_tiles) // 2, pair, scores_into(sa_ref, past_tiles))


def _band_prompt(q, k, v, rb_rows):
    b, s, _ = q.shape
    assert s % (2 * BAND_Q) == 0 and s >= BAND_K
    head = pl.BlockSpec((None, s, A_HEAD_DIM), lambda h, i: (i, 0, h))
    blocks = 4 * _nbytes((s, A_HEAD_DIM), BF16)
    scratch = 3 * _nbytes((BAND_Q, BAND_K), F32)
    temps = 16 * _nbytes((BAND_Q, TOEPLITZ_W), F32)
    return pl.pallas_call(
        functools.partial(_band_body, n_tiles=s // BAND_Q),
        grid=(A_HEADS, b),
        in_specs=[pl.BlockSpec((None, 1, TOEPLITZ_W), lambda h, i: (h, 0, 0)), head, head, head],
        out_specs=head,
        out_shape=jax.ShapeDtypeStruct(q.shape, BF16),
        scratch_shapes=[pltpu.VMEM((BAND_Q, BAND_K), F32)] * 3,
        compiler_params=pltpu.CompilerParams(
            dimension_semantics=("arbitrary", "arbitrary"),
            vmem_limit_bytes=_vmem_limit(blocks, scratch, temps)),
        name="band_prompt",
    )(rb_rows, q, k, v)


def _band_sample_body(rb_ref, q_ref, kn_ref, vn_ref, ck_ref, cv_ref, o_ref):
    t_new = q_ref.shape[0]
    n_cache = ck_ref.shape[0]
    for h in range(A_HEADS):
        cols = slice(h * A_HEAD_DIM, (h + 1) * A_HEAD_DIM)
        bias = _toeplitz(rb_ref[h], t_new)
        q = q_ref[:, cols]
        s_c = _dot_nt(q, ck_ref[:, cols].astype(BF16)) + bias[:, BAND_PAST - n_cache:BAND_PAST]
        s_n = _dot_nt(q, kn_ref[:, cols]) + bias[:, BAND_PAST:BAND_PAST + t_new]
        m = jnp.maximum(jnp.max(s_c, axis=-1, keepdims=True), jnp.max(s_n, axis=-1, keepdims=True))
        p_c = jnp.exp2(s_c - m)
        p_n = jnp.exp2(s_n - m)
        l = jnp.sum(p_c, axis=-1, keepdims=True) + jnp.sum(p_n, axis=-1, keepdims=True)
        o = _dot(p_c.astype(BF16), cv_ref[:, cols].astype(BF16)) + _dot(p_n.astype(BF16), vn_ref[:, cols])
        o_ref[:, cols] = (o / l).astype(BF16)


def _band_sample(q, k_new, v_new, cache_k, cache_v, layer, rb_rows):
    b, t, w = q.shape
    a = cache_k.shape[2]
    assert t <= LANES and a <= BAND_PAST
    new = pl.BlockSpec((None, t, w), lambda i: (i, 0, 0))
    cache = pl.BlockSpec((None, None, a, w), lambda i: (layer, i, 0, 0))
    blocks = 4 * _nbytes((t, w), BF16) + 2 * _nbytes((a, w), F32)
    temps = 2 * _nbytes((a, w), F32) + 16 * _nbytes((t, TOEPLITZ_W), F32)
    return pl.pallas_call(
        _band_sample_body,
        grid=(b,),
        in_specs=[_resident(rb_rows.shape), new, new, new, cache, cache],
        out_specs=new,
        out_shape=jax.ShapeDtypeStruct(q.shape, BF16),
        compiler_params=pltpu.CompilerParams(
            dimension_semantics=("arbitrary",),
            vmem_limit_bytes=_vmem_limit(blocks, _nbytes(rb_rows.shape, F32), temps)),
        name="band_sample",
    )(rb_rows, q, k_new, v_new, cache_k, cache_v)


def _mla_body(qn_ref, qr_ref, kn_ref, kr_ref, v_ref, o_ref, sa_ref, sb_ref, *, n_tiles):
    tile = MLA_TILE
    r = lax.broadcasted_iota(jnp.int32, (tile, tile), 0)
    c = lax.broadcasted_iota(jnp.int32, (tile, tile), 1)
    chunk_bits = CHUNK.bit_length() - 1
    diag_visible = lax.shift_right_logical(c, chunk_bits) <= lax.shift_right_logical(r, chunk_bits)
    ones = jnp.ones((tile, LANES), BF16)

    def keys(c0):
        return jnp.concatenate([kn_ref[pl.ds(c0, tile), :], kr_ref[pl.ds(c0, tile), :]], axis=1)

    def values(c0):
        return jnp.concatenate([v_ref[pl.ds(c0, tile), :], ones], axis=1)

    def row_max(s):
        return jnp.max(s, axis=-1, keepdims=True)

    for i in range(n_tiles):
        r0 = i * tile
        q = jnp.concatenate([qn_ref[r0:r0 + tile, :], qr_ref[r0:r0 + tile, :]], axis=1)

        def scores_into(s_ref, c0, q=q):
            s = _dot_nt(q, keys(c0))
            s_ref[...] = s
            return row_max(s)

        def update(m, acc, s, s_max, c0):
            m_new = jnp.maximum(m, s_max)
            p = jnp.exp2(s - m_new).astype(BF16)
            return m_new, jnp.exp2(m - m_new) * acc + _dot(p, values(c0))

        def pair(step, state, scores_into=scores_into, update=update):
            m, acc, max_a = state
            c_a = pl.multiple_of(2 * step * tile, tile)
            c_b = pl.multiple_of(c_a + tile, tile)
            max_b = scores_into(sb_ref, c_b)
            m, acc = update(m, acc, sa_ref[...], max_a, c_a)
            max_next = scores_into(sa_ref, pl.multiple_of(c_b + tile, tile))
            m, acc = update(m, acc, sb_ref[...], max_b, c_b)
            return m, acc, max_next

        state = (jnp.full((tile, 1), NEG, F32), jnp.zeros((tile, 2 * LANES), F32), scores_into(sa_ref, 0))
        if i // 2:
            state = lax.fori_loop(0, i // 2, pair, state)
        m, acc, max_a = state
        if i % 2:
            scores_into(sb_ref, r0)
            m, acc = update(m, acc, sa_ref[...], max_a, r0 - tile)
            s = jnp.where(diag_visible, sb_ref[...], NEG)
        else:
            s = jnp.where(diag_visible, sa_ref[...], NEG)
        m, acc = update(m, acc, s, row_max(s), r0)
        o_ref[r0:r0 + tile, :] = (acc[:, 0:V_DIM] / acc[:, V_DIM:]).astype(BF16)


def _mla_prompt(qn, qr, kn, kr, v):
    b, s, _ = qn.shape
    assert s % MLA_TILE == 0
    head = pl.BlockSpec((None, s, LANES), lambda i, h: (i, 0, h))
    shared = pl.BlockSpec((None, s, LANES), lambda i, h: (i, 0, 0))
    blocks = 6 * _nbytes((s, LANES), BF16)
    scratch = 2 * _nbytes((MLA_TILE, MLA_TILE), F32)
    temps = 8 * _nbytes((MLA_TILE, MLA_TILE), F32)
    return pl.pallas_call(
        functools.partial(_mla_body, n_tiles=s // MLA_TILE),
        grid=(b, MLA_HEADS),
        in_specs=[head, head, head, shared, head],
        out_specs=head,
        out_shape=jax.ShapeDtypeStruct(qn.shape, BF16),
        scratch_shapes=[pltpu.VMEM((MLA_TILE, MLA_TILE), F32)] * 2,
        compiler_params=pltpu.CompilerParams(
            dimension_semantics=("arbitrary", "arbitrary"),
            vmem_limit_bytes=_vmem_limit(blocks, scratch, temps)),
        name="mla_prompt",
    )(qn, qr, kn, kr, v)


def _mla_sample_body(qn_ref, qr_ref, cn_ref, krn_ref, cc_ref, ckr_ref, wuk_ref, wuv_ref, o_ref):
    t_new = qn_ref.shape[0]
    q_lat = jnp.concatenate(
        [_dot(qn_ref[:, h * NOPE_DIM:(h + 1) * NOPE_DIM], wuk_ref[h]).astype(BF16) for h in range(MLA_HEADS)], axis=0)
    q_rot = jnp.concatenate(
        [qr_ref[:, h * LANES:h * LANES + ROPE_DIM] for h in range(MLA_HEADS)], axis=0)
    cache = cc_ref[...].astype(BF16)
    new = cn_ref[...].astype(BF16)
    s_c = _dot_nt(q_lat, cache) + _dot_nt(q_rot, ckr_ref[...].astype(BF16))
    s_n = _dot_nt(q_lat, new) + _dot_nt(q_rot, krn_ref[:, 0:ROPE_DIM])
    m = jnp.maximum(jnp.max(s_c, axis=-1, keepdims=True), jnp.max(s_n, axis=-1, keepdims=True))
    p_c = jnp.exp2(s_c - m)
    p_n = jnp.exp2(s_n - m)
    l = jnp.sum(p_c, axis=-1, keepdims=True) + jnp.sum(p_n, axis=-1, keepdims=True)
    o_lat = ((_dot(p_c.astype(BF16), cache) + _dot(p_n.astype(BF16), new)) / l).astype(BF16)
    for h in range(MLA_HEADS):
        o_ref[:, h * V_DIM:(h + 1) * V_DIM] = _dot(o_lat[h * t_new:(h + 1) * t_new, :], wuv_ref[h]).astype(BF16)


def _mla_sample(qn, qr, ckv_new, kr_new, cache_ckv, cache_kr, layer, wuk, wuv):
    b, t, w = qn.shape
    past = cache_ckv.shape[2]

    def per_batch(shape):
        return pl.BlockSpec((None,) + shape, lambda i: (i, 0, 0))

    def cached(shape):
        return pl.BlockSpec((None, None) + shape, lambda i: (layer, i, 0, 0))

    blocks = (2 * _nbytes((t, w), BF16) + _nbytes((t, KV_LORA), F32) + _nbytes((t, LANES), BF16)
              + _nbytes((past, KV_LORA), F32) + _nbytes((past, ROPE_DIM), F32) + _nbytes((t, w), BF16))
    resident = _nbytes(wuk.shape, BF16) + _nbytes(wuv.shape, BF16)
    temps = _nbytes((past, KV_LORA), F32) + 4 * _nbytes((MLA_HEADS * t, past), F32)
    return pl.pallas_call(
        _mla_sample_body,
        grid=(b,),
        in_specs=[per_batch((t, w)), per_batch((t, w)), per_batch((t, KV_LORA)), per_batch((t, LANES)),
                  cached((past, KV_LORA)), cached((past, ROPE_DIM)),
                  _resident(wuk.shape), _resident(wuv.shape)],
        out_specs=per_batch((t, w)),
        out_shape=jax.ShapeDtypeStruct(qn.shape, BF16),
        compiler_params=pltpu.CompilerParams(
            dimension_semantics=("arbitrary",),
            vmem_limit_bytes=_vmem_limit(blocks, resident, temps)),
        name="mla_sample",
    )(qn, qr, ckv_new, kr_new, cache_ckv, cache_kr, wuk, wuv)


def _merge_body(h_ref, g_ref, oa_ref, ob_ref, wga_ref, wgb_ref, woa_ref, wob_ref, wout_ref, o_ref, u_ref, acc_ref):
    j = pl.program_id(1)

    @pl.when(j == 0)
    def _():
        u_ref[...] = _rms(h_ref[...], g_ref[...]).astype(BF16)
        acc_ref[...] = jnp.zeros_like(acc_ref)

    u = u_ref[...]
    m = (jax.nn.sigmoid(_dot(u, wga_ref[...])) * _dot(oa_ref[...], woa_ref[...])
         + jax.nn.sigmoid(_dot(u, wgb_ref[...])) * _dot(ob_ref[...], wob_ref[...]))
    acc_ref[...] += _dot(m.astype(BF16), wout_ref[...])

    @pl.when(j == pl.num_programs(1) - 1)
    def _():
        o_ref[...] = h_ref[...] + acc_ref[...]


def _merge(h, g, oa, ob, wga, wgb, woa, wob, wout):
    n, d = h.shape
    tm = min(ROW_TILE, n)
    tn = _col_tile(d)
    blocks = (2 * _nbytes((tm, d), F32) + 2 * _nbytes((tm, A_WIDTH), BF16) + 2 * _nbytes((d, tn), BF16)
              + 2 * _nbytes((A_WIDTH, tn), BF16) + _nbytes((tn, d), BF16))
    scratch = _nbytes((tm, d), BF16) + _nbytes((tm, d), F32)
    temps = 6 * _nbytes((tm, tn), F32) + _nbytes((tm, d), F32)
    return pl.pallas_call(
        _merge_body,
        grid=(n // tm, d // tn),
        in_specs=[
            pl.BlockSpec((tm, d), lambda i, j: (i, 0)),
            pl.BlockSpec((1, d), lambda i, j: (0, 0)),
            pl.BlockSpec((tm, A_WIDTH), lambda i, j: (i, 0)),
            pl.BlockSpec((tm, MLA_WIDTH), lambda i, j: (i, 0)),
            pl.BlockSpec((d, tn), lambda i, j: (0, j)),
            pl.BlockSpec((d, tn), lambda i, j: (0, j)),
            pl.BlockSpec((A_WIDTH, tn), lambda i, j: (0, j)),
            pl.BlockSpec((MLA_WIDTH, tn), lambda i, j: (0, j)),
            pl.BlockSpec((tn, d), lambda i, j: (j, 0)),
        ],
        out_specs=pl.BlockSpec((tm, d), lambda i, j: (i, 0)),
        out_shape=jax.ShapeDtypeStruct((n, d), F32),
        scratch_shapes=[pltpu.VMEM((tm, d), BF16), pltpu.VMEM((tm, d), F32)],
        compiler_params=pltpu.CompilerParams(
            dimension_semantics=("arbitrary", "arbitrary"),
            vmem_limit_bytes=_vmem_limit(blocks, scratch, temps)),
        name="merge",
    )(h, g, oa, ob, wga, wgb, woa, wob, wout)


def _ple_body(h_ref, p_ref, gp_ref, wpg_ref, wp_ref, gf_ref, y_ref, *, final_norm):
    h = h_ref[...]
    gate = jax.nn.sigmoid(_dot(_rms(h, gp_ref[...]).astype(BF16), wpg_ref[...]))
    h = h + gate * _dot(p_ref[...].astype(BF16), wp_ref[...])
    y_ref[...] = _rms(h, gf_ref[...]) if final_norm else h


def _ple(h, p, layer, gp, wpg, wp, gf, *, final_norm):
    n, d = h.shape
    e = p.shape[2]
    tm = min(ROW_TILE, n)
    blocks = 2 * _nbytes((tm, d), F32) + _nbytes((tm, e), F32)
    resident = _nbytes(wpg.shape, BF16) + _nbytes(wp.shape, BF16)
    temps = 4 * _nbytes((tm, d), F32)
    return pl.pallas_call(
        functools.partial(_ple_body, final_norm=final_norm),
        grid=(n // tm,),
        in_specs=[
            pl.BlockSpec((tm, d), lambda i: (i, 0)),
            pl.BlockSpec((None, tm, e), lambda i: (layer, i, 0)),
            _resident((1, d)),
            _resident(wpg.shape),
            _resident(wp.shape),
            _resident((1, d)),
        ],
        out_specs=pl.BlockSpec((tm, d), lambda i: (i, 0)),
        out_shape=jax.ShapeDtypeStruct((n, d), F32),
        compiler_params=pltpu.CompilerParams(
            dimension_semantics=("arbitrary",),
            vmem_limit_bytes=_vmem_limit(blocks, resident, temps)),
        name="ple",
    )(h, p, gp, wpg, wp, gf)


def _rope_tables(pos):
    half = ROPE_DIM // 2
    inv = ROPE_THETA ** (-jnp.arange(half, dtype=F32) / half)
    ang = pos.astype(F32)[:, None] * inv[None, :]
    cos, sin = jnp.cos(ang), jnp.sin(ang)
    zero = jnp.zeros((pos.shape[0], LANES - ROPE_DIM), F32)
    return jnp.concatenate([cos, cos, zero], axis=1), jnp.concatenate([-sin, sin, zero], axis=1)


def _pad_swap(w, lead):
    half = ROPE_DIM // 2
    x1, x2 = w[..., :half], w[..., half:]
    zero = jnp.zeros(lead + (LANES - ROPE_DIM,), w.dtype)
    return jnp.concatenate([x1, x2, zero], axis=-1), jnp.concatenate([x2, x1, zero], axis=-1)


def _layer_weights(i, d, g_ffn1, w_ffn1_gate, w_ffn1_up, w_ffn1_down, g_mix, w_in, g_cq, w_uq, w_uk, w_uv, g_ckv,
                   rel_bias, w_oa, w_ob, w_out, g_ffn2, w_ffn2_gate, w_ffn2_up, w_ffn2_down, g_ple, w_ple_gate, w_ple):
    bf = lambda a: a.astype(BF16)
    row = lambda a: a.reshape(1, -1).astype(F32)
    w = w_in[i]
    c0 = 3 * A_WIDTH
    c1 = c0 + Q_LORA
    c2 = c1 + KV_LORA
    c3 = c2 + ROPE_DIM
    kr_rot, kr_swp = _pad_swap(w[:, c2:c3], (d,))
    uq = w_uq[i].reshape(Q_LORA, MLA_HEADS, NOPE_DIM + ROPE_DIM)
    uq_rot, uq_swp = _pad_swap(uq[:, :, NOPE_DIM:], (Q_LORA, MLA_HEADS))
    rel = jnp.clip(jnp.arange(TOEPLITZ_W) - (BAND_PAST + LANES - 1), -REL_CLIP, REL_CLIP) + REL_CLIP
    return dict(
        ffn1=(row(g_ffn1[i]), bf(w_ffn1_gate[i]), bf(w_ffn1_up[i]), bf(w_ffn1_down[i])),
        ffn2=(row(g_ffn2[i]), bf(w_ffn2_gate[i]), bf(w_ffn2_up[i]), bf(w_ffn2_down[i])),
        g_mix=row(g_mix[i]),
        inproj=(
            bf(w[:, 0:c0]),
            bf(w[:, c0:c1]),
            bf(jnp.concatenate([w[:, c1:c2], kr_rot, kr_swp], axis=1)),
            row(g_cq[i]),
            row(g_ckv[i]),
            bf(jnp.concatenate([uq[:, :, :NOPE_DIM].reshape(Q_LORA, MLA_WIDTH), uq_rot.reshape(Q_LORA, MLA_WIDTH),
                                uq_swp.reshape(Q_LORA, MLA_WIDTH)], axis=1)),
            bf(jnp.concatenate([jnp.transpose(w_uk[i], (2, 0, 1)).reshape(KV_LORA, MLA_WIDTH),
                                jnp.transpose(w_uv[i], (1, 0, 2)).reshape(KV_LORA, MLA_WIDTH)], axis=1)),
        ),
        rb_rows=rel_bias[i][:, rel].reshape(A_HEADS, 1, TOEPLITZ_W).astype(F32),
        w_uk=bf(w_uk[i]),
        w_uv=bf(w_uv[i]),
        merge=(bf(w[:, c3:c3 + d]), bf(w[:, c3 + d:c3 + 2 * d]), bf(w_oa[i]), bf(w_ob[i]), bf(w_out[i])),
        ple=(row(g_ple[i]), bf(w_ple_gate[i]), bf(w_ple[i])),
    )


def kernel(x_prompt, x_sample, cache_a_k, cache_a_v, cache_mla_ckv, cache_mla_krope, p_prompt, p_sample, g_ffn1, w_ffn1_gate, w_ffn1_up, w_ffn1_down, g_mix, w_in, g_cq, w_uq, w_uk, w_uv, g_ckv, rel_bias, w_oa, w_ob, w_out, g_ffn2, w_ffn2_gate, w_ffn2_up, w_ffn2_down, g_ple, w_ple_gate, w_ple, g_final):
    b, s, d = x_prompt.shape
    bs, t, _ = x_sample.shape
    depth = g_ffn1.shape[0]
    past = cache_mla_ckv.shape[2]
    a_len = cache_a_k.shape[2]
    a_keep = min(BAND_PAST, s)

    cs_p, sn_p = _rope_tables(jnp.arange(s))
    cs_s, sn_s = _rope_tables(past + jnp.arange(t))
    cs_s, sn_s = jnp.tile(cs_s, (bs, 1)), jnp.tile(sn_s, (bs, 1))
    g_fin = g_final.reshape(1, d).astype(F32)

    hp = x_prompt.reshape(b * s, d)
    hs = x_sample.reshape(bs * t, d)
    outs = [[] for _ in range(8)]
    for i in range(depth):
        lw = _layer_weights(i, d, g_ffn1, w_ffn1_gate, w_ffn1_up, w_ffn1_down, g_mix, w_in, g_cq, w_uq, w_uk, w_uv,
                            g_ckv, rel_bias, w_oa, w_ob, w_out, g_ffn2, w_ffn2_gate, w_ffn2_up, w_ffn2_down, g_ple,
                            w_ple_gate, w_ple)

        hp = _ffn(hp, *lw["ffn1"])
        qa, ka, va, qn, qr, kn, vm, krp, ckv, kr, ka_tail, va_tail = _inproj(
            hp, lw["g_mix"], cs_p, sn_p, *lw["inproj"], rows_per_seq=s, tail_rows=a_keep)
        seq = lambda a: a.reshape(b, s, a.shape[-1])
        oa = _band_prompt(seq(qa), seq(ka), seq(va), lw["rb_rows"])
        ob = _mla_prompt(seq(qn), seq(qr), seq(kn), seq(krp), seq(vm))
        hp = _merge(hp, lw["g_mix"], oa.reshape(b * s, A_WIDTH), ob.reshape(b * s, MLA_WIDTH), *lw["merge"])
        outs[0].append(ka_tail.reshape(b, a_keep, A_HEADS, A_HEAD_DIM))
        outs[1].append(va_tail.reshape(b, a_keep, A_HEADS, A_HEAD_DIM))
        outs[2].append(ckv.reshape(b, s, KV_LORA))
        outs[3].append(kr.reshape(b, s, ROPE_DIM))

        hs = _ffn(hs, *lw["ffn1"])
        qa, ka, va, qn, qr, _, _, krp, ckv, kr, ka_new, va_new = _inproj(
            hs, lw["g_mix"], cs_s, sn_s, *lw["inproj"], rows_per_seq=t, tail_rows=t)
        new = lambda a: a.reshape(bs, t, a.shape[-1])
        oa = _band_sample(new(qa), new(ka), new(va), cache_a_k.reshape(depth, bs, a_len, A_WIDTH),
                          cache_a_v.reshape(depth, bs, a_len, A_WIDTH), i, lw["rb_rows"])
        ob = _mla_sample(new(qn), new(qr), new(ckv), new(krp), cache_mla_ckv, cache_mla_krope, i,
                         lw["w_uk"], lw["w_uv"])
        hs = _merge(hs, lw["g_mix"], oa.reshape(bs * t, A_WIDTH), ob.reshape(bs * t, MLA_WIDTH), *lw["merge"])
        outs[4].append(ka_new.reshape(bs, t, A_HEADS, A_HEAD_DIM))
        outs[5].append(va_new.reshape(bs, t, A_HEADS, A_HEAD_DIM))
        outs[6].append(ckv.reshape(bs, t, KV_LORA))
        outs[7].append(kr.reshape(bs, t, ROPE_DIM))

        hp = _ffn(hp, *lw["ffn2"])
        hs = _ffn(hs, *lw["ffn2"])
        last = i == depth - 1
        hp = _ple(hp, p_prompt.reshape(depth, b * s, -1), i, *lw["ple"], g_fin, final_norm=last)
        hs = _ple(hs, p_sample.reshape(depth, bs * t, -1), i, *lw["ple"], g_fin, final_norm=last)

    return (hp.reshape(b, s, d), hs.reshape(bs, t, d), *[jnp.stack(o) for o in outs])
```

```python
import functools

import jax
import jax.numpy as jnp
from jax import lax
from jax.experimental import pallas as pl
from jax.experimental.pallas import tpu as pltpu

F32 = jnp.float32
BF16 = jnp.bfloat16

CHUNK = 64
BAND_PAST = 512
A_HEADS = 8
A_HEAD_DIM = 128
A_WIDTH = A_HEADS * A_HEAD_DIM
REL_CLIP = 128
MLA_HEADS = 8
Q_LORA = 768
KV_LORA = 512
NOPE_DIM = 128
ROPE_DIM = 64
V_DIM = 128
MLA_WIDTH = MLA_HEADS * V_DIM
LOG2E = 1.4426950408889634
MLA_SCALE = (NOPE_DIM + ROPE_DIM) ** -0.5 * LOG2E
A_SCALE = A_HEAD_DIM ** -0.5 * LOG2E
ROPE_THETA = 10000.0
EPS = 1e-6
NEG = -1e30

LANES = 128
V7X_VMEM_BYTES = 64 * 1024 * 1024
VMEM_CEILING = V7X_VMEM_BYTES - 6 * 1024 * 1024

ROW_TILE = 512
FFN_ROW_TILE = 1024
INPROJ_ROW_TILE = 256
COL_TILE = 512
BAND_Q = 4 * CHUNK
BAND_K = BAND_PAST + BAND_Q
TOEPLITZ_W = BAND_K + BAND_Q
MLA_TILE = 512
MLA_WIDE = 1024

def _dot(a, b):
    return jnp.dot(a, b, preferred_element_type=F32)


def _dot_nt(a, b):
    return lax.dot_general(a, b, (((1,), (1,)), ((), ())), preferred_element_type=F32)


def _rms(x, g):
    return x * lax.rsqrt(jnp.mean(x * x, axis=-1, keepdims=True) + EPS) * g


def _vmem_limit(block_bytes, scratch_bytes, temp_bytes):
    return int(min(2 * block_bytes + scratch_bytes + temp_bytes + (4 << 20), VMEM_CEILING))


def _nbytes(shape, dtype):
    n = 1
    for s in shape:
        n *= s
    return n * jnp.dtype(dtype).itemsize


def _resident(shape):
    nd = len(shape)
    return pl.BlockSpec(shape, lambda *_: (0,) * nd, pipeline_mode=pl.Buffered(1))


def _col_tile(n):
    for t in (COL_TILE, 256, LANES):
        if n % t == 0:
            return t
    raise ValueError(f"width {n} is not a multiple of {LANES}")


def _ffn_body(x_ref, g_ref, wg_ref, wu_ref, wd_ref, o_ref, u_ref):
    @pl.when(pl.program_id(1) == 0)
    def _():
        x = x_ref[...]
        u_ref[...] = _rms(x, g_ref[...]).astype(BF16)
        o_ref[...] = x

    u = u_ref[...]
    gate = _dot(u, wg_ref[...])
    up = _dot(u, wu_ref[...])
    half_act = (gate * jax.nn.sigmoid(gate) * (0.5 * up)).astype(BF16)
    d = o_ref.shape[1]
    chunk = _col_tile(d)
    for c in range(0, d, chunk):
        o_ref[:, c:c + chunk] += _dot(half_act, wd_ref[:, c:c + chunk])


def _ffn(x, g, wg, wu, wd):
    n, d = x.shape
    f = wg.shape[1]
    tm = min(FFN_ROW_TILE, n)
    tf = _col_tile(f)
    blocks = _nbytes((tm, d), F32) * 2 + 3 * _nbytes((d, tf), BF16)
    scratch = _nbytes((tm, d), BF16)
    temps = 6 * _nbytes((tm, tf), F32)
    return pl.pallas_call(
        _ffn_body,
        grid=(n // tm, f // tf),
        in_specs=[
            pl.BlockSpec((tm, d), lambda i, j: (i, 0)),
            pl.BlockSpec((1, d), lambda i, j: (0, 0)),
            pl.BlockSpec((d, tf), lambda i, j: (0, j)),
            pl.BlockSpec((d, tf), lambda i, j: (0, j)),
            pl.BlockSpec((tf, d), lambda i, j: (j, 0)),
        ],
        out_specs=pl.BlockSpec((tm, d), lambda i, j: (i, 0)),
        out_shape=jax.ShapeDtypeStruct((n, d), F32),
        scratch_shapes=[pltpu.VMEM((tm, d), BF16)],
        compiler_params=pltpu.CompilerParams(
            dimension_semantics=("arbitrary", "arbitrary"),
            vmem_limit_bytes=_vmem_limit(blocks, scratch, temps)),
        name="ffn",
    )(x, g, wg, wu, wd)


def _inproj_body(x_ref, g_ref, cs_ref, sn_ref, wa_ref, wc_ref, wk_ref, gcq_ref, gckv_ref, wuq_ref, wkv_ref,
                 qa_o, ka_o, va_o, qn_o, qr_o, kn_o, vm_o, krp_o, ckv_o, kr_o, kat_o, vat_o):
    u = _rms(x_ref[...], g_ref[...]).astype(BF16)

    qa_o[...] = (_dot(u, wa_ref[:, 0:A_WIDTH]) * A_SCALE).astype(BF16)
    ka = _dot(u, wa_ref[:, A_WIDTH:2 * A_WIDTH])
    ka_o[...] = ka.astype(BF16)
    kat_o[...] = ka
    va = _dot(u, wa_ref[:, 2 * A_WIDTH:3 * A_WIDTH])
    va_o[...] = va.astype(BF16)
    vat_o[...] = va

    cs = cs_ref[...]
    sn = sn_ref[...]

    cqn = _rms(_dot(u, wc_ref[...]), gcq_ref[...]).astype(BF16)
    qn_o[...] = (_dot(cqn, wuq_ref[:, 0:MLA_WIDTH]) * MLA_SCALE).astype(BF16)
    q_rot = _dot(cqn, wuq_ref[:, MLA_WIDTH:2 * MLA_WIDTH])
    q_swp = _dot(cqn, wuq_ref[:, 2 * MLA_WIDTH:3 * MLA_WIDTH])
    cs_h = jnp.concatenate([cs] * MLA_HEADS, axis=1)
    sn_h = jnp.concatenate([sn] * MLA_HEADS, axis=1)
    qr_o[...] = ((q_rot * cs_h + q_swp * sn_h) * MLA_SCALE).astype(BF16)

    zc = _dot(u, wk_ref[...])
    ckvn = _rms(zc[:, 0:KV_LORA], gckv_ref[...])
    ckv_o[...] = ckvn
    krp = zc[:, KV_LORA:KV_LORA + LANES] * cs + zc[:, KV_LORA + LANES:KV_LORA + 2 * LANES] * sn
    krp_o[...] = krp.astype(BF16)
    kr_o[...] = krp[:, 0:ROPE_DIM]

    kv = _dot(ckvn.astype(BF16), wkv_ref[...])
    kn_o[...] = kv[:, 0:MLA_WIDTH].astype(BF16)
    vm_o[...] = kv[:, MLA_WIDTH:2 * MLA_WIDTH].astype(BF16)


def _inproj(x, g, cs, sn, wa, wc, wk, gcq, gckv, wuq, wkv, *, rows_per_seq, tail_rows):
    n, d = x.shape
    tm = min(INPROJ_ROW_TILE, n)
    n_tiles = n // tm
    tab_tiles = cs.shape[0] // tm
    if tm <= rows_per_seq:
        tiles_per_seq = rows_per_seq // tm
        tail_tiles = tail_rows // tm
        n_seq = n // rows_per_seq

        def tail_map(i):
            return (i // tiles_per_seq) * tail_tiles + jnp.maximum(i % tiles_per_seq - (tiles_per_seq - tail_tiles), 0), 0

        tail_n = n_seq * tail_rows
    else:
        assert tail_rows == rows_per_seq

        def tail_map(i):
            return i, 0

        tail_n = n

    def row(i):
        return i, 0

    def tab(i):
        return i % tab_tiles, 0

    wide = pl.BlockSpec((tm, A_WIDTH), row)
    out_shapes = [jax.ShapeDtypeStruct((n, A_WIDTH), BF16)] * 7 + [
        jax.ShapeDtypeStruct((n, LANES), BF16),
        jax.ShapeDtypeStruct((n, KV_LORA), F32),
        jax.ShapeDtypeStruct((n, ROPE_DIM), F32),
        jax.ShapeDtypeStruct((tail_n, A_WIDTH), F32),
        jax.ShapeDtypeStruct((tail_n, A_WIDTH), F32),
    ]
    out_specs = [wide] * 7 + [
        pl.BlockSpec((tm, LANES), row),
        pl.BlockSpec((tm, KV_LORA), row),
        pl.BlockSpec((tm, ROPE_DIM), row),
        pl.BlockSpec((tm, A_WIDTH), tail_map),
        pl.BlockSpec((tm, A_WIDTH), tail_map),
    ]
    weights = [wa, wc, wk, gcq, gckv, wuq, wkv]
    blocks = (_nbytes((tm, d), F32) + 7 * _nbytes((tm, A_WIDTH), BF16) + 3 * _nbytes((tm, A_WIDTH), F32)
              + 4 * _nbytes((tm, LANES), F32))
    resident = sum(_nbytes(w.shape, w.dtype) for w in weights)
    temps = 6 * _nbytes((tm, 2 * A_WIDTH), F32)
    return pl.pallas_call(
        _inproj_body,
        grid=(n_tiles,),
        in_specs=[
            pl.BlockSpec((tm, d), row),
            _resident((1, d)),
            pl.BlockSpec((tm, LANES), tab),
            pl.BlockSpec((tm, LANES), tab),
        ] + [_resident(w.shape) for w in weights],
        out_specs=out_specs,
        out_shape=out_shapes,
        compiler_params=pltpu.CompilerParams(
            dimension_semantics=("arbitrary",),
            vmem_limit_bytes=_vmem_limit(blocks, resident, temps)),
        name="inproj",
    )(x, g, cs, sn, *weights)


def _toeplitz(row, nrows):
    x = jnp.broadcast_to(row * LOG2E, (nrows, TOEPLITZ_W))
    left = (BAND_Q - 1) - lax.broadcasted_iota(jnp.int32, (nrows, TOEPLITZ_W), 0)
    for b in range((BAND_Q - 1).bit_length()):
        rolled = pltpu.roll(x, TOEPLITZ_W - (1 << b), axis=1)
        x = jnp.where(((left >> b) & 1) == 1, rolled, x)
    return x


def _band_body(rb_ref, q_ref, k_ref, v_ref, o_ref, t_ref, sa_ref, sb_ref, *, n_tiles):
    @pl.when(pl.program_id(1) == 0)
    def _():
        t = _toeplitz(rb_ref[...], BAND_Q)[:, 0:BAND_K]
        i = lax.broadcasted_iota(jnp.int32, (BAND_Q, BAND_K), 0)
        j = lax.broadcasted_iota(jnp.int32, (BAND_Q, BAND_K), 1)
        first = lax.shift_left(lax.shift_right_logical(i, CHUNK.bit_length() - 1), CHUNK.bit_length() - 1)
        visible = (j >= first) & (j < first + BAND_PAST + CHUNK)
        t_ref[...] = jnp.where(visible, t, NEG)

    ones = jnp.ones((BAND_K, LANES), BF16)

    def finish(s, row_max, k0, nk, r0):
        p = jnp.exp2(s - row_max).astype(BF16)
        acc = _dot(p, jnp.concatenate([v_ref[pl.ds(k0, nk), :], ones[0:nk]], axis=1))
        o_ref[pl.ds(r0, BAND_Q), :] = (acc[:, 0:A_HEAD_DIM] / acc[:, A_HEAD_DIM:]).astype(BF16)

    past_tiles = BAND_PAST // BAND_Q
    for t in range(past_tiles):
        nk = (t + 1) * BAND_Q
        s = _dot_nt(q_ref[t * BAND_Q:(t + 1) * BAND_Q, :], k_ref[0:nk, :]) + t_ref[:, BAND_K - nk:BAND_K]
        finish(s, jnp.max(s, axis=-1, keepdims=True), 0, nk, t * BAND_Q)

    def rows(t):
        return pl.multiple_of(t * BAND_Q, BAND_Q)

    def band(t):
        return pl.multiple_of(t * BAND_Q - BAND_PAST, BAND_Q)

    def scores_into(s_ref, t):
        s = _dot_nt(q_ref[pl.ds(rows(t), BAND_Q), :], k_ref[pl.ds(band(t), BAND_K), :]) + t_ref[...]
        s_ref[...] = s
        return jnp.max(s, axis=-1, keepdims=True)

    def pair(step, max_a):
        t_a = past_tiles + 2 * step
        max_b = scores_into(sb_ref, t_a + 1)
        finish(sa_ref[...], max_a, band(t_a), BAND_K, rows(t_a))
        max_next = scores_into(sa_ref, jnp.minimum(t_a + 2, n_tiles - 1))
        finish(sb_ref[...], max_b, band(t_a + 1), BAND_K, rows(t_a + 1))
        return max_next

    lax.fori_loop(0, (n_tiles - past_tiles) // 2, pair, scores_into(sa_ref, past_tiles))


def _band_prompt(q, k, v, rb_rows):
    b, s, _ = q.shape
    assert s % (2 * BAND_Q) == 0 and s >= BAND_K
    head = pl.BlockSpec((None, s, A_HEAD_DIM), lambda h, i: (i, 0, h))
    blocks = 4 * _nbytes((s, A_HEAD_DIM), BF16)
    scratch = 3 * _nbytes((BAND_Q, BAND_K), F32)
    temps = 16 * _nbytes((BAND_Q, TOEPLITZ_W), F32)
    return pl.pallas_call(
        functools.partial(_band_body, n_tiles=s // BAND_Q),
        grid=(A_HEADS, b),
        in_specs=[pl.BlockSpec((None, 1, TOEPLITZ_W), lambda h, i: (h, 0, 0)), head, head, head],
        out_specs=head,
        out_shape=jax.ShapeDtypeStruct(q.shape, BF16),
        scratch_shapes=[pltpu.VMEM((BAND_Q, BAND_K), F32)] * 3,
        compiler_params=pltpu.CompilerParams(
            dimension_semantics=("arbitrary", "arbitrary"),
            vmem_limit_bytes=_vmem_limit(blocks, scratch, temps)),
        name="band_prompt",
    )(rb_rows, q, k, v)


def _band_sample_body(rb_ref, q_ref, kn_ref, vn_ref, ck_ref, cv_ref, o_ref):
    t_new = q_ref.shape[0]
    n_cache = ck_ref.shape[0]
    for h in range(A_HEADS):
        cols = slice(h * A_HEAD_DIM, (h + 1) * A_HEAD_DIM)
        bias = _toeplitz(rb_ref[h], t_new)
        q = q_ref[:, cols]
        s_c = _dot_nt(q, ck_ref[:, cols].astype(BF16)) + bias[:, BAND_PAST - n_cache:BAND_PAST]
        s_n = _dot_nt(q, kn_ref[:, cols]) + bias[:, BAND_PAST:BAND_PAST + t_new]
        m = jnp.maximum(jnp.max(s_c, axis=-1, keepdims=True), jnp.max(s_n, axis=-1, keepdims=True))
        p_c = jnp.exp2(s_c - m)
        p_n = jnp.exp2(s_n - m)
        l = jnp.sum(p_c, axis=-1, keepdims=True) + jnp.sum(p_n, axis=-1, keepdims=True)
        o = _dot(p_c.astype(BF16), cv_ref[:, cols].astype(BF16)) + _dot(p_n.astype(BF16), vn_ref[:, cols])
        o_ref[:, cols] = (o / l).astype(BF16)


def _band_sample(q, k_new, v_new, cache_k, cache_v, layer, rb_rows):
    b, t, w = q.shape
    a = cache_k.shape[2]
    assert t <= BAND_Q and a <= BAND_PAST
    new = pl.BlockSpec((None, t, w), lambda i: (i, 0, 0))
    cache = pl.BlockSpec((None, None, a, w), lambda i: (layer, i, 0, 0))
    blocks = 4 * _nbytes((t, w), BF16) + 2 * _nbytes((a, w), F32)
    temps = 2 * _nbytes((a, w), F32) + 16 * _nbytes((t, TOEPLITZ_W), F32)
    return pl.pallas_call(
        _band_sample_body,
        grid=(b,),
        in_specs=[_resident(rb_rows.shape), new, new, new, cache, cache],
        out_specs=new,
        out_shape=jax.ShapeDtypeStruct(q.shape, BF16),
        compiler_params=pltpu.CompilerParams(
            dimension_semantics=("arbitrary",),
            vmem_limit_bytes=_vmem_limit(blocks, _nbytes(rb_rows.shape, F32), temps)),
        name="band_sample",
    )(rb_rows, q, k_new, v_new, cache_k, cache_v)


def _mla_body(qn_ref, qr_ref, kn_ref, kr_ref, v_ref, o_ref, sa_ref, sb_ref, *, n_tiles):
    tile = MLA_TILE
    r = lax.broadcasted_iota(jnp.int32, (tile, tile), 0)
    c = lax.broadcasted_iota(jnp.int32, (tile, tile), 1)
    chunk_bits = CHUNK.bit_length() - 1
    diag_visible = lax.shift_right_logical(c, chunk_bits) <= lax.shift_right_logical(r, chunk_bits)
    ones = jnp.ones((MLA_WIDE, LANES), BF16)
    buffers = (sa_ref, sb_ref)

    def row_max(s):
        return jnp.max(s, axis=-1, keepdims=True)

    for i in range(n_tiles):
        r0 = i * tile
        q = jnp.concatenate([qn_ref[r0:r0 + tile, :], qr_ref[r0:r0 + tile, :]], axis=1)
        blocks = [(c0, min(MLA_WIDE, r0 - c0)) for c0 in range(0, r0, MLA_WIDE)] + [(r0, tile)]

        def issue(k):
            c0, w = blocks[k]
            keys = jnp.concatenate([kn_ref[c0:c0 + w, :], kr_ref[c0:c0 + w, :]], axis=1)
            s = _dot_nt(q, keys)
            buffers[k % 2][:, 0:w] = s
            return row_max(s)

        m = jnp.full((tile, 1), NEG, F32)
        acc = jnp.zeros((tile, 2 * LANES), F32)
        s_max = issue(0)
        for k, (c0, w) in enumerate(blocks):
            last = k == len(blocks) - 1
            next_max = None if last else issue(k + 1)
            s = buffers[k % 2][:, 0:w]
            if last:
                s = jnp.where(diag_visible, s, NEG)
                s_max = row_max(s)
            m_new = jnp.maximum(m, s_max)
            p = jnp.exp2(s - m_new).astype(BF16)
            values = jnp.concatenate([v_ref[c0:c0 + w, :], ones[0:w]], axis=1)
            acc = jnp.exp2(m - m_new) * acc + _dot(p, values)
            m, s_max = m_new, next_max
        o_ref[r0:r0 + tile, :] = (acc[:, 0:V_DIM] / acc[:, V_DIM:]).astype(BF16)


def _mla_prompt(qn, qr, kn, kr, v):
    b, s, _ = qn.shape
    assert s % MLA_TILE == 0
    head = pl.BlockSpec((None, s, LANES), lambda i, h: (i, 0, h))
    shared = pl.BlockSpec((None, s, LANES), lambda i, h: (i, 0, 0))
    blocks = 6 * _nbytes((s, LANES), BF16)
    scratch = 2 * _nbytes((MLA_TILE, MLA_WIDE), F32)
    temps = 6 * _nbytes((MLA_TILE, MLA_WIDE), F32)
    return pl.pallas_call(
        functools.partial(_mla_body, n_tiles=s // MLA_TILE),
        grid=(b, MLA_HEADS),
        in_specs=[head, head, head, shared, head],
        out_specs=head,
        out_shape=jax.ShapeDtypeStruct(qn.shape, BF16),
        scratch_shapes=[pltpu.VMEM((MLA_TILE, MLA_WIDE), F32)] * 2,
        compiler_params=pltpu.CompilerParams(
            dimension_semantics=("arbitrary", "arbitrary"),
            vmem_limit_bytes=_vmem_limit(blocks, scratch, temps)),
        name="mla_prompt",
    )(qn, qr, kn, kr, v)


def _mla_sample_body(qn_ref, qr_ref, cn_ref, krn_ref, cc_ref, ckr_ref, wuk_ref, wuv_ref, o_ref):
    t_new = qn_ref.shape[0]
    q_lat = jnp.concatenate(
        [_dot(qn_ref[:, h * NOPE_DIM:(h + 1) * NOPE_DIM], wuk_ref[h]).astype(BF16) for h in range(MLA_HEADS)], axis=0)
    q_rot = jnp.concatenate(
        [qr_ref[:, h * LANES:h * LANES + ROPE_DIM] for h in range(MLA_HEADS)], axis=0)
    cache = cc_ref[...].astype(BF16)
    new = cn_ref[...].astype(BF16)
    s_c = _dot_nt(q_lat, cache) + _dot_nt(q_rot, ckr_ref[...].astype(BF16))
    s_n = _dot_nt(q_lat, new) + _dot_nt(q_rot, krn_ref[:, 0:ROPE_DIM])
    m = jnp.maximum(jnp.max(s_c, axis=-1, keepdims=True), jnp.max(s_n, axis=-1, keepdims=True))
    p_c = jnp.exp2(s_c - m)
    p_n = jnp.exp2(s_n - m)
    l = jnp.sum(p_c, axis=-1, keepdims=True) + jnp.sum(p_n, axis=-1, keepdims=True)
    o_lat = ((_dot(p_c.astype(BF16), cache) + _dot(p_n.astype(BF16), new)) / l).astype(BF16)
    for h in range(MLA_HEADS):
        o_ref[:, h * V_DIM:(h + 1) * V_DIM] = _dot(o_lat[h * t_new:(h + 1) * t_new, :], wuv_ref[h]).astype(BF16)


def _mla_sample(qn, qr, ckv_new, kr_new, cache_ckv, cache_kr, layer, wuk, wuv):
    b, t, w = qn.shape
    past = cache_ckv.shape[2]

    def per_batch(shape):
        return pl.BlockSpec((None,) + shape, lambda i: (i, 0, 0))

    def cached(shape):
        return pl.BlockSpec((None, None) + shape, lambda i: (layer, i, 0, 0))

    blocks = (2 * _nbytes((t, w), BF16) + _nbytes((t, KV_LORA), F32) + _nbytes((t, LANES), BF16)
              + _nbytes((past, KV_LORA), F32) + _nbytes((past, ROPE_DIM), F32) + _nbytes((t, w), BF16))
    resident = _nbytes(wuk.shape, BF16) + _nbytes(wuv.shape, BF16)
    temps = _nbytes((past, KV_LORA), F32) + 4 * _nbytes((MLA_HEADS * t, past), F32)
    return pl.pallas_call(
        _mla_sample_body,
        grid=(b,),
        in_specs=[per_batch((t, w)), per_batch((t, w)), per_batch((t, KV_LORA)), per_batch((t, LANES)),
                  cached((past, KV_LORA)), cached((past, ROPE_DIM)),
                  _resident(wuk.shape), _resident(wuv.shape)],
        out_specs=per_batch((t, w)),
        out_shape=jax.ShapeDtypeStruct(qn.shape, BF16),
        compiler_params=pltpu.CompilerParams(
            dimension_semantics=("arbitrary",),
            vmem_limit_bytes=_vmem_limit(blocks, resident, temps)),
        name="mla_sample",
    )(qn, qr, ckv_new, kr_new, cache_ckv, cache_kr, wuk, wuv)


def _merge_body(h_ref, g_ref, oa_ref, ob_ref, wga_ref, wgb_ref, woa_ref, wob_ref, wout_ref, o_ref, u_ref, acc_ref):
    j = pl.program_id(1)

    @pl.when(j == 0)
    def _():
        u_ref[...] = _rms(h_ref[...], g_ref[...]).astype(BF16)
        acc_ref[...] = jnp.zeros_like(acc_ref)

    u = u_ref[...]
    m = (jax.nn.sigmoid(_dot(u, wga_ref[...])) * _dot(oa_ref[...], woa_ref[...])
         + jax.nn.sigmoid(_dot(u, wgb_ref[...])) * _dot(ob_ref[...], wob_ref[...]))
    acc_ref[...] += _dot(m.astype(BF16), wout_ref[...])

    @pl.when(j == pl.num_programs(1) - 1)
    def _():
        o_ref[...] = h_ref[...] + acc_ref[...]


def _merge(h, g, oa, ob, wga, wgb, woa, wob, wout):
    n, d = h.shape
    tm = min(ROW_TILE, n)
    tn = _col_tile(d)
    blocks = (2 * _nbytes((tm, d), F32) + 2 * _nbytes((tm, A_WIDTH), BF16) + 2 * _nbytes((d, tn), BF16)
              + 2 * _nbytes((A_WIDTH, tn), BF16) + _nbytes((tn, d), BF16))
    scratch = _nbytes((tm, d), BF16) + _nbytes((tm, d), F32)
    temps = 6 * _nbytes((tm, tn), F32) + _nbytes((tm, d), F32)
    return pl.pallas_call(
        _merge_body,
        grid=(n // tm, d // tn),
        in_specs=[
            pl.BlockSpec((tm, d), lambda i, j: (i, 0)),
            pl.BlockSpec((1, d), lambda i, j: (0, 0)),
            pl.BlockSpec((tm, A_WIDTH), lambda i, j: (i, 0)),
            pl.BlockSpec((tm, MLA_WIDTH), lambda i, j: (i, 0)),
            pl.BlockSpec((d, tn), lambda i, j: (0, j)),
            pl.BlockSpec((d, tn), lambda i, j: (0, j)),
            pl.BlockSpec((A_WIDTH, tn), lambda i, j: (0, j)),
            pl.BlockSpec((MLA_WIDTH, tn), lambda i, j: (0, j)),
            pl.BlockSpec((tn, d), lambda i, j: (j, 0)),
        ],
        out_specs=pl.BlockSpec((tm, d), lambda i, j: (i, 0)),
        out_shape=jax.ShapeDtypeStruct((n, d), F32),
        scratch_shapes=[pltpu.VMEM((tm, d), BF16), pltpu.VMEM((tm, d), F32)],
        compiler_params=pltpu.CompilerParams(
            dimension_semantics=("arbitrary", "arbitrary"),
            vmem_limit_bytes=_vmem_limit(blocks, scratch, temps)),
        name="merge",
    )(h, g, oa, ob, wga, wgb, woa, wob, wout)


def _ple_body(h_ref, p_ref, gp_ref, wpg_ref, wp_ref, gf_ref, y_ref, *, final_norm):
    h = h_ref[...]
    gate = jax.nn.sigmoid(_dot(_rms(h, gp_ref[...]).astype(BF16), wpg_ref[...]))
    h = h + gate * _dot(p_ref[...].astype(BF16), wp_ref[...])
    y_ref[...] = _rms(h, gf_ref[...]) if final_norm else h


def _ple(h, p, layer, gp, wpg, wp, gf, *, final_norm):
    n, d = h.shape
    e = p.shape[2]
    tm = min(ROW_TILE, n)
    blocks = 2 * _nbytes((tm, d), F32) + _nbytes((tm, e), F32)
    resident = _nbytes(wpg.shape, BF16) + _nbytes(wp.shape, BF16)
    temps = 4 * _nbytes((tm, d), F32)
    return pl.pallas_call(
        functools.partial(_ple_body, final_norm=final_norm),
        grid=(n // tm,),
        in_specs=[
            pl.BlockSpec((tm, d), lambda i: (i, 0)),
            pl.BlockSpec((None, tm, e), lambda i: (layer, i, 0)),
            _resident((1, d)),
            _resident(wpg.shape),
            _resident(wp.shape),
            _resident((1, d)),
        ],
        out_specs=pl.BlockSpec((tm, d), lambda i: (i, 0)),
        out_shape=jax.ShapeDtypeStruct((n, d), F32),
        compiler_params=pltpu.CompilerParams(
            dimension_semantics=("arbitrary",),
            vmem_limit_bytes=_vmem_limit(blocks, resident, temps)),
        name="ple",
    )(h, p, gp, wpg, wp, gf)


def _rope_tables(pos):
    half = ROPE_DIM // 2
    inv = ROPE_THETA ** (-jnp.arange(half, dtype=F32) / half)
    ang = pos.astype(F32)[:, None] * inv[None, :]
    cos, sin = jnp.cos(ang), jnp.sin(ang)
    zero = jnp.zeros((pos.shape[0], LANES - ROPE_DIM), F32)
    return jnp.concatenate([cos, cos, zero], axis=1), jnp.concatenate([-sin, sin, zero], axis=1)


def _pad_swap(w, lead):
    half = ROPE_DIM // 2
    x1, x2 = w[..., :half], w[..., half:]
    zero = jnp.zeros(lead + (LANES - ROPE_DIM,), w.dtype)
    return jnp.concatenate([x1, x2, zero], axis=-1), jnp.concatenate([x2, x1, zero], axis=-1)


def _layer_weights(i, d, g_ffn1, w_ffn1_gate, w_ffn1_up, w_ffn1_down, g_mix, w_in, g_cq, w_uq, w_uk, w_uv, g_ckv,
                   rel_bias, w_oa, w_ob, w_out, g_ffn2, w_ffn2_gate, w_ffn2_up, w_ffn2_down, g_ple, w_ple_gate, w_ple):
    bf = lambda a: a.astype(BF16)
    row = lambda a: a.reshape(1, -1).astype(F32)
    w = w_in[i]
    c0 = 3 * A_WIDTH
    c1 = c0 + Q_LORA
    c2 = c1 + KV_LORA
    c3 = c2 + ROPE_DIM
    kr_rot, kr_swp = _pad_swap(w[:, c2:c3], (d,))
    uq = w_uq[i].reshape(Q_LORA, MLA_HEADS, NOPE_DIM + ROPE_DIM)
    uq_rot, uq_swp = _pad_swap(uq[:, :, NOPE_DIM:], (Q_LORA, MLA_HEADS))
    rel = jnp.clip(jnp.arange(TOEPLITZ_W) - (BAND_PAST + BAND_Q - 1), -REL_CLIP, REL_CLIP) + REL_CLIP
    return dict(
        ffn1=(row(g_ffn1[i]), bf(w_ffn1_gate[i]), bf(w_ffn1_up[i]), bf(w_ffn1_down[i])),
        ffn2=(row(g_ffn2[i]), bf(w_ffn2_gate[i]), bf(w_ffn2_up[i]), bf(w_ffn2_down[i])),
        g_mix=row(g_mix[i]),
        inproj=(
            bf(w[:, 0:c0]),
            bf(w[:, c0:c1]),
            bf(jnp.concatenate([w[:, c1:c2], kr_rot, kr_swp], axis=1)),
            row(g_cq[i]),
            row(g_ckv[i]),
            bf(jnp.concatenate([uq[:, :, :NOPE_DIM].reshape(Q_LORA, MLA_WIDTH), uq_rot.reshape(Q_LORA, MLA_WIDTH),
                                uq_swp.reshape(Q_LORA, MLA_WIDTH)], axis=1)),
            bf(jnp.concatenate([jnp.transpose(w_uk[i], (2, 0, 1)).reshape(KV_LORA, MLA_WIDTH),
                                jnp.transpose(w_uv[i], (1, 0, 2)).reshape(KV_LORA, MLA_WIDTH)], axis=1)),
        ),
        rb_rows=rel_bias[i][:, rel].reshape(A_HEADS, 1, TOEPLITZ_W).astype(F32),
        w_uk=bf(w_uk[i]),
        w_uv=bf(w_uv[i]),
        merge=(bf(w[:, c3:c3 + d]), bf(w[:, c3 + d:c3 + 2 * d]), bf(w_oa[i]), bf(w_ob[i]), bf(w_out[i])),
        ple=(row(g_ple[i]), bf(w_ple_gate[i]), bf(w_ple[i])),
    )


def kernel(x_prompt, x_sample, cache_a_k, cache_a_v, cache_mla_ckv, cache_mla_krope, p_prompt, p_sample, g_ffn1, w_ffn1_gate, w_ffn1_up, w_ffn1_down, g_mix, w_in, g_cq, w_uq, w_uk, w_uv, g_ckv, rel_bias, w_oa, w_ob, w_out, g_ffn2, w_ffn2_gate, w_ffn2_up, w_ffn2_down, g_ple, w_ple_gate, w_ple, g_final):
    b, s, d = x_prompt.shape
    bs, t, _ = x_sample.shape
    depth = g_ffn1.shape[0]
    past = cache_mla_ckv.shape[2]
    a_len = cache_a_k.shape[2]
    a_keep = min(BAND_PAST, s)

    cs_p, sn_p = _rope_tables(jnp.arange(s))
    cs_s, sn_s = _rope_tables(past + jnp.arange(t))
    cs_s, sn_s = jnp.tile(cs_s, (bs, 1)), jnp.tile(sn_s, (bs, 1))
    g_fin = g_final.reshape(1, d).astype(F32)

    hp = x_prompt.reshape(b * s, d)
    hs = x_sample.reshape(bs * t, d)
    outs = [[] for _ in range(8)]
    for i in range(depth):
        lw = _layer_weights(i, d, g_ffn1, w_ffn1_gate, w_ffn1_up, w_ffn1_down, g_mix, w_in, g_cq, w_uq, w_uk, w_uv,
                            g_ckv, rel_bias, w_oa, w_ob, w_out, g_ffn2, w_ffn2_gate, w_ffn2_up, w_ffn2_down, g_ple,
                            w_ple_gate, w_ple)

        hp = _ffn(hp, *lw["ffn1"])
        qa, ka, va, qn, qr, kn, vm, krp, ckv, kr, ka_tail, va_tail = _inproj(
            hp, lw["g_mix"], cs_p, sn_p, *lw["inproj"], rows_per_seq=s, tail_rows=a_keep)
        seq = lambda a: a.reshape(b, s, a.shape[-1])
        oa = _band_prompt(seq(qa), seq(ka), seq(va), lw["rb_rows"])
        ob = _mla_prompt(seq(qn), seq(qr), seq(kn), seq(krp), seq(vm))
        hp = _merge(hp, lw["g_mix"], oa.reshape(b * s, A_WIDTH), ob.reshape(b * s, MLA_WIDTH), *lw["merge"])
        outs[0].append(ka_tail.reshape(b, a_keep, A_HEADS, A_HEAD_DIM))
        outs[1].append(va_tail.reshape(b, a_keep, A_HEADS, A_HEAD_DIM))
        outs[2].append(ckv.reshape(b, s, KV_LORA))
        outs[3].append(kr.reshape(b, s, ROPE_DIM))

        hs = _ffn(hs, *lw["ffn1"])
        qa, ka, va, qn, qr, _, _, krp, ckv, kr, ka_new, va_new = _inproj(
            hs, lw["g_mix"], cs_s, sn_s, *lw["inproj"], rows_per_seq=t, tail_rows=t)
        new = lambda a: a.reshape(bs, t, a.shape[-1])
        oa = _band_sample(new(qa), new(ka), new(va), cache_a_k.reshape(depth, bs, a_len, A_WIDTH),
                          cache_a_v.reshape(depth, bs, a_len, A_WIDTH), i, lw["rb_rows"])
        ob = _mla_sample(new(qn), new(qr), new(ckv), new(krp), cache_mla_ckv, cache_mla_krope, i,
                         lw["w_uk"], lw["w_uv"])
        hs = _merge(hs, lw["g_mix"], oa.reshape(bs * t, A_WIDTH), ob.reshape(bs * t, MLA_WIDTH), *lw["merge"])
        outs[4].append(ka_new.reshape(bs, t, A_HEADS, A_HEAD_DIM))
        outs[5].append(va_new.reshape(bs, t, A_HEADS, A_HEAD_DIM))
        outs[6].append(ckv.reshape(bs, t, KV_LORA))
        outs[7].append(kr.reshape(bs, t, ROPE_DIM))

        hp = _ffn(hp, *lw["ffn2"])
        hs = _ffn(hs, *lw["ffn2"])
        last = i == depth - 1
        hp = _ple(hp, p_prompt.reshape(depth, b * s, -1), i, *lw["ple"], g_fin, final_norm=last)
        hs = _ple(hs, p_sample.reshape(depth, bs * t, -1), i, *lw["ple"], g_fin, final_norm=last)

    return (hp.reshape(b, s, d), hs.reshape(bs, t, d), *[jnp.stack(o) for o in outs])
```

```python
import functools

import jax
import jax.numpy as jnp
from jax import lax
from jax.experimental import pallas as pl
from jax.experimental.pallas import tpu as pltpu

F32 = jnp.float32
BF16 = jnp.bfloat16

CHUNK = 64
BAND_PAST = 512
A_HEADS = 8
A_HEAD_DIM = 128
A_WIDTH = A_HEADS * A_HEAD_DIM
REL_CLIP = 128
MLA_HEADS = 8
Q_LORA = 768
KV_LORA = 512
NOPE_DIM = 128
ROPE_DIM = 64
V_DIM = 128
MLA_WIDTH = MLA_HEADS * V_DIM
LOG2E = 1.4426950408889634
MLA_SCALE = (NOPE_DIM + ROPE_DIM) ** -0.5 * LOG2E
A_SCALE = A_HEAD_DIM ** -0.5 * LOG2E
ROPE_THETA = 10000.0
EPS = 1e-6
NEG = -1e30

LANES = 128
V7X_VMEM_BYTES = 64 * 1024 * 1024
VMEM_CEILING = V7X_VMEM_BYTES - 6 * 1024 * 1024

ROW_TILE = 512
FFN_ROW_TILE = 1024
INPROJ_ROW_TILE = 256
COL_TILE = 512
BAND_Q = 4 * CHUNK
BAND_K = BAND_PAST + BAND_Q
TOEPLITZ_W = BAND_K + BAND_Q
MLA_TILE = 512
MLA_WIDE = 1024

def _dot(a, b):
    return jnp.dot(a, b, preferred_element_type=F32)


def _dot_nt(a, b):
    return lax.dot_general(a, b, (((1,), (1,)), ((), ())), preferred_element_type=F32)


def _rms(x, g):
    return x * lax.rsqrt(jnp.mean(x * x, axis=-1, keepdims=True) + EPS) * g


def _vmem_limit(block_bytes, scratch_bytes, temp_bytes):
    return int(min(2 * block_bytes + scratch_bytes + temp_bytes + (4 << 20), VMEM_CEILING))


def _nbytes(shape, dtype):
    n = 1
    for s in shape:
        n *= s
    return n * jnp.dtype(dtype).itemsize


def _resident(shape):
    nd = len(shape)
    return pl.BlockSpec(shape, lambda *_: (0,) * nd, pipeline_mode=pl.Buffered(1))


def _col_tile(n):
    for t in (COL_TILE, 256, LANES):
        if n % t == 0:
            return t
    raise ValueError(f"width {n} is not a multiple of {LANES}")


def _ffn_body(x_ref, g_ref, wg_ref, wu_ref, wd_ref, o_ref, u_ref):
    @pl.when(pl.program_id(1) == 0)
    def _():
        x = x_ref[...]
        u_ref[...] = _rms(x, g_ref[...]).astype(BF16)
        o_ref[...] = x

    u = u_ref[...]
    gate = _dot(u, wg_ref[...])
    up = _dot(u, wu_ref[...])
    half_act = (gate * jax.nn.sigmoid(gate) * (0.5 * up)).astype(BF16)
    d = o_ref.shape[1]
    chunk = _col_tile(d)
    for c in range(0, d, chunk):
        o_ref[:, c:c + chunk] += _dot(half_act, wd_ref[:, c:c + chunk])


def _ffn(x, g, wg, wu, wd):
    n, d = x.shape
    f = wg.shape[1]
    tm = min(FFN_ROW_TILE, n)
    tf = _col_tile(f)
    blocks = _nbytes((tm, d), F32) * 2 + 3 * _nbytes((d, tf), BF16)
    scratch = _nbytes((tm, d), BF16)
    temps = 6 * _nbytes((tm, tf), F32)
    return pl.pallas_call(
        _ffn_body,
        grid=(n // tm, f // tf),
        in_specs=[
            pl.BlockSpec((tm, d), lambda i, j: (i, 0)),
            pl.BlockSpec((1, d), lambda i, j: (0, 0)),
            pl.BlockSpec((d, tf), lambda i, j: (0, j)),
            pl.BlockSpec((d, tf), lambda i, j: (0, j)),
            pl.BlockSpec((tf, d), lambda i, j: (j, 0)),
        ],
        out_specs=pl.BlockSpec((tm, d), lambda i, j: (i, 0)),
        out_shape=jax.ShapeDtypeStruct((n, d), F32),
        scratch_shapes=[pltpu.VMEM((tm, d), BF16)],
        compiler_params=pltpu.CompilerParams(
            dimension_semantics=("arbitrary", "arbitrary"),
            vmem_limit_bytes=_vmem_limit(blocks, scratch, temps)),
        name="ffn",
    )(x, g, wg, wu, wd)


def _inproj_body(x_ref, g_ref, rot_ref, wa_ref, wc_ref, wk_ref, gcq_ref, gckv_ref, wuq_ref, wkv_ref,
                 qa_o, ka_o, va_o, qn_o, qr_o, kn_o, vm_o, krp_o, ckv_o, kr_o, kat_o, vat_o):
    u = _rms(x_ref[...], g_ref[...]).astype(BF16)

    qa_o[...] = (_dot(u, wa_ref[:, 0:A_WIDTH]) * A_SCALE).astype(BF16)
    ka = _dot(u, wa_ref[:, A_WIDTH:2 * A_WIDTH])
    ka_o[...] = ka.astype(BF16)
    kat_o[...] = ka
    va = _dot(u, wa_ref[:, 2 * A_WIDTH:3 * A_WIDTH])
    va_o[...] = va.astype(BF16)
    vat_o[...] = va

    rot = rot_ref[...]
    half = LANES // 2

    cqn = _rms(_dot(u, wc_ref[...]), gcq_ref[...]).astype(BF16)
    qn_o[...] = (_dot(cqn, wuq_ref[:, 0:MLA_WIDTH]) * MLA_SCALE).astype(BF16)
    t = _dot(cqn, wuq_ref[:, MLA_WIDTH:2 * MLA_WIDTH]) * jnp.concatenate([rot] * MLA_HEADS, axis=1)
    qr_o[...] = ((t + pltpu.roll(t, half, axis=1)) * MLA_SCALE).astype(BF16)

    zc = _dot(u, wk_ref[...])
    ckvn = _rms(zc[:, 0:KV_LORA], gckv_ref[...])
    ckv_o[...] = ckvn
    t = zc[:, KV_LORA:KV_LORA + LANES] * rot
    kr = t + pltpu.roll(t, half, axis=1)
    lane = lax.broadcasted_iota(jnp.int32, kr.shape, 1)
    krp_o[...] = jnp.where(lane >= half, kr, 0.0).astype(BF16)
    kr_o[...] = kr[:, 0:ROPE_DIM]

    kv = _dot(ckvn.astype(BF16), wkv_ref[...])
    kn_o[...] = kv[:, 0:MLA_WIDTH].astype(BF16)
    vm_o[...] = kv[:, MLA_WIDTH:2 * MLA_WIDTH].astype(BF16)


def _inproj(x, g, rot, wa, wc, wk, gcq, gckv, wuq, wkv, *, rows_per_seq, tail_rows):
    n, d = x.shape
    tm = min(INPROJ_ROW_TILE, n)
    n_tiles = n // tm
    tab_tiles = rot.shape[0] // tm
    if tm <= rows_per_seq:
        tiles_per_seq = rows_per_seq // tm
        tail_tiles = tail_rows // tm
        n_seq = n // rows_per_seq

        def tail_map(i):
            return (i // tiles_per_seq) * tail_tiles + jnp.maximum(i % tiles_per_seq - (tiles_per_seq - tail_tiles), 0), 0

        tail_n = n_seq * tail_rows
    else:
        assert tail_rows == rows_per_seq

        def tail_map(i):
            return i, 0

        tail_n = n

    def row(i):
        return i, 0

    def tab(i):
        return i % tab_tiles, 0

    wide = pl.BlockSpec((tm, A_WIDTH), row)
    out_shapes = [jax.ShapeDtypeStruct((n, A_WIDTH), BF16)] * 7 + [
        jax.ShapeDtypeStruct((n, LANES), BF16),
        jax.ShapeDtypeStruct((n, KV_LORA), F32),
        jax.ShapeDtypeStruct((n, ROPE_DIM), F32),
        jax.ShapeDtypeStruct((tail_n, A_WIDTH), F32),
        jax.ShapeDtypeStruct((tail_n, A_WIDTH), F32),
    ]
    out_specs = [wide] * 7 + [
        pl.BlockSpec((tm, LANES), row),
        pl.BlockSpec((tm, KV_LORA), row),
        pl.BlockSpec((tm, ROPE_DIM), row),
        pl.BlockSpec((tm, A_WIDTH), tail_map),
        pl.BlockSpec((tm, A_WIDTH), tail_map),
    ]
    weights = [wa, wc, wk, gcq, gckv, wuq, wkv]
    blocks = (_nbytes((tm, d), F32) + 7 * _nbytes((tm, A_WIDTH), BF16) + 3 * _nbytes((tm, A_WIDTH), F32)
              + 4 * _nbytes((tm, LANES), F32))
    resident = sum(_nbytes(w.shape, w.dtype) for w in weights)
    temps = 6 * _nbytes((tm, 2 * A_WIDTH), F32)
    return pl.pallas_call(
        _inproj_body,
        grid=(n_tiles,),
        in_specs=[
            pl.BlockSpec((tm, d), row),
            _resident((1, d)),
            pl.BlockSpec((tm, LANES), tab),
        ] + [_resident(w.shape) for w in weights],
        out_specs=out_specs,
        out_shape=out_shapes,
        compiler_params=pltpu.CompilerParams(
            dimension_semantics=("arbitrary",),
            vmem_limit_bytes=_vmem_limit(blocks, resident, temps)),
        name="inproj",
    )(x, g, rot, *weights)


def _toeplitz(row, nrows):
    x = jnp.broadcast_to(row * LOG2E, (nrows, TOEPLITZ_W))
    left = (BAND_Q - 1) - lax.broadcasted_iota(jnp.int32, (nrows, TOEPLITZ_W), 0)
    for b in range((BAND_Q - 1).bit_length()):
        rolled = pltpu.roll(x, TOEPLITZ_W - (1 << b), axis=1)
        x = jnp.where(((left >> b) & 1) == 1, rolled, x)
    return x


def _band_body(rb_ref, q_ref, k_ref, v_ref, o_ref, t_ref, sa_ref, sb_ref, *, n_tiles):
    @pl.when(pl.program_id(1) == 0)
    def _():
        t = _toeplitz(rb_ref[...], BAND_Q)[:, 0:BAND_K]
        i = lax.broadcasted_iota(jnp.int32, (BAND_Q, BAND_K), 0)
        j = lax.broadcasted_iota(jnp.int32, (BAND_Q, BAND_K), 1)
        first = lax.shift_left(lax.shift_right_logical(i, CHUNK.bit_length() - 1), CHUNK.bit_length() - 1)
        visible = (j >= first) & (j < first + BAND_PAST + CHUNK)
        t_ref[...] = jnp.where(visible, t, NEG)

    ones = jnp.ones((BAND_K, LANES), BF16)
    buffers = (sa_ref, sb_ref)
    past_tiles = BAND_PAST // BAND_Q

    def band(t):
        return max(t - past_tiles, 0) * BAND_Q, min(t + 1, past_tiles + 1) * BAND_Q

    def issue(t):
        k0, nk = band(t)
        s = _dot_nt(q_ref[t * BAND_Q:(t + 1) * BAND_Q, :], k_ref[k0:k0 + nk, :]) + t_ref[:, BAND_K - nk:BAND_K]
        buffers[t % 2][:, 0:nk] = s
        return jnp.max(s, axis=-1, keepdims=True)

    row_max = issue(0)
    for t in range(n_tiles):
        next_max = issue(t + 1) if t + 1 < n_tiles else None
        k0, nk = band(t)
        p = jnp.exp2(buffers[t % 2][:, 0:nk] - row_max).astype(BF16)
        acc = _dot(p, jnp.concatenate([v_ref[k0:k0 + nk, :], ones[0:nk]], axis=1))
        o_ref[t * BAND_Q:(t + 1) * BAND_Q, :] = (acc[:, 0:A_HEAD_DIM] / acc[:, A_HEAD_DIM:]).astype(BF16)
        row_max = next_max


def _band_prompt(q, k, v, rb_rows):
    b, s, _ = q.shape
    assert s % BAND_Q == 0 and s >= BAND_K
    head = pl.BlockSpec((None, s, A_HEAD_DIM), lambda h, i: (i, 0, h))
    blocks = 4 * _nbytes((s, A_HEAD_DIM), BF16)
    scratch = 3 * _nbytes((BAND_Q, BAND_K), F32)
    temps = 16 * _nbytes((BAND_Q, TOEPLITZ_W), F32)
    return pl.pallas_call(
        functools.partial(_band_body, n_tiles=s // BAND_Q),
        grid=(A_HEADS, b),
        in_specs=[pl.BlockSpec((None, 1, TOEPLITZ_W), lambda h, i: (h, 0, 0)), head, head, head],
        out_specs=head,
        out_shape=jax.ShapeDtypeStruct(q.shape, BF16),
        scratch_shapes=[pltpu.VMEM((BAND_Q, BAND_K), F32)] * 3,
        compiler_params=pltpu.CompilerParams(
            dimension_semantics=("arbitrary", "arbitrary"),
            vmem_limit_bytes=_vmem_limit(blocks, scratch, temps)),
        name="band_prompt",
    )(rb_rows, q, k, v)


def _band_sample_body(rb_ref, q_ref, kn_ref, vn_ref, ck_ref, cv_ref, o_ref):
    t_new = q_ref.shape[0]
    n_cache = ck_ref.shape[0]
    for h in range(A_HEADS):
        cols = slice(h * A_HEAD_DIM, (h + 1) * A_HEAD_DIM)
        bias = _toeplitz(rb_ref[h], t_new)
        q = q_ref[:, cols]
        s_c = _dot_nt(q, ck_ref[:, cols].astype(BF16)) + bias[:, BAND_PAST - n_cache:BAND_PAST]
        s_n = _dot_nt(q, kn_ref[:, cols]) + bias[:, BAND_PAST:BAND_PAST + t_new]
        m = jnp.maximum(jnp.max(s_c, axis=-1, keepdims=True), jnp.max(s_n, axis=-1, keepdims=True))
        p_c = jnp.exp2(s_c - m)
        p_n = jnp.exp2(s_n - m)
        l = jnp.sum(p_c, axis=-1, keepdims=True) + jnp.sum(p_n, axis=-1, keepdims=True)
        o = _dot(p_c.astype(BF16), cv_ref[:, cols].astype(BF16)) + _dot(p_n.astype(BF16), vn_ref[:, cols])
        o_ref[:, cols] = (o / l).astype(BF16)


def _band_sample(q, k_new, v_new, cache_k, cache_v, layer, rb_rows):
    b, t, w = q.shape
    a = cache_k.shape[2]
    assert t <= BAND_Q and a <= BAND_PAST
    new = pl.BlockSpec((None, t, w), lambda i: (i, 0, 0))
    cache = pl.BlockSpec((None, None, a, w), lambda i: (layer, i, 0, 0))
    blocks = 4 * _nbytes((t, w), BF16) + 2 * _nbytes((a, w), F32)
    temps = 2 * _nbytes((a, w), F32) + 16 * _nbytes((t, TOEPLITZ_W), F32)
    return pl.pallas_call(
        _band_sample_body,
        grid=(b,),
        in_specs=[_resident(rb_rows.shape), new, new, new, cache, cache],
        out_specs=new,
        out_shape=jax.ShapeDtypeStruct(q.shape, BF16),
        compiler_params=pltpu.CompilerParams(
            dimension_semantics=("arbitrary",),
            vmem_limit_bytes=_vmem_limit(blocks, _nbytes(rb_rows.shape, F32), temps)),
        name="band_sample",
    )(rb_rows, q, k_new, v_new, cache_k, cache_v)


def _mla_body(qn_ref, qr_ref, kn_ref, kr_ref, v_ref, o_ref, sa_ref, sb_ref, *, n_tiles):
    tile = MLA_TILE
    r = lax.broadcasted_iota(jnp.int32, (tile, tile), 0)
    c = lax.broadcasted_iota(jnp.int32, (tile, tile), 1)
    chunk_bits = CHUNK.bit_length() - 1
    diag_visible = lax.shift_right_logical(c, chunk_bits) <= lax.shift_right_logical(r, chunk_bits)
    ones = jnp.ones((MLA_WIDE, LANES), BF16)
    buffers = (sa_ref, sb_ref)

    def row_max(s):
        return jnp.max(s, axis=-1, keepdims=True)

    for i in range(n_tiles):
        r0 = i * tile
        q = jnp.concatenate([qn_ref[r0:r0 + tile, :], qr_ref[r0:r0 + tile, :]], axis=1)
        blocks = [(c0, min(MLA_WIDE, r0 - c0)) for c0 in range(0, r0, MLA_WIDE)] + [(r0, tile)]

        def issue(k):
            c0, w = blocks[k]
            keys = jnp.concatenate([kn_ref[c0:c0 + w, :], kr_ref[c0:c0 + w, :]], axis=1)
            s = _dot_nt(q, keys)
            buffers[k % 2][:, 0:w] = s
            return row_max(s)

        m = jnp.full((tile, 1), NEG, F32)
        acc = jnp.zeros((tile, 2 * LANES), F32)
        s_max = issue(0)
        for k, (c0, w) in enumerate(blocks):
            last = k == len(blocks) - 1
            next_max = None if last else issue(k + 1)
            s = buffers[k % 2][:, 0:w]
            if last:
                s = jnp.where(diag_visible, s, NEG)
                s_max = row_max(s)
            m_new = jnp.maximum(m, s_max)
            p = jnp.exp2(s - m_new).astype(BF16)
            values = jnp.concatenate([v_ref[c0:c0 + w, :], ones[0:w]], axis=1)
            acc = jnp.exp2(m - m_new) * acc + _dot(p, values)
            m, s_max = m_new, next_max
        o_ref[r0:r0 + tile, :] = (acc[:, 0:V_DIM] / acc[:, V_DIM:]).astype(BF16)


def _mla_prompt(qn, qr, kn, kr, v):
    b, s, _ = qn.shape
    assert s % MLA_TILE == 0
    head = pl.BlockSpec((None, s, LANES), lambda i, h: (i, 0, h))
    shared = pl.BlockSpec((None, s, LANES), lambda i, h: (i, 0, 0))
    blocks = 6 * _nbytes((s, LANES), BF16)
    scratch = 2 * _nbytes((MLA_TILE, MLA_WIDE), F32)
    temps = 6 * _nbytes((MLA_TILE, MLA_WIDE), F32)
    return pl.pallas_call(
        functools.partial(_mla_body, n_tiles=s // MLA_TILE),
        grid=(b, MLA_HEADS),
        in_specs=[head, head, head, shared, head],
        out_specs=head,
        out_shape=jax.ShapeDtypeStruct(qn.shape, BF16),
        scratch_shapes=[pltpu.VMEM((MLA_TILE, MLA_WIDE), F32)] * 2,
        compiler_params=pltpu.CompilerParams(
            dimension_semantics=("arbitrary", "arbitrary"),
            vmem_limit_bytes=_vmem_limit(blocks, scratch, temps)),
        name="mla_prompt",
    )(qn, qr, kn, kr, v)


def _mla_sample_body(qn_ref, qr_ref, cn_ref, krn_ref, cc_ref, ckr_ref, wuk_ref, wuv_ref, o_ref):
    t_new = qn_ref.shape[0]
    q_lat = jnp.concatenate(
        [_dot(qn_ref[:, h * NOPE_DIM:(h + 1) * NOPE_DIM], wuk_ref[h]).astype(BF16) for h in range(MLA_HEADS)], axis=0)
    q_rot = jnp.concatenate(
        [qr_ref[:, (h + 1) * LANES - ROPE_DIM:(h + 1) * LANES] for h in range(MLA_HEADS)], axis=0)
    cache = cc_ref[...].astype(BF16)
    new = cn_ref[...].astype(BF16)
    s_c = _dot_nt(q_lat, cache) + _dot_nt(q_rot, ckr_ref[...].astype(BF16))
    s_n = _dot_nt(q_lat, new) + _dot_nt(q_rot, krn_ref[:, LANES - ROPE_DIM:LANES])
    m = jnp.maximum(jnp.max(s_c, axis=-1, keepdims=True), jnp.max(s_n, axis=-1, keepdims=True))
    p_c = jnp.exp2(s_c - m)
    p_n = jnp.exp2(s_n - m)
    l = jnp.sum(p_c, axis=-1, keepdims=True) + jnp.sum(p_n, axis=-1, keepdims=True)
    o_lat = ((_dot(p_c.astype(BF16), cache) + _dot(p_n.astype(BF16), new)) / l).astype(BF16)
    for h in range(MLA_HEADS):
        o_ref[:, h * V_DIM:(h + 1) * V_DIM] = _dot(o_lat[h * t_new:(h + 1) * t_new, :], wuv_ref[h]).astype(BF16)


def _mla_sample(qn, qr, ckv_new, kr_new, cache_ckv, cache_kr, layer, wuk, wuv):
    b, t, w = qn.shape
    past = cache_ckv.shape[2]

    def per_batch(shape):
        return pl.BlockSpec((None,) + shape, lambda i: (i, 0, 0))

    def cached(shape):
        return pl.BlockSpec((None, None) + shape, lambda i: (layer, i, 0, 0))

    blocks = (2 * _nbytes((t, w), BF16) + _nbytes((t, KV_LORA), F32) + _nbytes((t, LANES), BF16)
              + _nbytes((past, KV_LORA), F32) + _nbytes((past, ROPE_DIM), F32) + _nbytes((t, w), BF16))
    resident = _nbytes(wuk.shape, BF16) + _nbytes(wuv.shape, BF16)
    temps = _nbytes((past, KV_LORA), F32) + 4 * _nbytes((MLA_HEADS * t, past), F32)
    return pl.pallas_call(
        _mla_sample_body,
        grid=(b,),
        in_specs=[per_batch((t, w)), per_batch((t, w)), per_batch((t, KV_LORA)), per_batch((t, LANES)),
                  cached((past, KV_LORA)), cached((past, ROPE_DIM)),
                  _resident(wuk.shape), _resident(wuv.shape)],
        out_specs=per_batch((t, w)),
        out_shape=jax.ShapeDtypeStruct(qn.shape, BF16),
        compiler_params=pltpu.CompilerParams(
            dimension_semantics=("arbitrary",),
            vmem_limit_bytes=_vmem_limit(blocks, resident, temps)),
        name="mla_sample",
    )(qn, qr, ckv_new, kr_new, cache_ckv, cache_kr, wuk, wuv)


def _merge_body(h_ref, g_ref, oa_ref, ob_ref, wga_ref, wgb_ref, woa_ref, wob_ref, wout_ref, o_ref, u_ref):
    @pl.when(pl.program_id(1) == 0)
    def _():
        h = h_ref[...]
        u_ref[...] = _rms(h, g_ref[...]).astype(BF16)
        o_ref[...] = h

    u = u_ref[...]
    m = (jax.nn.sigmoid(_dot(u, wga_ref[...])) * _dot(oa_ref[...], woa_ref[...])
         + jax.nn.sigmoid(_dot(u, wgb_ref[...])) * _dot(ob_ref[...], wob_ref[...])).astype(BF16)
    d = o_ref.shape[1]
    chunk = _col_tile(d)
    for c in range(0, d, chunk):
        o_ref[:, c:c + chunk] += _dot(m, wout_ref[:, c:c + chunk])


def _merge(h, g, oa, ob, wga, wgb, woa, wob, wout):
    n, d = h.shape
    tm = min(ROW_TILE, n)
    tn = _col_tile(d)
    blocks = (2 * _nbytes((tm, d), F32) + 2 * _nbytes((tm, A_WIDTH), BF16) + 2 * _nbytes((d, tn), BF16)
              + 2 * _nbytes((A_WIDTH, tn), BF16) + _nbytes((tn, d), BF16))
    scratch = _nbytes((tm, d), BF16)
    temps = 8 * _nbytes((tm, tn), F32)
    return pl.pallas_call(
        _merge_body,
        grid=(n // tm, d // tn),
        in_specs=[
            pl.BlockSpec((tm, d), lambda i, j: (i, 0)),
            pl.BlockSpec((1, d), lambda i, j: (0, 0)),
            pl.BlockSpec((tm, A_WIDTH), lambda i, j: (i, 0)),
            pl.BlockSpec((tm, MLA_WIDTH), lambda i, j: (i, 0)),
            pl.BlockSpec((d, tn), lambda i, j: (0, j)),
            pl.BlockSpec((d, tn), lambda i, j: (0, j)),
            pl.BlockSpec((A_WIDTH, tn), lambda i, j: (0, j)),
            pl.BlockSpec((MLA_WIDTH, tn), lambda i, j: (0, j)),
            pl.BlockSpec((tn, d), lambda i, j: (j, 0)),
        ],
        out_specs=pl.BlockSpec((tm, d), lambda i, j: (i, 0)),
        out_shape=jax.ShapeDtypeStruct((n, d), F32),
        scratch_shapes=[pltpu.VMEM((tm, d), BF16)],
        compiler_params=pltpu.CompilerParams(
            dimension_semantics=("arbitrary", "arbitrary"),
            vmem_limit_bytes=_vmem_limit(blocks, scratch, temps)),
        name="merge",
    )(h, g, oa, ob, wga, wgb, woa, wob, wout)


def _ple_body(h_ref, p_ref, gp_ref, wpg_ref, wp_ref, gf_ref, y_ref, *, final_norm):
    h = h_ref[...]
    gate = jax.nn.sigmoid(_dot(_rms(h, gp_ref[...]).astype(BF16), wpg_ref[...]))
    h = h + gate * _dot(p_ref[...].astype(BF16), wp_ref[...])
    y_ref[...] = _rms(h, gf_ref[...]) if final_norm else h


def _ple(h, p, layer, gp, wpg, wp, gf, *, final_norm):
    n, d = h.shape
    e = p.shape[2]
    tm = min(ROW_TILE, n)
    blocks = 2 * _nbytes((tm, d), F32) + _nbytes((tm, e), F32)
    resident = _nbytes(wpg.shape, BF16) + _nbytes(wp.shape, BF16)
    temps = 4 * _nbytes((tm, d), F32)
    return pl.pallas_call(
        functools.partial(_ple_body, final_norm=final_norm),
        grid=(n // tm,),
        in_specs=[
            pl.BlockSpec((tm, d), lambda i: (i, 0)),
            pl.BlockSpec((None, tm, e), lambda i: (layer, i, 0)),
            _resident((1, d)),
            _resident(wpg.shape),
            _resident(wp.shape),
            _resident((1, d)),
        ],
        out_specs=pl.BlockSpec((tm, d), lambda i: (i, 0)),
        out_shape=jax.ShapeDtypeStruct((n, d), F32),
        compiler_params=pltpu.CompilerParams(
            dimension_semantics=("arbitrary",),
            vmem_limit_bytes=_vmem_limit(blocks, resident, temps)),
        name="ple",
    )(h, p, gp, wpg, wp, gf)


def _rope_table(pos):
    half = ROPE_DIM // 2
    inv = ROPE_THETA ** (-jnp.arange(half, dtype=F32) / half)
    ang = pos.astype(F32)[:, None] * inv[None, :]
    cos, sin = jnp.cos(ang), jnp.sin(ang)
    return jnp.concatenate([cos, cos, -sin, sin], axis=1)


def _rope_columns(w):
    half = ROPE_DIM // 2
    x1, x2 = w[..., :half], w[..., half:]
    return jnp.concatenate([x1, x2, x2, x1], axis=-1)


def _layer_weights(i, d, g_ffn1, w_ffn1_gate, w_ffn1_up, w_ffn1_down, g_mix, w_in, g_cq, w_uq, w_uk, w_uv, g_ckv,
                   rel_bias, w_oa, w_ob, w_out, g_ffn2, w_ffn2_gate, w_ffn2_up, w_ffn2_down, g_ple, w_ple_gate, w_ple):
    bf = lambda a: a.astype(BF16)
    row = lambda a: a.reshape(1, -1).astype(F32)
    w = w_in[i]
    c0 = 3 * A_WIDTH
    c1 = c0 + Q_LORA
    c2 = c1 + KV_LORA
    c3 = c2 + ROPE_DIM
    uq = w_uq[i].reshape(Q_LORA, MLA_HEADS, NOPE_DIM + ROPE_DIM)
    rel = jnp.clip(jnp.arange(TOEPLITZ_W) - (BAND_PAST + BAND_Q - 1), -REL_CLIP, REL_CLIP) + REL_CLIP
    return dict(
        ffn1=(row(g_ffn1[i]), bf(w_ffn1_gate[i]), bf(w_ffn1_up[i]), bf(w_ffn1_down[i])),
        ffn2=(row(g_ffn2[i]), bf(w_ffn2_gate[i]), bf(w_ffn2_up[i]), bf(w_ffn2_down[i])),
        g_mix=row(g_mix[i]),
        inproj=(
            bf(w[:, 0:c0]),
            bf(w[:, c0:c1]),
            bf(jnp.concatenate([w[:, c1:c2], _rope_columns(w[:, c2:c3])], axis=1)),
            row(g_cq[i]),
            row(g_ckv[i]),
            bf(jnp.concatenate([uq[:, :, :NOPE_DIM].reshape(Q_LORA, MLA_WIDTH),
                                _rope_columns(uq[:, :, NOPE_DIM:]).reshape(Q_LORA, MLA_WIDTH)], axis=1)),
            bf(jnp.concatenate([jnp.transpose(w_uk[i], (2, 0, 1)).reshape(KV_LORA, MLA_WIDTH),
                                jnp.transpose(w_uv[i], (1, 0, 2)).reshape(KV_LORA, MLA_WIDTH)], axis=1)),
        ),
        rb_rows=rel_bias[i][:, rel].reshape(A_HEADS, 1, TOEPLITZ_W).astype(F32),
        w_uk=bf(w_uk[i]),
        w_uv=bf(w_uv[i]),
        merge=(bf(w[:, c3:c3 + d]), bf(w[:, c3 + d:c3 + 2 * d]), bf(w_oa[i]), bf(w_ob[i]), bf(w_out[i])),
        ple=(row(g_ple[i]), bf(w_ple_gate[i]), bf(w_ple[i])),
    )


def kernel(x_prompt, x_sample, cache_a_k, cache_a_v, cache_mla_ckv, cache_mla_krope, p_prompt, p_sample, g_ffn1, w_ffn1_gate, w_ffn1_up, w_ffn1_down, g_mix, w_in, g_cq, w_uq, w_uk, w_uv, g_ckv, rel_bias, w_oa, w_ob, w_out, g_ffn2, w_ffn2_gate, w_ffn2_up, w_ffn2_down, g_ple, w_ple_gate, w_ple, g_final):
    b, s, d = x_prompt.shape
    bs, t, _ = x_sample.shape
    depth = g_ffn1.shape[0]
    past = cache_mla_ckv.shape[2]
    a_len = cache_a_k.shape[2]
    a_keep = min(BAND_PAST, s)

    rot_p = _rope_table(jnp.arange(s))
    rot_s = jnp.tile(_rope_table(past + jnp.arange(t)), (bs, 1))
    g_fin = g_final.reshape(1, d).astype(F32)

    hp = x_prompt.reshape(b * s, d)
    hs = x_sample.reshape(bs * t, d)
    outs = [[] for _ in range(8)]
    for i in range(depth):
        lw = _layer_weights(i, d, g_ffn1, w_ffn1_gate, w_ffn1_up, w_ffn1_down, g_mix, w_in, g_cq, w_uq, w_uk, w_uv,
                            g_ckv, rel_bias, w_oa, w_ob, w_out, g_ffn2, w_ffn2_gate, w_ffn2_up, w_ffn2_down, g_ple,
                            w_ple_gate, w_ple)

        hp = _ffn(hp, *lw["ffn1"])
        qa, ka, va, qn, qr, kn, vm, krp, ckv, kr, ka_tail, va_tail = _inproj(
            hp, lw["g_mix"], rot_p, *lw["inproj"], rows_per_seq=s, tail_rows=a_keep)
        seq = lambda a: a.reshape(b, s, a.shape[-1])
        oa = _band_prompt(seq(qa), seq(ka), seq(va), lw["rb_rows"])
        ob = _mla_prompt(seq(qn), seq(qr), seq(kn), seq(krp), seq(vm))
        hp = _merge(hp, lw["g_mix"], oa.reshape(b * s, A_WIDTH), ob.reshape(b * s, MLA_WIDTH), *lw["merge"])
        outs[0].append(ka_tail.reshape(b, a_keep, A_HEADS, A_HEAD_DIM))
        outs[1].append(va_tail.reshape(b, a_keep, A_HEADS, A_HEAD_DIM))
        outs[2].append(ckv.reshape(b, s, KV_LORA))
        outs[3].append(kr.reshape(b, s, ROPE_DIM))

        hs = _ffn(hs, *lw["ffn1"])
        qa, ka, va, qn, qr, _, _, krp, ckv, kr, ka_new, va_new = _inproj(
            hs, lw["g_mix"], rot_s, *lw["inproj"], rows_per_seq=t, tail_rows=t)
        new = lambda a: a.reshape(bs, t, a.shape[-1])
        oa = _band_sample(new(qa), new(ka), new(va), cache_a_k.reshape(depth, bs, a_len, A_WIDTH),
                          cache_a_v.reshape(depth, bs, a_len, A_WIDTH), i, lw["rb_rows"])
        ob = _mla_sample(new(qn), new(qr), new(ckv), new(krp), cache_mla_ckv, cache_mla_krope, i,
                         lw["w_uk"], lw["w_uv"])
        hs = _merge(hs, lw["g_mix"], oa.reshape(bs * t, A_WIDTH), ob.reshape(bs * t, MLA_WIDTH), *lw["merge"])
        outs[4].append(ka_new.reshape(bs, t, A_HEADS, A_HEAD_DIM))
        outs[5].append(va_new.reshape(bs, t, A_HEADS, A_HEAD_DIM))
        outs[6].append(ckv.reshape(bs, t, KV_LORA))
        outs[7].append(kr.reshape(bs, t, ROPE_DIM))

        hp = _ffn(hp, *lw["ffn2"])
        hs = _ffn(hs, *lw["ffn2"])
        last = i == depth - 1
        hp = _ple(hp, p_prompt.reshape(depth, b * s, -1), i, *lw["ple"], g_fin, final_norm=last)
        hs = _ple(hs, p_sample.reshape(depth, bs * t, -1), i, *lw["ple"], g_fin, final_norm=last)

    stacked = [o[0][None] if depth == 1 else jnp.stack(o) for o in outs]
    return (hp.reshape(b, s, d), hs.reshape(bs, t, d), *stacked)
```

```python
import functools

import jax
import jax.numpy as jnp
from jax import lax
from jax.experimental import pallas as pl
from jax.experimental.pallas import tpu as pltpu

F32 = jnp.float32
BF16 = jnp.bfloat16

CHUNK = 64
BAND_PAST = 512
A_HEADS = 8
A_HEAD_DIM = 128
A_WIDTH = A_HEADS * A_HEAD_DIM
REL_CLIP = 128
MLA_HEADS = 8
Q_LORA = 768
KV_LORA = 512
NOPE_DIM = 128
ROPE_DIM = 64
V_DIM = 128
MLA_WIDTH = MLA_HEADS * V_DIM
LOG2E = 1.4426950408889634
MLA_SCALE = (NOPE_DIM + ROPE_DIM) ** -0.5 * LOG2E
A_SCALE = A_HEAD_DIM ** -0.5 * LOG2E
ROPE_THETA = 10000.0
EPS = 1e-6
NEG = -1e30

LANES = 128
V7X_VMEM_BYTES = 64 * 1024 * 1024
VMEM_CEILING = V7X_VMEM_BYTES - 6 * 1024 * 1024

ROW_TILE = 512
MERGE_ROW_TILE = 1024
MERGE_COL_TILE = 256
FFN_ROW_TILE = 1024
INPROJ_ROW_TILE = 256
COL_TILE = 512
BAND_Q = 4 * CHUNK
BAND_K = BAND_PAST + BAND_Q
TOEPLITZ_W = BAND_K + BAND_Q
MLA_TILE = 512
MLA_WIDE = 1024

def _dot(a, b):
    return jnp.dot(a, b, preferred_element_type=F32)


def _dot_nt(a, b):
    return lax.dot_general(a, b, (((1,), (1,)), ((), ())), preferred_element_type=F32)


def _rms(x, g):
    return x * lax.rsqrt(jnp.mean(x * x, axis=-1, keepdims=True) + EPS) * g


def _vmem_limit(block_bytes, scratch_bytes, temp_bytes):
    return int(min(2 * block_bytes + scratch_bytes + temp_bytes + (4 << 20), VMEM_CEILING))


def _nbytes(shape, dtype):
    n = 1
    for s in shape:
        n *= s
    return n * jnp.dtype(dtype).itemsize


def _resident(shape):
    nd = len(shape)
    return pl.BlockSpec(shape, lambda *_: (0,) * nd, pipeline_mode=pl.Buffered(1))


def _col_tile(n):
    for t in (COL_TILE, 256, LANES):
        if n % t == 0:
            return t
    raise ValueError(f"width {n} is not a multiple of {LANES}")


def _ffn_body(x_ref, g_ref, wg_ref, wu_ref, wd_ref, o_ref, u_ref):
    @pl.when(pl.program_id(1) == 0)
    def _():
        x = x_ref[...]
        u_ref[...] = _rms(x, g_ref[...]).astype(BF16)
        o_ref[...] = x

    u = u_ref[...]
    gate = _dot(u, wg_ref[...])
    up = _dot(u, wu_ref[...])
    half_act = (gate * jax.nn.sigmoid(gate) * (0.5 * up)).astype(BF16)
    d = o_ref.shape[1]
    chunk = _col_tile(d)
    for c in range(0, d, chunk):
        o_ref[:, c:c + chunk] += _dot(half_act, wd_ref[:, c:c + chunk])


def _ffn(x, g, wg, wu, wd):
    n, d = x.shape
    f = wg.shape[1]
    tm = min(FFN_ROW_TILE, n)
    tf = _col_tile(f)
    blocks = _nbytes((tm, d), F32) * 2 + 3 * _nbytes((d, tf), BF16)
    scratch = _nbytes((tm, d), BF16)
    temps = 6 * _nbytes((tm, tf), F32)
    return pl.pallas_call(
        _ffn_body,
        grid=(n // tm, f // tf),
        in_specs=[
            pl.BlockSpec((tm, d), lambda i, j: (i, 0)),
            pl.BlockSpec((1, d), lambda i, j: (0, 0)),
            pl.BlockSpec((d, tf), lambda i, j: (0, j)),
            pl.BlockSpec((d, tf), lambda i, j: (0, j)),
            pl.BlockSpec((tf, d), lambda i, j: (j, 0)),
        ],
        out_specs=pl.BlockSpec((tm, d), lambda i, j: (i, 0)),
        out_shape=jax.ShapeDtypeStruct((n, d), F32),
        scratch_shapes=[pltpu.VMEM((tm, d), BF16)],
        compiler_params=pltpu.CompilerParams(
            dimension_semantics=("arbitrary", "arbitrary"),
            vmem_limit_bytes=_vmem_limit(blocks, scratch, temps)),
        name="ffn",
    )(x, g, wg, wu, wd)


def _inproj_body(x_ref, g_ref, rot_ref, wa_ref, wc_ref, wk_ref, gcq_ref, gckv_ref, wuq_ref, wkv_ref,
                 qa_o, ka_o, va_o, qn_o, qr_o, kn_o, vm_o, krp_o, ckv_o, kr_o, kat_o, vat_o, u_o):
    u = _rms(x_ref[...], g_ref[...]).astype(BF16)
    u_o[...] = u

    qa_o[...] = (_dot(u, wa_ref[:, 0:A_WIDTH]) * A_SCALE).astype(BF16)
    ka = _dot(u, wa_ref[:, A_WIDTH:2 * A_WIDTH])
    ka_o[...] = ka.astype(BF16)
    kat_o[...] = ka
    va = _dot(u, wa_ref[:, 2 * A_WIDTH:3 * A_WIDTH])
    va_o[...] = va.astype(BF16)
    vat_o[...] = va

    rot = rot_ref[...]
    half = LANES // 2

    cqn = _rms(_dot(u, wc_ref[...]), gcq_ref[...]).astype(BF16)
    qn_o[...] = (_dot(cqn, wuq_ref[:, 0:MLA_WIDTH]) * MLA_SCALE).astype(BF16)
    t = _dot(cqn, wuq_ref[:, MLA_WIDTH:2 * MLA_WIDTH]) * jnp.concatenate([rot] * MLA_HEADS, axis=1)
    qr_o[...] = ((t + pltpu.roll(t, half, axis=1)) * MLA_SCALE).astype(BF16)

    zc = _dot(u, wk_ref[...])
    ckvn = _rms(zc[:, 0:KV_LORA], gckv_ref[...])
    ckv_o[...] = ckvn
    t = zc[:, KV_LORA:KV_LORA + LANES] * rot
    kr = t + pltpu.roll(t, half, axis=1)
    lane = lax.broadcasted_iota(jnp.int32, kr.shape, 1)
    krp_o[...] = jnp.where(lane >= half, kr, 0.0).astype(BF16)
    kr_o[...] = kr[:, 0:ROPE_DIM]

    kv = _dot(ckvn.astype(BF16), wkv_ref[...])
    kn_o[...] = kv[:, 0:MLA_WIDTH].astype(BF16)
    vm_o[...] = kv[:, MLA_WIDTH:2 * MLA_WIDTH].astype(BF16)


def _inproj(x, g, rot, wa, wc, wk, gcq, gckv, wuq, wkv, *, rows_per_seq, tail_rows):
    n, d = x.shape
    tm = min(INPROJ_ROW_TILE, n)
    n_tiles = n // tm
    tab_tiles = rot.shape[0] // tm
    if tm <= rows_per_seq:
        tiles_per_seq = rows_per_seq // tm
        tail_tiles = tail_rows // tm
        n_seq = n // rows_per_seq

        def tail_map(i):
            return (i // tiles_per_seq) * tail_tiles + jnp.maximum(i % tiles_per_seq - (tiles_per_seq - tail_tiles), 0), 0

        tail_n = n_seq * tail_rows
    else:
        assert tail_rows == rows_per_seq

        def tail_map(i):
            return i, 0

        tail_n = n

    def row(i):
        return i, 0

    def tab(i):
        return i % tab_tiles, 0

    wide = pl.BlockSpec((tm, A_WIDTH), row)
    out_shapes = [jax.ShapeDtypeStruct((n, A_WIDTH), BF16)] * 7 + [
        jax.ShapeDtypeStruct((n, LANES), BF16),
        jax.ShapeDtypeStruct((n, KV_LORA), F32),
        jax.ShapeDtypeStruct((n, ROPE_DIM), F32),
        jax.ShapeDtypeStruct((tail_n, A_WIDTH), F32),
        jax.ShapeDtypeStruct((tail_n, A_WIDTH), F32),
        jax.ShapeDtypeStruct((n, d), BF16),
    ]
    out_specs = [wide] * 7 + [
        pl.BlockSpec((tm, LANES), row),
        pl.BlockSpec((tm, KV_LORA), row),
        pl.BlockSpec((tm, ROPE_DIM), row),
        pl.BlockSpec((tm, A_WIDTH), tail_map),
        pl.BlockSpec((tm, A_WIDTH), tail_map),
        pl.BlockSpec((tm, d), row),
    ]
    weights = [wa, wc, wk, gcq, gckv, wuq, wkv]
    blocks = (_nbytes((tm, d), F32) + _nbytes((tm, d), BF16) + 7 * _nbytes((tm, A_WIDTH), BF16)
              + 3 * _nbytes((tm, A_WIDTH), F32)
              + 4 * _nbytes((tm, LANES), F32))
    resident = sum(_nbytes(w.shape, w.dtype) for w in weights)
    temps = 6 * _nbytes((tm, 2 * A_WIDTH), F32)
    return pl.pallas_call(
        _inproj_body,
        grid=(n_tiles,),
        in_specs=[
            pl.BlockSpec((tm, d), row),
            _resident((1, d)),
            pl.BlockSpec((tm, LANES), tab),
        ] + [_resident(w.shape) for w in weights],
        out_specs=out_specs,
        out_shape=out_shapes,
        compiler_params=pltpu.CompilerParams(
            dimension_semantics=("arbitrary",),
            vmem_limit_bytes=_vmem_limit(blocks, resident, temps)),
        name="inproj",
    )(x, g, rot, *weights)


def _toeplitz(row, nrows):
    x = jnp.broadcast_to(row * LOG2E, (nrows, TOEPLITZ_W))
    left = (BAND_Q - 1) - lax.broadcasted_iota(jnp.int32, (nrows, TOEPLITZ_W), 0)
    for b in range((BAND_Q - 1).bit_length()):
        rolled = pltpu.roll(x, TOEPLITZ_W - (1 << b), axis=1)
        x = jnp.where(((left >> b) & 1) == 1, rolled, x)
    return x


def _band_body(rb_ref, q_ref, k_ref, v_ref, o_ref, t_ref, sa_ref, sb_ref, *, n_tiles):
    @pl.when(pl.program_id(1) == 0)
    def _():
        t = _toeplitz(rb_ref[...], BAND_Q)[:, 0:BAND_K]
        i = lax.broadcasted_iota(jnp.int32, (BAND_Q, BAND_K), 0)
        j = lax.broadcasted_iota(jnp.int32, (BAND_Q, BAND_K), 1)
        first = lax.shift_left(lax.shift_right_logical(i, CHUNK.bit_length() - 1), CHUNK.bit_length() - 1)
        visible = (j >= first) & (j < first + BAND_PAST + CHUNK)
        t_ref[...] = jnp.where(visible, t, NEG)

    ones = jnp.ones((BAND_K, LANES), BF16)
    buffers = (sa_ref, sb_ref)
    past_tiles = BAND_PAST // BAND_Q

    def band(t):
        return max(t - past_tiles, 0) * BAND_Q, min(t + 1, past_tiles + 1) * BAND_Q

    def issue(t):
        k0, nk = band(t)
        s = _dot_nt(q_ref[t * BAND_Q:(t + 1) * BAND_Q, :], k_ref[k0:k0 + nk, :]) + t_ref[:, BAND_K - nk:BAND_K]
        buffers[t % 2][:, 0:nk] = s
        return jnp.max(s, axis=-1, keepdims=True)

    row_max = issue(0)
    for t in range(n_tiles):
        next_max = issue(t + 1) if t + 1 < n_tiles else None
        k0, nk = band(t)
        p = jnp.exp2(buffers[t % 2][:, 0:nk] - row_max).astype(BF16)
        acc = _dot(p, jnp.concatenate([v_ref[k0:k0 + nk, :], ones[0:nk]], axis=1))
        o_ref[t * BAND_Q:(t + 1) * BAND_Q, :] = (acc[:, 0:A_HEAD_DIM] / acc[:, A_HEAD_DIM:]).astype(BF16)
        row_max = next_max


def _band_prompt(q, k, v, rb_rows):
    b, s, _ = q.shape
    assert s % BAND_Q == 0 and s >= BAND_K
    head = pl.BlockSpec((None, s, A_HEAD_DIM), lambda h, i: (i, 0, h))
    blocks = 4 * _nbytes((s, A_HEAD_DIM), BF16)
    scratch = 3 * _nbytes((BAND_Q, BAND_K), F32)
    temps = 16 * _nbytes((BAND_Q, TOEPLITZ_W), F32)
    return pl.pallas_call(
        functools.partial(_band_body, n_tiles=s // BAND_Q),
        grid=(A_HEADS, b),
        in_specs=[pl.BlockSpec((None, 1, TOEPLITZ_W), lambda h, i: (h, 0, 0)), head, head, head],
        out_specs=head,
        out_shape=jax.ShapeDtypeStruct(q.shape, BF16),
        scratch_shapes=[pltpu.VMEM((BAND_Q, BAND_K), F32)] * 3,
        compiler_params=pltpu.CompilerParams(
            dimension_semantics=("arbitrary", "arbitrary"),
            vmem_limit_bytes=_vmem_limit(blocks, scratch, temps)),
        name="band_prompt",
    )(rb_rows, q, k, v)


def _band_sample_body(rb_ref, q_ref, kn_ref, vn_ref, ck_ref, cv_ref, o_ref):
    t_new = q_ref.shape[0]
    n_cache = ck_ref.shape[0]
    for h in range(A_HEADS):
        cols = slice(h * A_HEAD_DIM, (h + 1) * A_HEAD_DIM)
        bias = _toeplitz(rb_ref[h], t_new)
        q = q_ref[:, cols]
        s_c = _dot_nt(q, ck_ref[:, h, :].astype(BF16)) + bias[:, BAND_PAST - n_cache:BAND_PAST]
        s_n = _dot_nt(q, kn_ref[:, cols]) + bias[:, BAND_PAST:BAND_PAST + t_new]
        m = jnp.maximum(jnp.max(s_c, axis=-1, keepdims=True), jnp.max(s_n, axis=-1, keepdims=True))
        p_c = jnp.exp2(s_c - m)
        p_n = jnp.exp2(s_n - m)
        l = jnp.sum(p_c, axis=-1, keepdims=True) + jnp.sum(p_n, axis=-1, keepdims=True)
        o = _dot(p_c.astype(BF16), cv_ref[:, h, :].astype(BF16)) + _dot(p_n.astype(BF16), vn_ref[:, cols])
        o_ref[:, cols] = (o / l).astype(BF16)


def _band_sample(q, k_new, v_new, cache_k, cache_v, layer, rb_rows):
    b, t, w = q.shape
    a = cache_k.shape[2]
    assert t <= BAND_Q and a <= BAND_PAST
    new = pl.BlockSpec((None, t, w), lambda i: (i, 0, 0))
    cache = pl.BlockSpec((None, None, a, A_HEADS, A_HEAD_DIM), lambda i: (layer, i, 0, 0, 0))
    blocks = 4 * _nbytes((t, w), BF16) + 2 * _nbytes((a, w), F32)
    temps = 2 * _nbytes((a, w), F32) + 16 * _nbytes((t, TOEPLITZ_W), F32)
    return pl.pallas_call(
        _band_sample_body,
        grid=(b,),
        in_specs=[_resident(rb_rows.shape), new, new, new, cache, cache],
        out_specs=new,
        out_shape=jax.ShapeDtypeStruct(q.shape, BF16),
        compiler_params=pltpu.CompilerParams(
            dimension_semantics=("arbitrary",),
            vmem_limit_bytes=_vmem_limit(blocks, _nbytes(rb_rows.shape, F32), temps)),
        name="band_sample",
    )(rb_rows, q, k_new, v_new, cache_k, cache_v)


def _mla_body(qn_ref, qr_ref, kn_ref, kr_ref, v_ref, o_ref, sa_ref, sb_ref, *, n_tiles):
    tile = MLA_TILE
    r = lax.broadcasted_iota(jnp.int32, (tile, tile), 0)
    c = lax.broadcasted_iota(jnp.int32, (tile, tile), 1)
    chunk_bits = CHUNK.bit_length() - 1
    diag_visible = lax.shift_right_logical(c, chunk_bits) <= lax.shift_right_logical(r, chunk_bits)
    ones = jnp.ones((MLA_WIDE, LANES), BF16)
    buffers = (sa_ref, sb_ref)

    def row_max(s):
        return jnp.max(s, axis=-1, keepdims=True)

    for i in range(n_tiles):
        r0 = i * tile
        q = jnp.concatenate([qn_ref[r0:r0 + tile, :], qr_ref[r0:r0 + tile, :]], axis=1)
        blocks = [(c0, min(MLA_WIDE, r0 - c0)) for c0 in range(0, r0, MLA_WIDE)] + [(r0, tile)]

        def issue(k):
            c0, w = blocks[k]
            keys = jnp.concatenate([kn_ref[c0:c0 + w, :], kr_ref[c0:c0 + w, :]], axis=1)
            s = _dot_nt(q, keys)
            buffers[k % 2][:, 0:w] = s
            return row_max(s)

        m = jnp.full((tile, 1), NEG, F32)
        acc = jnp.zeros((tile, 2 * LANES), F32)
        s_max = issue(0)
        for k, (c0, w) in enumerate(blocks):
            last = k == len(blocks) - 1
            next_max = None if last else issue(k + 1)
            s = buffers[k % 2][:, 0:w]
            if last:
                s = jnp.where(diag_visible, s, NEG)
                s_max = row_max(s)
            m_new = jnp.maximum(m, s_max)
            p = jnp.exp2(s - m_new).astype(BF16)
            values = jnp.concatenate([v_ref[c0:c0 + w, :], ones[0:w]], axis=1)
            acc = jnp.exp2(m - m_new) * acc + _dot(p, values)
            m, s_max = m_new, next_max
        o_ref[r0:r0 + tile, :] = (acc[:, 0:V_DIM] / acc[:, V_DIM:]).astype(BF16)


def _mla_prompt(qn, qr, kn, kr, v):
    b, s, _ = qn.shape
    assert s % MLA_TILE == 0
    head = pl.BlockSpec((None, s, LANES), lambda i, h: (i, 0, h))
    shared = pl.BlockSpec((None, s, LANES), lambda i, h: (i, 0, 0))
    blocks = 6 * _nbytes((s, LANES), BF16)
    scratch = 2 * _nbytes((MLA_TILE, MLA_WIDE), F32)
    temps = 6 * _nbytes((MLA_TILE, MLA_WIDE), F32)
    return pl.pallas_call(
        functools.partial(_mla_body, n_tiles=s // MLA_TILE),
        grid=(b, MLA_HEADS),
        in_specs=[head, head, head, shared, head],
        out_specs=head,
        out_shape=jax.ShapeDtypeStruct(qn.shape, BF16),
        scratch_shapes=[pltpu.VMEM((MLA_TILE, MLA_WIDE), F32)] * 2,
        compiler_params=pltpu.CompilerParams(
            dimension_semantics=("arbitrary", "arbitrary"),
            vmem_limit_bytes=_vmem_limit(blocks, scratch, temps)),
        name="mla_prompt",
    )(qn, qr, kn, kr, v)


def _mla_sample_body(qn_ref, qr_ref, cn_ref, krn_ref, cc_ref, ckr_ref, wuk_ref, wuv_ref, o_ref):
    t_new = qn_ref.shape[0]
    q_lat = jnp.concatenate(
        [_dot(qn_ref[:, h * NOPE_DIM:(h + 1) * NOPE_DIM], wuk_ref[h]).astype(BF16) for h in range(MLA_HEADS)], axis=0)
    q_rot = jnp.concatenate(
        [qr_ref[:, (h + 1) * LANES - ROPE_DIM:(h + 1) * LANES] for h in range(MLA_HEADS)], axis=0)
    cache = cc_ref[...].astype(BF16)
    new = cn_ref[...].astype(BF16)
    s_c = _dot_nt(q_lat, cache) + _dot_nt(q_rot, ckr_ref[...].astype(BF16))
    s_n = _dot_nt(q_lat, new) + _dot_nt(q_rot, krn_ref[:, LANES - ROPE_DIM:LANES])
    m = jnp.maximum(jnp.max(s_c, axis=-1, keepdims=True), jnp.max(s_n, axis=-1, keepdims=True))
    p_c = jnp.exp2(s_c - m)
    p_n = jnp.exp2(s_n - m)
    l = jnp.sum(p_c, axis=-1, keepdims=True) + jnp.sum(p_n, axis=-1, keepdims=True)
    o_lat = ((_dot(p_c.astype(BF16), cache) + _dot(p_n.astype(BF16), new)) / l).astype(BF16)
    for h in range(MLA_HEADS):
        o_ref[:, h * V_DIM:(h + 1) * V_DIM] = _dot(o_lat[h * t_new:(h + 1) * t_new, :], wuv_ref[h]).astype(BF16)


def _mla_sample(qn, qr, ckv_new, kr_new, cache_ckv, cache_kr, layer, wuk, wuv):
    b, t, w = qn.shape
    past = cache_ckv.shape[2]

    def per_batch(shape):
        return pl.BlockSpec((None,) + shape, lambda i: (i, 0, 0))

    def cached(shape):
        return pl.BlockSpec((None, None) + shape, lambda i: (layer, i, 0, 0))

    blocks = (2 * _nbytes((t, w), BF16) + _nbytes((t, KV_LORA), F32) + _nbytes((t, LANES), BF16)
              + _nbytes((past, KV_LORA), F32) + _nbytes((past, ROPE_DIM), F32) + _nbytes((t, w), BF16))
    resident = _nbytes(wuk.shape, BF16) + _nbytes(wuv.shape, BF16)
    temps = _nbytes((past, KV_LORA), F32) + 4 * _nbytes((MLA_HEADS * t, past), F32)
    return pl.pallas_call(
        _mla_sample_body,
        grid=(b,),
        in_specs=[per_batch((t, w)), per_batch((t, w)), per_batch((t, KV_LORA)), per_batch((t, LANES)),
                  cached((past, KV_LORA)), cached((past, ROPE_DIM)),
                  _resident(wuk.shape), _resident(wuv.shape)],
        out_specs=per_batch((t, w)),
        out_shape=jax.ShapeDtypeStruct(qn.shape, BF16),
        compiler_params=pltpu.CompilerParams(
            dimension_semantics=("arbitrary",),
            vmem_limit_bytes=_vmem_limit(blocks, resident, temps)),
        name="mla_sample",
    )(qn, qr, ckv_new, kr_new, cache_ckv, cache_kr, wuk, wuv)


def _merge_body(h_ref, u_ref, oa_ref, ob_ref, wga_ref, wgb_ref, woa_ref, wob_ref, wout_ref, o_ref):
    j = pl.program_id(1)
    tn = h_ref.shape[1]
    d = o_ref.shape[1]

    @pl.when(j == 0)
    def _():
        o_ref[...] = jnp.zeros_like(o_ref)

    u = u_ref[...]
    m = (jax.nn.sigmoid(_dot(u, wga_ref[...])) * _dot(oa_ref[...], woa_ref[...])
         + jax.nn.sigmoid(_dot(u, wgb_ref[...])) * _dot(ob_ref[...], wob_ref[...])).astype(BF16)
    chunk = _col_tile(d)
    for c in range(0, d, chunk):
        o_ref[:, c:c + chunk] += _dot(m, wout_ref[:, c:c + chunk])

    for step in range(d // tn):
        @pl.when(j == step)
        def _(step=step):
            o_ref[:, step * tn:(step + 1) * tn] += h_ref[...]


def _merge(h, u, oa, ob, wga, wgb, woa, wob, wout):
    n, d = h.shape
    tm = min(MERGE_ROW_TILE, n)
    tn = MERGE_COL_TILE
    assert d % tn == 0
    blocks = (_nbytes((tm, tn), F32) + _nbytes((tm, d), F32) + _nbytes((tm, d), BF16)
              + 2 * _nbytes((tm, A_WIDTH), BF16) + 2 * _nbytes((d, tn), BF16)
              + 2 * _nbytes((A_WIDTH, tn), BF16) + _nbytes((tn, d), BF16))
    temps = 8 * _nbytes((tm, max(tn, _col_tile(d))), F32)
    return pl.pallas_call(
        _merge_body,
        grid=(n // tm, d // tn),
        in_specs=[
            pl.BlockSpec((tm, tn), lambda i, j: (i, j)),
            pl.BlockSpec((tm, d), lambda i, j: (i, 0)),
            pl.BlockSpec((tm, A_WIDTH), lambda i, j: (i, 0)),
            pl.BlockSpec((tm, MLA_WIDTH), lambda i, j: (i, 0)),
            pl.BlockSpec((d, tn), lambda i, j: (0, j)),
            pl.BlockSpec((d, tn), lambda i, j: (0, j)),
            pl.BlockSpec((A_WIDTH, tn), lambda i, j: (0, j)),
            pl.BlockSpec((MLA_WIDTH, tn), lambda i, j: (0, j)),
            pl.BlockSpec((tn, d), lambda i, j: (j, 0)),
        ],
        out_specs=pl.BlockSpec((tm, d), lambda i, j: (i, 0)),
        out_shape=jax.ShapeDtypeStruct((n, d), F32),
        compiler_params=pltpu.CompilerParams(
            dimension_semantics=("arbitrary", "arbitrary"),
            vmem_limit_bytes=_vmem_limit(blocks, 0, temps)),
        name="merge",
    )(h, u, oa, ob, wga, wgb, woa, wob, wout)


def _ple_body(h_ref, p_ref, gp_ref, wpg_ref, wp_ref, gf_ref, y_ref, *, final_norm):
    h = h_ref[...]
    gate = jax.nn.sigmoid(_dot(_rms(h, gp_ref[...]).astype(BF16), wpg_ref[...]))
    h = h + gate * _dot(p_ref[...].astype(BF16), wp_ref[...])
    y_ref[...] = _rms(h, gf_ref[...]) if final_norm else h


def _ple(h, p, layer, gp, wpg, wp, gf, *, final_norm):
    n, d = h.shape
    e = p.shape[2]
    tm = min(ROW_TILE, n)
    blocks = 2 * _nbytes((tm, d), F32) + _nbytes((tm, e), F32)
    resident = _nbytes(wpg.shape, BF16) + _nbytes(wp.shape, BF16)
    temps = 4 * _nbytes((tm, d), F32)
    return pl.pallas_call(
        functools.partial(_ple_body, final_norm=final_norm),
        grid=(n // tm,),
        in_specs=[
            pl.BlockSpec((tm, d), lambda i: (i, 0)),
            pl.BlockSpec((None, tm, e), lambda i: (layer, i, 0)),
            _resident((1, d)),
            _resident(wpg.shape),
            _resident(wp.shape),
            _resident((1, d)),
        ],
        out_specs=pl.BlockSpec((tm, d), lambda i: (i, 0)),
        out_shape=jax.ShapeDtypeStruct((n, d), F32),
        compiler_params=pltpu.CompilerParams(
            dimension_semantics=("arbitrary",),
            vmem_limit_bytes=_vmem_limit(blocks, resident, temps)),
        name="ple",
    )(h, p, gp, wpg, wp, gf)


def _rope_table(pos):
    half = ROPE_DIM // 2
    inv = ROPE_THETA ** (-jnp.arange(half, dtype=F32) / half)
    ang = pos.astype(F32)[:, None] * inv[None, :]
    cos, sin = jnp.cos(ang), jnp.sin(ang)
    return jnp.concatenate([cos, cos, -sin, sin], axis=1)


def _rope_columns(w):
    half = ROPE_DIM // 2
    x1, x2 = w[..., :half], w[..., half:]
    return jnp.concatenate([x1, x2, x2, x1], axis=-1)


def _layer_weights(i, d, g_ffn1, w_ffn1_gate, w_ffn1_up, w_ffn1_down, g_mix, w_in, g_cq, w_uq, w_uk, w_uv, g_ckv,
                   rel_bias, w_oa, w_ob, w_out, g_ffn2, w_ffn2_gate, w_ffn2_up, w_ffn2_down, g_ple, w_ple_gate, w_ple):
    bf = lambda a: a.astype(BF16)
    row = lambda a: a.reshape(1, -1).astype(F32)
    w = w_in[i]
    c0 = 3 * A_WIDTH
    c1 = c0 + Q_LORA
    c2 = c1 + KV_LORA
    c3 = c2 + ROPE_DIM
    uq = w_uq[i].reshape(Q_LORA, MLA_HEADS, NOPE_DIM + ROPE_DIM)
    rel = jnp.clip(jnp.arange(TOEPLITZ_W) - (BAND_PAST + BAND_Q - 1), -REL_CLIP, REL_CLIP) + REL_CLIP
    return dict(
        ffn1=(row(g_ffn1[i]), bf(w_ffn1_gate[i]), bf(w_ffn1_up[i]), bf(w_ffn1_down[i])),
        ffn2=(row(g_ffn2[i]), bf(w_ffn2_gate[i]), bf(w_ffn2_up[i]), bf(w_ffn2_down[i])),
        g_mix=row(g_mix[i]),
        inproj=(
            bf(w[:, 0:c0]),
            bf(w[:, c0:c1]),
            bf(jnp.concatenate([w[:, c1:c2], _rope_columns(w[:, c2:c3])], axis=1)),
            row(g_cq[i]),
            row(g_ckv[i]),
            bf(jnp.concatenate([uq[:, :, :NOPE_DIM].reshape(Q_LORA, MLA_WIDTH),
                                _rope_columns(uq[:, :, NOPE_DIM:]).reshape(Q_LORA, MLA_WIDTH)], axis=1)),
            bf(jnp.concatenate([jnp.transpose(w_uk[i], (2, 0, 1)).reshape(KV_LORA, MLA_WIDTH),
                                jnp.transpose(w_uv[i], (1, 0, 2)).reshape(KV_LORA, MLA_WIDTH)], axis=1)),
        ),
        rb_rows=rel_bias[i][:, rel].reshape(A_HEADS, 1, TOEPLITZ_W).astype(F32),
        w_uk=bf(w_uk[i]),
        w_uv=bf(w_uv[i]),
        merge=(bf(w[:, c3:c3 + d]), bf(w[:, c3 + d:c3 + 2 * d]), bf(w_oa[i]), bf(w_ob[i]), bf(w_out[i])),
        ple=(row(g_ple[i]), bf(w_ple_gate[i]), bf(w_ple[i])),
    )


def kernel(x_prompt, x_sample, cache_a_k, cache_a_v, cache_mla_ckv, cache_mla_krope, p_prompt, p_sample, g_ffn1, w_ffn1_gate, w_ffn1_up, w_ffn1_down, g_mix, w_in, g_cq, w_uq, w_uk, w_uv, g_ckv, rel_bias, w_oa, w_ob, w_out, g_ffn2, w_ffn2_gate, w_ffn2_up, w_ffn2_down, g_ple, w_ple_gate, w_ple, g_final):
    b, s, d = x_prompt.shape
    bs, t, _ = x_sample.shape
    depth = g_ffn1.shape[0]
    past = cache_mla_ckv.shape[2]
    a_len = cache_a_k.shape[2]
    a_keep = min(BAND_PAST, s)

    rot_p = _rope_table(jnp.arange(s))
    rot_s = jnp.tile(_rope_table(past + jnp.arange(t)), (bs, 1))
    g_fin = g_final.reshape(1, d).astype(F32)

    hp = x_prompt.reshape(b * s, d)
    hs = x_sample.reshape(bs * t, d)
    outs = [[] for _ in range(8)]
    for i in range(depth):
        lw = _layer_weights(i, d, g_ffn1, w_ffn1_gate, w_ffn1_up, w_ffn1_down, g_mix, w_in, g_cq, w_uq, w_uk, w_uv,
                            g_ckv, rel_bias, w_oa, w_ob, w_out, g_ffn2, w_ffn2_gate, w_ffn2_up, w_ffn2_down, g_ple,
                            w_ple_gate, w_ple)

        hp = _ffn(hp, *lw["ffn1"])
        qa, ka, va, qn, qr, kn, vm, krp, ckv, kr, ka_tail, va_tail, u = _inproj(
            hp, lw["g_mix"], rot_p, *lw["inproj"], rows_per_seq=s, tail_rows=a_keep)
        seq = lambda a: a.reshape(b, s, a.shape[-1])
        oa = _band_prompt(seq(qa), seq(ka), seq(va), lw["rb_rows"])
        ob = _mla_prompt(seq(qn), seq(qr), seq(kn), seq(krp), seq(vm))
        hp = _merge(hp, u, oa.reshape(b * s, A_WIDTH), ob.reshape(b * s, MLA_WIDTH), *lw["merge"])
        outs[0].append(ka_tail.reshape(b, a_keep, A_HEADS, A_HEAD_DIM))
        outs[1].append(va_tail.reshape(b, a_keep, A_HEADS, A_HEAD_DIM))
        outs[2].append(ckv.reshape(b, s, KV_LORA))
        outs[3].append(kr.reshape(b, s, ROPE_DIM))

        hs = _ffn(hs, *lw["ffn1"])
        qa, ka, va, qn, qr, _, _, krp, ckv, kr, ka_new, va_new, u = _inproj(
            hs, lw["g_mix"], rot_s, *lw["inproj"], rows_per_seq=t, tail_rows=t)
        new = lambda a: a.reshape(bs, t, a.shape[-1])
        oa = _band_sample(new(qa), new(ka), new(va), cache_a_k, cache_a_v, i, lw["rb_rows"])
        ob = _mla_sample(new(qn), new(qr), new(ckv), new(krp), cache_mla_ckv, cache_mla_krope, i,
                         lw["w_uk"], lw["w_uv"])
        hs = _merge(hs, u, oa.reshape(bs * t, A_WIDTH), ob.reshape(bs * t, MLA_WIDTH), *lw["merge"])
        outs[4].append(ka_new.reshape(bs, t, A_HEADS, A_HEAD_DIM))
        outs[5].append(va_new.reshape(bs, t, A_HEADS, A_HEAD_DIM))
        outs[6].append(ckv.reshape(bs, t, KV_LORA))
        outs[7].append(kr.reshape(bs, t, ROPE_DIM))

        hp = _ffn(hp, *lw["ffn2"])
        hs = _ffn(hs, *lw["ffn2"])
        last = i == depth - 1
        hp = _ple(hp, p_prompt.reshape(depth, b * s, -1), i, *lw["ple"], g_fin, final_norm=last)
        hs = _ple(hs, p_sample.reshape(depth, bs * t, -1), i, *lw["ple"], g_fin, final_norm=last)

    stacked = [o[0][None] if depth == 1 else jnp.stack(o) for o in outs]
    return (hp.reshape(b, s, d), hs.reshape(bs, t, d), *stacked)
```

```python
import functools

import jax
import jax.numpy as jnp
from jax import lax
from jax.experimental import pallas as pl
from jax.experimental.pallas import tpu as pltpu

F32 = jnp.float32
BF16 = jnp.bfloat16

CHUNK = 64
BAND_PAST = 512
A_HEADS = 8
A_HEAD_DIM = 128
A_WIDTH = A_HEADS * A_HEAD_DIM
REL_CLIP = 128
MLA_HEADS = 8
Q_LORA = 768
KV_LORA = 512
NOPE_DIM = 128
ROPE_DIM = 64
V_DIM = 128
MLA_WIDTH = MLA_HEADS * V_DIM
LOG2E = 1.4426950408889634
MLA_SCALE = (NOPE_DIM + ROPE_DIM) ** -0.5 * LOG2E
A_SCALE = A_HEAD_DIM ** -0.5 * LOG2E
ROPE_THETA = 10000.0
EPS = 1e-6
NEG = -1e30

LANES = 128
V7X_VMEM_BYTES = 64 * 1024 * 1024
VMEM_CEILING = V7X_VMEM_BYTES - 6 * 1024 * 1024

ROW_TILE = 512
MERGE_ROW_TILE = 1024
MERGE_COL_TILE = 256
FFN_ROW_TILE = 1024
INPROJ_ROW_TILE = 256
COL_TILE = 512
BAND_Q = 4 * CHUNK
BAND_K = BAND_PAST + BAND_Q
TOEPLITZ_W = BAND_K + BAND_Q
MLA_TILE = 512
MLA_WIDE = 1024

def _dot(a, b):
    return jnp.dot(a, b, preferred_element_type=F32)


def _dot_nt(a, b):
    return lax.dot_general(a, b, (((1,), (1,)), ((), ())), preferred_element_type=F32)


def _rms(x, g):
    return x * lax.rsqrt(jnp.mean(x * x, axis=-1, keepdims=True) + EPS) * g


def _vmem_limit(block_bytes, scratch_bytes, temp_bytes):
    return int(min(2 * block_bytes + scratch_bytes + temp_bytes + (4 << 20), VMEM_CEILING))


def _nbytes(shape, dtype):
    n = 1
    for s in shape:
        n *= s
    return n * jnp.dtype(dtype).itemsize


def _resident(shape):
    nd = len(shape)
    return pl.BlockSpec(shape, lambda *_: (0,) * nd, pipeline_mode=pl.Buffered(1))


def _col_tile(n):
    for t in (COL_TILE, 256, LANES):
        if n % t == 0:
            return t
    raise ValueError(f"width {n} is not a multiple of {LANES}")


def _ffn_body(x_ref, g_ref, wg_ref, wu_ref, wd_ref, o_ref, u_ref):
    @pl.when(pl.program_id(1) == 0)
    def _():
        x = x_ref[...]
        u_ref[...] = _rms(x, g_ref[...]).astype(BF16)
        o_ref[...] = x

    u = u_ref[...]
    gate = _dot(u, wg_ref[...])
    up = _dot(u, wu_ref[...])
    half_act = (gate * jax.nn.sigmoid(gate) * (0.5 * up)).astype(BF16)
    d = o_ref.shape[1]
    chunk = _col_tile(d)
    for c in range(0, d, chunk):
        o_ref[:, c:c + chunk] += _dot(half_act, wd_ref[:, c:c + chunk])


def _ffn(x, g, wg, wu, wd):
    n, d = x.shape
    n_steps, _, tf = wg.shape
    tm = min(FFN_ROW_TILE, n)
    blocks = _nbytes((tm, d), F32) * 2 + 3 * _nbytes((d, tf), BF16)
    scratch = _nbytes((tm, d), BF16)
    temps = 6 * _nbytes((tm, tf), F32)
    return pl.pallas_call(
        _ffn_body,
        grid=(n // tm, n_steps),
        in_specs=[
            pl.BlockSpec((tm, d), lambda i, j: (i, 0)),
            pl.BlockSpec((1, d), lambda i, j: (0, 0)),
            pl.BlockSpec((None, d, tf), lambda i, j: (j, 0, 0)),
            pl.BlockSpec((None, d, tf), lambda i, j: (j, 0, 0)),
            pl.BlockSpec((tf, d), lambda i, j: (j, 0)),
        ],
        out_specs=pl.BlockSpec((tm, d), lambda i, j: (i, 0)),
        out_shape=jax.ShapeDtypeStruct((n, d), F32),
        scratch_shapes=[pltpu.VMEM((tm, d), BF16)],
        compiler_params=pltpu.CompilerParams(
            dimension_semantics=("arbitrary", "arbitrary"),
            vmem_limit_bytes=_vmem_limit(blocks, scratch, temps)),
        name="ffn",
    )(x, g, wg, wu, wd)


def _inproj_body(x_ref, g_ref, rot_ref, wa_ref, wc_ref, wk_ref, gcq_ref, gckv_ref, wuq_ref, wkv_ref,
                 qa_o, ka_o, va_o, qn_o, qr_o, kn_o, vm_o, krp_o, ckv_o, kr_o, kat_o, vat_o, u_o):
    u = _rms(x_ref[...], g_ref[...]).astype(BF16)
    u_o[...] = u

    rot = rot_ref[...]
    half = LANES // 2

    cqn = _rms(_dot(u, wc_ref[...]), gcq_ref[...]).astype(BF16)
    qn_o[...] = (_dot(cqn, wuq_ref[:, 0:MLA_WIDTH]) * MLA_SCALE).astype(BF16)
    t = _dot(cqn, wuq_ref[:, MLA_WIDTH:2 * MLA_WIDTH]) * jnp.concatenate([rot] * MLA_HEADS, axis=1)
    qr_o[...] = ((t + pltpu.roll(t, half, axis=1)) * MLA_SCALE).astype(BF16)

    zc = _dot(u, wk_ref[...])
    ckvn = _rms(zc[:, 0:KV_LORA], gckv_ref[...])
    ckv_o[...] = ckvn
    t = zc[:, KV_LORA:KV_LORA + LANES] * rot
    kr = t + pltpu.roll(t, half, axis=1)
    lane = lax.broadcasted_iota(jnp.int32, kr.shape, 1)
    krp_o[...] = jnp.where(lane >= half, kr, 0.0).astype(BF16)
    kr_o[...] = kr[:, 0:ROPE_DIM]

    kv = _dot(ckvn.astype(BF16), wkv_ref[...])
    kn_o[...] = kv[:, 0:MLA_WIDTH].astype(BF16)
    vm_o[...] = kv[:, MLA_WIDTH:2 * MLA_WIDTH].astype(BF16)

    qa_o[...] = (_dot(u, wa_ref[:, 0:A_WIDTH]) * A_SCALE).astype(BF16)
    ka = _dot(u, wa_ref[:, A_WIDTH:2 * A_WIDTH])
    ka_o[...] = ka.astype(BF16)
    kat_o[...] = ka
    va = _dot(u, wa_ref[:, 2 * A_WIDTH:3 * A_WIDTH])
    va_o[...] = va.astype(BF16)
    vat_o[...] = va


def _inproj(x, g, rot, wa, wc, wk, gcq, gckv, wuq, wkv, *, rows_per_seq, tail_rows):
    n, d = x.shape
    tm = min(INPROJ_ROW_TILE, n)
    n_tiles = n // tm
    tab_tiles = rot.shape[0] // tm
    if tm <= rows_per_seq:
        tiles_per_seq = rows_per_seq // tm
        tail_tiles = tail_rows // tm
        n_seq = n // rows_per_seq

        def tail_map(i):
            return (i // tiles_per_seq) * tail_tiles + jnp.maximum(i % tiles_per_seq - (tiles_per_seq - tail_tiles), 0), 0

        tail_n = n_seq * tail_rows
    else:
        assert tail_rows == rows_per_seq

        def tail_map(i):
            return i, 0

        tail_n = n

    def row(i):
        return i, 0

    def tab(i):
        return i % tab_tiles, 0

    wide = pl.BlockSpec((tm, A_WIDTH), row)
    out_shapes = [jax.ShapeDtypeStruct((n, A_WIDTH), BF16)] * 7 + [
        jax.ShapeDtypeStruct((n, LANES), BF16),
        jax.ShapeDtypeStruct((n, KV_LORA), F32),
        jax.ShapeDtypeStruct((n, ROPE_DIM), F32),
        jax.ShapeDtypeStruct((tail_n, A_WIDTH), F32),
        jax.ShapeDtypeStruct((tail_n, A_WIDTH), F32),
        jax.ShapeDtypeStruct((n, d), BF16),
    ]
    out_specs = [wide] * 7 + [
        pl.BlockSpec((tm, LANES), row),
        pl.BlockSpec((tm, KV_LORA), row),
        pl.BlockSpec((tm, ROPE_DIM), row),
        pl.BlockSpec((tm, A_WIDTH), tail_map),
        pl.BlockSpec((tm, A_WIDTH), tail_map),
        pl.BlockSpec((tm, d), row),
    ]
    weights = [wa, wc, wk, gcq, gckv, wuq, wkv]
    blocks = (_nbytes((tm, d), F32) + _nbytes((tm, d), BF16) + 7 * _nbytes((tm, A_WIDTH), BF16)
              + 3 * _nbytes((tm, A_WIDTH), F32)
              + 4 * _nbytes((tm, LANES), F32))
    resident = sum(_nbytes(w.shape, w.dtype) for w in weights)
    temps = 6 * _nbytes((tm, 2 * A_WIDTH), F32)
    return pl.pallas_call(
        _inproj_body,
        grid=(n_tiles,),
        in_specs=[
            pl.BlockSpec((tm, d), row),
            _resident((1, d)),
            pl.BlockSpec((tm, LANES), tab),
        ] + [_resident(w.shape) for w in weights],
        out_specs=out_specs,
        out_shape=out_shapes,
        compiler_params=pltpu.CompilerParams(
            dimension_semantics=("arbitrary",),
            vmem_limit_bytes=_vmem_limit(blocks, resident, temps)),
        name="inproj",
    )(x, g, rot, *weights)


def _toeplitz(row, nrows):
    x = jnp.broadcast_to(row * LOG2E, (nrows, TOEPLITZ_W))
    left = (BAND_Q - 1) - lax.broadcasted_iota(jnp.int32, (nrows, TOEPLITZ_W), 0)
    for b in range((BAND_Q - 1).bit_length()):
        rolled = pltpu.roll(x, TOEPLITZ_W - (1 << b), axis=1)
        x = jnp.where(((left >> b) & 1) == 1, rolled, x)
    return x


def _band_body(rb_ref, q_ref, k_ref, v_ref, o_ref, t_ref, sa_ref, sb_ref, *, n_tiles):
    @pl.when(pl.program_id(1) == 0)
    def _():
        t = _toeplitz(rb_ref[...], BAND_Q)[:, 0:BAND_K]
        i = lax.broadcasted_iota(jnp.int32, (BAND_Q, BAND_K), 0)
        j = lax.broadcasted_iota(jnp.int32, (BAND_Q, BAND_K), 1)
        first = lax.shift_left(lax.shift_right_logical(i, CHUNK.bit_length() - 1), CHUNK.bit_length() - 1)
        visible = (j >= first) & (j < first + BAND_PAST + CHUNK)
        t_ref[...] = jnp.where(visible, t, NEG)

    ones = jnp.ones((BAND_K, LANES), BF16)
    buffers = (sa_ref, sb_ref)
    past_tiles = BAND_PAST // BAND_Q

    def band(t):
        return max(t - past_tiles, 0) * BAND_Q, min(t + 1, past_tiles + 1) * BAND_Q

    def issue(t):
        k0, nk = band(t)
        s = _dot_nt(q_ref[t * BAND_Q:(t + 1) * BAND_Q, :], k_ref[k0:k0 + nk, :]) + t_ref[:, BAND_K - nk:BAND_K]
        buffers[t % 2][:, 0:nk] = s
        return jnp.max(s, axis=-1, keepdims=True)

    row_max = issue(0)
    for t in range(n_tiles):
        next_max = issue(t + 1) if t + 1 < n_tiles else None
        k0, nk = band(t)
        p = jnp.exp2(buffers[t % 2][:, 0:nk] - row_max).astype(BF16)
        acc = _dot(p, jnp.concatenate([v_ref[k0:k0 + nk, :], ones[0:nk]], axis=1))
        o_ref[t * BAND_Q:(t + 1) * BAND_Q, :] = (acc[:, 0:A_HEAD_DIM] / acc[:, A_HEAD_DIM:]).astype(BF16)
        row_max = next_max


def _band_prompt(q, k, v, rb_rows):
    b, s, _ = q.shape
    assert s % BAND_Q == 0 and s >= BAND_K
    head = pl.BlockSpec((None, s, A_HEAD_DIM), lambda h, i: (i, 0, h))
    blocks = 4 * _nbytes((s, A_HEAD_DIM), BF16)
    scratch = 3 * _nbytes((BAND_Q, BAND_K), F32)
    temps = 16 * _nbytes((BAND_Q, TOEPLITZ_W), F32)
    return pl.pallas_call(
        functools.partial(_band_body, n_tiles=s // BAND_Q),
        grid=(A_HEADS, b),
        in_specs=[pl.BlockSpec((None, 1, TOEPLITZ_W), lambda h, i: (h, 0, 0)), head, head, head],
        out_specs=head,
        out_shape=jax.ShapeDtypeStruct(q.shape, BF16),
        scratch_shapes=[pltpu.VMEM((BAND_Q, BAND_K), F32)] * 3,
        compiler_params=pltpu.CompilerParams(
            dimension_semantics=("arbitrary", "arbitrary"),
            vmem_limit_bytes=_vmem_limit(blocks, scratch, temps)),
        name="band_prompt",
    )(rb_rows, q, k, v)


def _band_sample_body(rb_ref, q_ref, kn_ref, vn_ref, ck_ref, cv_ref, o_ref):
    t_new = q_ref.shape[0]
    n_cache = ck_ref.shape[0]
    for h in range(A_HEADS):
        cols = slice(h * A_HEAD_DIM, (h + 1) * A_HEAD_DIM)
        bias = _toeplitz(rb_ref[h], t_new)
        q = q_ref[:, cols]
        s_c = _dot_nt(q, ck_ref[:, h, :].astype(BF16)) + bias[:, BAND_PAST - n_cache:BAND_PAST]
        s_n = _dot_nt(q, kn_ref[:, cols]) + bias[:, BAND_PAST:BAND_PAST + t_new]
        m = jnp.maximum(jnp.max(s_c, axis=-1, keepdims=True), jnp.max(s_n, axis=-1, keepdims=True))
        p_c = jnp.exp2(s_c - m)
        p_n = jnp.exp2(s_n - m)
        l = jnp.sum(p_c, axis=-1, keepdims=True) + jnp.sum(p_n, axis=-1, keepdims=True)
        o = _dot(p_c.astype(BF16), cv_ref[:, h, :].astype(BF16)) + _dot(p_n.astype(BF16), vn_ref[:, cols])
        o_ref[:, cols] = (o / l).astype(BF16)


def _band_sample(q, k_new, v_new, cache_k, cache_v, layer, rb_rows):
    b, t, w = q.shape
    a = cache_k.shape[2]
    assert t <= BAND_Q and a <= BAND_PAST
    new = pl.BlockSpec((None, t, w), lambda i: (i, 0, 0))
    cache = pl.BlockSpec((None, None, a, A_HEADS, A_HEAD_DIM), lambda i: (layer, i, 0, 0, 0))
    blocks = 4 * _nbytes((t, w), BF16) + 2 * _nbytes((a, w), F32)
    temps = 2 * _nbytes((a, w), F32) + 16 * _nbytes((t, TOEPLITZ_W), F32)
    return pl.pallas_call(
        _band_sample_body,
        grid=(b,),
        in_specs=[_resident(rb_rows.shape), new, new, new, cache, cache],
        out_specs=new,
        out_shape=jax.ShapeDtypeStruct(q.shape, BF16),
        compiler_params=pltpu.CompilerParams(
            dimension_semantics=("arbitrary",),
            vmem_limit_bytes=_vmem_limit(blocks, _nbytes(rb_rows.shape, F32), temps)),
        name="band_sample",
    )(rb_rows, q, k_new, v_new, cache_k, cache_v)


def _mla_body(qn_ref, qr_ref, kn_ref, kr_ref, v_ref, o_ref, sa_ref, sb_ref, *, n_tiles):
    tile = MLA_TILE
    r = lax.broadcasted_iota(jnp.int32, (tile, tile), 0)
    c = lax.broadcasted_iota(jnp.int32, (tile, tile), 1)
    chunk_bits = CHUNK.bit_length() - 1
    diag_visible = lax.shift_right_logical(c, chunk_bits) <= lax.shift_right_logical(r, chunk_bits)
    ones = jnp.ones((MLA_WIDE, LANES), BF16)
    buffers = (sa_ref, sb_ref)

    def row_max(s):
        return jnp.max(s, axis=-1, keepdims=True)

    for i in range(n_tiles):
        r0 = i * tile
        q = jnp.concatenate([qn_ref[r0:r0 + tile, :], qr_ref[r0:r0 + tile, :]], axis=1)
        blocks = [(c0, min(MLA_WIDE, r0 - c0)) for c0 in range(0, r0, MLA_WIDE)] + [(r0, tile)]

        def issue(k):
            c0, w = blocks[k]
            keys = jnp.concatenate([kn_ref[c0:c0 + w, :], kr_ref[c0:c0 + w, :]], axis=1)
            s = _dot_nt(q, keys)
            buffers[k % 2][:, 0:w] = s
            return row_max(s)

        m = jnp.full((tile, 1), NEG, F32)
        acc = jnp.zeros((tile, 2 * LANES), F32)
        s_max = issue(0)
        for k, (c0, w) in enumerate(blocks):
            last = k == len(blocks) - 1
            next_max = None if last else issue(k + 1)
            s = buffers[k % 2][:, 0:w]
            if last:
                s = jnp.where(diag_visible, s, NEG)
                s_max = row_max(s)
            m_new = jnp.maximum(m, s_max)
            p = jnp.exp2(s - m_new).astype(BF16)
            values = jnp.concatenate([v_ref[c0:c0 + w, :], ones[0:w]], axis=1)
            acc = jnp.exp2(m - m_new) * acc + _dot(p, values)
            m, s_max = m_new, next_max
        o_ref[r0:r0 + tile, :] = (acc[:, 0:V_DIM] / acc[:, V_DIM:]).astype(BF16)


def _mla_prompt(qn, qr, kn, kr, v):
    b, s, _ = qn.shape
    assert s % MLA_TILE == 0
    head = pl.BlockSpec((None, s, LANES), lambda i, h: (i, 0, h))
    shared = pl.BlockSpec((None, s, LANES), lambda i, h: (i, 0, 0))
    blocks = 6 * _nbytes((s, LANES), BF16)
    scratch = 2 * _nbytes((MLA_TILE, MLA_WIDE), F32)
    temps = 6 * _nbytes((MLA_TILE, MLA_WIDE), F32)
    return pl.pallas_call(
        functools.partial(_mla_body, n_tiles=s // MLA_TILE),
        grid=(b, MLA_HEADS),
        in_specs=[head, head, head, shared, head],
        out_specs=head,
        out_shape=jax.ShapeDtypeStruct(qn.shape, BF16),
        scratch_shapes=[pltpu.VMEM((MLA_TILE, MLA_WIDE), F32)] * 2,
        compiler_params=pltpu.CompilerParams(
            dimension_semantics=("arbitrary", "arbitrary"),
            vmem_limit_bytes=_vmem_limit(blocks, scratch, temps)),
        name="mla_prompt",
    )(qn, qr, kn, kr, v)


def _mla_sample_body(qn_ref, qr_ref, cn_ref, krn_ref, cc_ref, ckr_ref, wuk_ref, wuv_ref, o_ref):
    t_new = qn_ref.shape[0]
    q_lat = jnp.concatenate(
        [_dot(qn_ref[:, h * NOPE_DIM:(h + 1) * NOPE_DIM], wuk_ref[h]).astype(BF16) for h in range(MLA_HEADS)], axis=0)
    q_rot = jnp.concatenate(
        [qr_ref[:, (h + 1) * LANES - ROPE_DIM:(h + 1) * LANES] for h in range(MLA_HEADS)], axis=0)
    cache = cc_ref[...].astype(BF16)
    new = cn_ref[...].astype(BF16)
    s_c = _dot_nt(q_lat, cache) + _dot_nt(q_rot, ckr_ref[...].astype(BF16))
    s_n = _dot_nt(q_lat, new) + _dot_nt(q_rot, krn_ref[:, LANES - ROPE_DIM:LANES])
    m = jnp.maximum(jnp.max(s_c, axis=-1, keepdims=True), jnp.max(s_n, axis=-1, keepdims=True))
    p_c = jnp.exp2(s_c - m)
    p_n = jnp.exp2(s_n - m)
    l = jnp.sum(p_c, axis=-1, keepdims=True) + jnp.sum(p_n, axis=-1, keepdims=True)
    o_lat = ((_dot(p_c.astype(BF16), cache) + _dot(p_n.astype(BF16), new)) / l).astype(BF16)
    for h in range(MLA_HEADS):
        o_ref[:, h * V_DIM:(h + 1) * V_DIM] = _dot(o_lat[h * t_new:(h + 1) * t_new, :], wuv_ref[h]).astype(BF16)


def _mla_sample(qn, qr, ckv_new, kr_new, cache_ckv, cache_kr, layer, wuk, wuv):
    b, t, w = qn.shape
    past = cache_ckv.shape[2]

    def per_batch(shape):
        return pl.BlockSpec((None,) + shape, lambda i: (i, 0, 0))

    def cached(shape):
        return pl.BlockSpec((None, None) + shape, lambda i: (layer, i, 0, 0))

    blocks = (2 * _nbytes((t, w), BF16) + _nbytes((t, KV_LORA), F32) + _nbytes((t, LANES), BF16)
              + _nbytes((past, KV_LORA), F32) + _nbytes((past, ROPE_DIM), F32) + _nbytes((t, w), BF16))
    resident = _nbytes(wuk.shape, BF16) + _nbytes(wuv.shape, BF16)
    temps = _nbytes((past, KV_LORA), F32) + 4 * _nbytes((MLA_HEADS * t, past), F32)
    return pl.pallas_call(
        _mla_sample_body,
        grid=(b,),
        in_specs=[per_batch((t, w)), per_batch((t, w)), per_batch((t, KV_LORA)), per_batch((t, LANES)),
                  cached((past, KV_LORA)), cached((past, ROPE_DIM)),
                  _resident(wuk.shape), _resident(wuv.shape)],
        out_specs=per_batch((t, w)),
        out_shape=jax.ShapeDtypeStruct(qn.shape, BF16),
        compiler_params=pltpu.CompilerParams(
            dimension_semantics=("arbitrary",),
            vmem_limit_bytes=_vmem_limit(blocks, resident, temps)),
        name="mla_sample",
    )(qn, qr, ckv_new, kr_new, cache_ckv, cache_kr, wuk, wuv)


def _merge_body(h_ref, u_ref, oa_ref, ob_ref, wga_ref, wgb_ref, woa_ref, wob_ref, wout_ref, o_ref):
    j = pl.program_id(1)
    tn = h_ref.shape[1]
    d = o_ref.shape[1]

    @pl.when(j == 0)
    def _():
        o_ref[...] = jnp.zeros_like(o_ref)

    u = u_ref[...]
    m = (jax.nn.sigmoid(_dot(u, wga_ref[...])) * _dot(oa_ref[...], woa_ref[...])
         + jax.nn.sigmoid(_dot(u, wgb_ref[...])) * _dot(ob_ref[...], wob_ref[...])).astype(BF16)
    chunk = _col_tile(d)
    for c in range(0, d, chunk):
        o_ref[:, c:c + chunk] += _dot(m, wout_ref[:, c:c + chunk])

    for step in range(d // tn):
        @pl.when(j == step)
        def _(step=step):
            o_ref[:, step * tn:(step + 1) * tn] += h_ref[...]


def _merge(h, u, oa, ob, wga, wgb, woa, wob, wout):
    n, d = h.shape
    tm = min(MERGE_ROW_TILE, n)
    n_steps, _, tn = wga.shape
    assert n_steps * tn == d
    blocks = (_nbytes((tm, tn), F32) + _nbytes((tm, d), F32) + _nbytes((tm, d), BF16)
              + 2 * _nbytes((tm, A_WIDTH), BF16) + 2 * _nbytes((d, tn), BF16)
              + 2 * _nbytes((A_WIDTH, tn), BF16) + _nbytes((tn, d), BF16))
    temps = 8 * _nbytes((tm, max(tn, _col_tile(d))), F32)
    return pl.pallas_call(
        _merge_body,
        grid=(n // tm, d // tn),
        in_specs=[
            pl.BlockSpec((tm, tn), lambda i, j: (i, j)),
            pl.BlockSpec((tm, d), lambda i, j: (i, 0)),
            pl.BlockSpec((tm, A_WIDTH), lambda i, j: (i, 0)),
            pl.BlockSpec((tm, MLA_WIDTH), lambda i, j: (i, 0)),
            pl.BlockSpec((None, d, tn), lambda i, j: (j, 0, 0)),
            pl.BlockSpec((None, d, tn), lambda i, j: (j, 0, 0)),
            pl.BlockSpec((None, A_WIDTH, tn), lambda i, j: (j, 0, 0)),
            pl.BlockSpec((None, MLA_WIDTH, tn), lambda i, j: (j, 0, 0)),
            pl.BlockSpec((tn, d), lambda i, j: (j, 0)),
        ],
        out_specs=pl.BlockSpec((tm, d), lambda i, j: (i, 0)),
        out_shape=jax.ShapeDtypeStruct((n, d), F32),
        compiler_params=pltpu.CompilerParams(
            dimension_semantics=("arbitrary", "arbitrary"),
            vmem_limit_bytes=_vmem_limit(blocks, 0, temps)),
        name="merge",
    )(h, u, oa, ob, wga, wgb, woa, wob, wout)


def _ple_body(h_ref, p_ref, gp_ref, wpg_ref, wp_ref, gf_ref, y_ref, *, final_norm):
    h = h_ref[...]
    gate = jax.nn.sigmoid(_dot(_rms(h, gp_ref[...]).astype(BF16), wpg_ref[...]))
    h = h + gate * _dot(p_ref[...].astype(BF16), wp_ref[...])
    y_ref[...] = _rms(h, gf_ref[...]) if final_norm else h


def _ple(h, p, layer, gp, wpg, wp, gf, *, final_norm):
    n, d = h.shape
    e = p.shape[2]
    tm = min(ROW_TILE, n)
    blocks = 2 * _nbytes((tm, d), F32) + _nbytes((tm, e), F32)
    resident = _nbytes(wpg.shape, BF16) + _nbytes(wp.shape, BF16)
    temps = 4 * _nbytes((tm, d), F32)
    return pl.pallas_call(
        functools.partial(_ple_body, final_norm=final_norm),
        grid=(n // tm,),
        in_specs=[
            pl.BlockSpec((tm, d), lambda i: (i, 0)),
            pl.BlockSpec((None, tm, e), lambda i: (layer, i, 0)),
            _resident((1, d)),
            _resident(wpg.shape),
            _resident(wp.shape),
            _resident((1, d)),
        ],
        out_specs=pl.BlockSpec((tm, d), lambda i: (i, 0)),
        out_shape=jax.ShapeDtypeStruct((n, d), F32),
        compiler_params=pltpu.CompilerParams(
            dimension_semantics=("arbitrary",),
            vmem_limit_bytes=_vmem_limit(blocks, resident, temps)),
        name="ple",
    )(h, p, gp, wpg, wp, gf)


def _rope_table(pos):
    half = ROPE_DIM // 2
    inv = ROPE_THETA ** (-jnp.arange(half, dtype=F32) / half)
    ang = pos.astype(F32)[:, None] * inv[None, :]
    cos, sin = jnp.cos(ang), jnp.sin(ang)
    return jnp.concatenate([cos, cos, -sin, sin], axis=1)


def _rope_columns(w):
    half = ROPE_DIM // 2
    x1, x2 = w[..., :half], w[..., half:]
    return jnp.concatenate([x1, x2, x2, x1], axis=-1)


def _layer_weights(i, d, g_ffn1, w_ffn1_gate, w_ffn1_up, w_ffn1_down, g_mix, w_in, g_cq, w_uq, w_uk, w_uv, g_ckv,
                   rel_bias, w_oa, w_ob, w_out, g_ffn2, w_ffn2_gate, w_ffn2_up, w_ffn2_down, g_ple, w_ple_gate, w_ple):
    bf = lambda a: a.astype(BF16)

    def tiles(a, t):
        return jnp.transpose(bf(a).reshape(a.shape[0], a.shape[1] // t, t), (1, 0, 2))

    tf = _col_tile(w_ffn1_gate.shape[2])
    tn = MERGE_COL_TILE
    row = lambda a: a.reshape(1, -1).astype(F32)
    w = w_in[i]
    c0 = 3 * A_WIDTH
    c1 = c0 + Q_LORA
    c2 = c1 + KV_LORA
    c3 = c2 + ROPE_DIM
    uq = w_uq[i].reshape(Q_LORA, MLA_HEADS, NOPE_DIM + ROPE_DIM)
    rel = jnp.clip(jnp.arange(TOEPLITZ_W) - (BAND_PAST + BAND_Q - 1), -REL_CLIP, REL_CLIP) + REL_CLIP
    return dict(
        ffn1=(row(g_ffn1[i]), tiles(w_ffn1_gate[i], tf), tiles(w_ffn1_up[i], tf), bf(w_ffn1_down[i])),
        ffn2=(row(g_ffn2[i]), tiles(w_ffn2_gate[i], tf), tiles(w_ffn2_up[i], tf), bf(w_ffn2_down[i])),
        g_mix=row(g_mix[i]),
        inproj=(
            bf(w[:, 0:c0]),
            bf(w[:, c0:c1]),
            bf(jnp.concatenate([w[:, c1:c2], _rope_columns(w[:, c2:c3])], axis=1)),
            row(g_cq[i]),
            row(g_ckv[i]),
            bf(jnp.concatenate([uq[:, :, :NOPE_DIM].reshape(Q_LORA, MLA_WIDTH),
                                _rope_columns(uq[:, :, NOPE_DIM:]).reshape(Q_LORA, MLA_WIDTH)], axis=1)),
            bf(jnp.concatenate([jnp.transpose(w_uk[i], (2, 0, 1)).reshape(KV_LORA, MLA_WIDTH),
                                jnp.transpose(w_uv[i], (1, 0, 2)).reshape(KV_LORA, MLA_WIDTH)], axis=1)),
        ),
        rb_rows=rel_bias[i][:, rel].reshape(A_HEADS, 1, TOEPLITZ_W).astype(F32),
        w_uk=bf(w_uk[i]),
        w_uv=bf(w_uv[i]),
        merge=(tiles(w[:, c3:c3 + d], tn), tiles(w[:, c3 + d:c3 + 2 * d], tn), tiles(w_oa[i], tn),
               tiles(w_ob[i], tn), bf(w_out[i])),
        ple=(row(g_ple[i]), bf(w_ple_gate[i]), bf(w_ple[i])),
    )


def kernel(x_prompt, x_sample, cache_a_k, cache_a_v, cache_mla_ckv, cache_mla_krope, p_prompt, p_sample, g_ffn1, w_ffn1_gate, w_ffn1_up, w_ffn1_down, g_mix, w_in, g_cq, w_uq, w_uk, w_uv, g_ckv, rel_bias, w_oa, w_ob, w_out, g_ffn2, w_ffn2_gate, w_ffn2_up, w_ffn2_down, g_ple, w_ple_gate, w_ple, g_final):
    b, s, d = x_prompt.shape
    bs, t, _ = x_sample.shape
    depth = g_ffn1.shape[0]
    past = cache_mla_ckv.shape[2]
    a_len = cache_a_k.shape[2]
    a_keep = min(BAND_PAST, s)

    rot_p = _rope_table(jnp.arange(s))
    rot_s = jnp.tile(_rope_table(past + jnp.arange(t)), (bs, 1))
    g_fin = g_final.reshape(1, d).astype(F32)

    hp = x_prompt.reshape(b * s, d)
    hs = x_sample.reshape(bs * t, d)
    outs = [[] for _ in range(8)]
    for i in range(depth):
        lw = _layer_weights(i, d, g_ffn1, w_ffn1_gate, w_ffn1_up, w_ffn1_down, g_mix, w_in, g_cq, w_uq, w_uk, w_uv,
                            g_ckv, rel_bias, w_oa, w_ob, w_out, g_ffn2, w_ffn2_gate, w_ffn2_up, w_ffn2_down, g_ple,
                            w_ple_gate, w_ple)

        hp = _ffn(hp, *lw["ffn1"])
        qa, ka, va, qn, qr, kn, vm, krp, ckv, kr, ka_tail, va_tail, u = _inproj(
            hp, lw["g_mix"], rot_p, *lw["inproj"], rows_per_seq=s, tail_rows=a_keep)
        seq = lambda a: a.reshape(b, s, a.shape[-1])
        oa = _band_prompt(seq(qa), seq(ka), seq(va), lw["rb_rows"])
        ob = _mla_prompt(seq(qn), seq(qr), seq(kn), seq(krp), seq(vm))
        hp = _merge(hp, u, oa.reshape(b * s, A_WIDTH), ob.reshape(b * s, MLA_WIDTH), *lw["merge"])
        outs[0].append(ka_tail.reshape(b, a_keep, A_HEADS, A_HEAD_DIM))
        outs[1].append(va_tail.reshape(b, a_keep, A_HEADS, A_HEAD_DIM))
        outs[2].append(ckv.reshape(b, s, KV_LORA))
        outs[3].append(kr.reshape(b, s, ROPE_DIM))

        hs = _ffn(hs, *lw["ffn1"])
        qa, ka, va, qn, qr, _, _, krp, ckv, kr, ka_new, va_new, u = _inproj(
            hs, lw["g_mix"], rot_s, *lw["inproj"], rows_per_seq=t, tail_rows=t)
        new = lambda a: a.reshape(bs, t, a.shape[-1])
        oa = _band_sample(new(qa), new(ka), new(va), cache_a_k, cache_a_v, i, lw["rb_rows"])
        ob = _mla_sample(new(qn), new(qr), new(ckv), new(krp), cache_mla_ckv, cache_mla_krope, i,
                         lw["w_uk"], lw["w_uv"])
        hs = _merge(hs, u, oa.reshape(bs * t, A_WIDTH), ob.reshape(bs * t, MLA_WIDTH), *lw["merge"])
        outs[4].append(ka_new.reshape(bs, t, A_HEADS, A_HEAD_DIM))
        outs[5].append(va_new.reshape(bs, t, A_HEADS, A_HEAD_DIM))
        outs[6].append(ckv.reshape(bs, t, KV_LORA))
        outs[7].append(kr.reshape(bs, t, ROPE_DIM))

        hp = _ffn(hp, *lw["ffn2"])
        hs = _ffn(hs, *lw["ffn2"])
        last = i == depth - 1
        hp = _ple(hp, p_prompt.reshape(depth, b * s, -1), i, *lw["ple"], g_fin, final_norm=last)
        hs = _ple(hs, p_sample.reshape(depth, bs * t, -1), i, *lw["ple"], g_fin, final_norm=last)

    stacked = [o[0][None] if depth == 1 else jnp.stack(o) for o in outs]
    return (hp.reshape(b, s, d), hs.reshape(bs, t, d), *stacked)
```

```python
import functools

import jax
import jax.numpy as jnp
from jax import lax
from jax.experimental import pallas as pl
from jax.experimental.pallas import tpu as pltpu

F32 = jnp.float32
BF16 = jnp.bfloat16

CHUNK = 64
BAND_PAST = 512
A_HEADS = 8
A_HEAD_DIM = 128
A_WIDTH = A_HEADS * A_HEAD_DIM
REL_CLIP = 128
MLA_HEADS = 8
Q_LORA = 768
KV_LORA = 512
NOPE_DIM = 128
ROPE_DIM = 64
V_DIM = 128
MLA_WIDTH = MLA_HEADS * V_DIM
LOG2E = 1.4426950408889634
MLA_SCALE = (NOPE_DIM + ROPE_DIM) ** -0.5 * LOG2E
A_SCALE = A_HEAD_DIM ** -0.5 * LOG2E
ROPE_THETA = 10000.0
EPS = 1e-6
NEG = -1e30

LANES = 128
V7X_VMEM_BYTES = 64 * 1024 * 1024
VMEM_CEILING = V7X_VMEM_BYTES - 6 * 1024 * 1024

ROW_TILE = 512
MERGE_ROW_TILE = 1024
MERGE_COL_TILE = 256
FFN_ROW_TILE = 1024
INPROJ_ROW_TILE = 256
COL_TILE = 512
BAND_Q = 4 * CHUNK
BAND_K = BAND_PAST + BAND_Q
TOEPLITZ_W = BAND_K + BAND_Q
MLA_TILE = 512
MLA_WIDE = 1024

def _dot(a, b):
    return jnp.dot(a, b, preferred_element_type=F32)


def _dot_nt(a, b):
    return lax.dot_general(a, b, (((1,), (1,)), ((), ())), preferred_element_type=F32)


def _rms(x, g):
    return x * lax.rsqrt(jnp.mean(x * x, axis=-1, keepdims=True) + EPS) * g


def _vmem_limit(block_bytes, scratch_bytes, temp_bytes):
    return int(min(2 * block_bytes + scratch_bytes + temp_bytes + (4 << 20), VMEM_CEILING))


def _nbytes(shape, dtype):
    n = 1
    for s in shape:
        n *= s
    return n * jnp.dtype(dtype).itemsize


def _resident(shape):
    nd = len(shape)
    return pl.BlockSpec(shape, lambda *_: (0,) * nd, pipeline_mode=pl.Buffered(1))


def _col_tile(n):
    for t in (COL_TILE, 256, LANES):
        if n % t == 0:
            return t
    raise ValueError(f"width {n} is not a multiple of {LANES}")


def _ffn_body(x_ref, g_ref, wg_ref, wu_ref, wd_ref, o_ref, u_ref):
    @pl.when(pl.program_id(1) == 0)
    def _():
        x = x_ref[...]
        u_ref[...] = _rms(x, g_ref[...]).astype(BF16)
        o_ref[...] = x

    u = u_ref[...]
    gate = _dot(u, wg_ref[...])
    up = _dot(u, wu_ref[...])
    half_act = (gate * jax.nn.sigmoid(gate) * (0.5 * up)).astype(BF16)
    d = o_ref.shape[1]
    chunk = _col_tile(d)
    for c in range(0, d, chunk):
        o_ref[:, c:c + chunk] += _dot(half_act, wd_ref[:, c:c + chunk])


def _ffn(x, g, wg, wu, wd):
    n, d = x.shape
    f = wg.shape[1]
    tm = min(FFN_ROW_TILE, n)
    tf = _col_tile(f)
    blocks = _nbytes((tm, d), F32) * 2 + 3 * _nbytes((d, tf), BF16)
    scratch = _nbytes((tm, d), BF16)
    temps = 6 * _nbytes((tm, tf), F32)
    return pl.pallas_call(
        _ffn_body,
        grid=(n // tm, f // tf),
        in_specs=[
            pl.BlockSpec((tm, d), lambda i, j: (i, 0)),
            pl.BlockSpec((1, d), lambda i, j: (0, 0)),
            pl.BlockSpec((d, tf), lambda i, j: (0, j)),
            pl.BlockSpec((d, tf), lambda i, j: (0, j)),
            pl.BlockSpec((tf, d), lambda i, j: (j, 0)),
        ],
        out_specs=pl.BlockSpec((tm, d), lambda i, j: (i, 0)),
        out_shape=jax.ShapeDtypeStruct((n, d), F32),
        scratch_shapes=[pltpu.VMEM((tm, d), BF16)],
        compiler_params=pltpu.CompilerParams(
            dimension_semantics=("arbitrary", "arbitrary"),
            vmem_limit_bytes=_vmem_limit(blocks, scratch, temps)),
        name="ffn",
    )(x, g, wg, wu, wd)


def _ffn_cast_body(x_ref, g_ref, wg_ref, wu_ref, wd_ref, o_ref, wg_o, wu_o, wd_o, u_ref):
    @pl.when(pl.program_id(1) == 0)
    def _():
        x = x_ref[...]
        u_ref[...] = _rms(x, g_ref[...]).astype(BF16)
        o_ref[...] = x

    wg = wg_ref[...].astype(BF16)
    wu = wu_ref[...].astype(BF16)
    wd = wd_ref[...].astype(BF16)
    wg_o[...] = wg
    wu_o[...] = wu
    wd_o[...] = wd
    u = u_ref[...]
    gate = _dot(u, wg)
    up = _dot(u, wu)
    half_act = (gate * jax.nn.sigmoid(gate) * (0.5 * up)).astype(BF16)
    o_ref[...] += _dot(half_act, wd)


def _ffn_cast(x, g, wg, wu, wd, layer):
    n, d = x.shape
    f = wg.shape[2]
    assert n <= FFN_ROW_TILE
    tf = _col_tile(f)
    blocks = (2 * _nbytes((n, d), F32) + 3 * _nbytes((d, tf), F32) + 3 * _nbytes((d, tf), BF16))
    scratch = _nbytes((n, d), BF16)
    temps = 6 * _nbytes((n, tf), F32) + 3 * _nbytes((d, tf), BF16) + _nbytes((n, d), F32)
    return pl.pallas_call(
        _ffn_cast_body,
        grid=(1, f // tf),
        in_specs=[
            pl.BlockSpec((n, d), lambda i, j: (0, 0)),
            pl.BlockSpec((1, d), lambda i, j: (0, 0)),
            pl.BlockSpec((None, d, tf), lambda i, j: (layer, 0, j)),
            pl.BlockSpec((None, d, tf), lambda i, j: (layer, 0, j)),
            pl.BlockSpec((None, tf, d), lambda i, j: (layer, j, 0)),
        ],
        out_specs=[
            pl.BlockSpec((n, d), lambda i, j: (0, 0)),
            pl.BlockSpec((d, tf), lambda i, j: (0, j)),
            pl.BlockSpec((d, tf), lambda i, j: (0, j)),
            pl.BlockSpec((tf, d), lambda i, j: (j, 0)),
        ],
        out_shape=[
            jax.ShapeDtypeStruct((n, d), F32),
            jax.ShapeDtypeStruct((d, f), BF16),
            jax.ShapeDtypeStruct((d, f), BF16),
            jax.ShapeDtypeStruct((f, d), BF16),
        ],
        scratch_shapes=[pltpu.VMEM((n, d), BF16)],
        compiler_params=pltpu.CompilerParams(
            dimension_semantics=("arbitrary", "arbitrary"),
            vmem_limit_bytes=_vmem_limit(blocks, scratch, temps)),
        name="ffn_cast",
    )(x, g, wg, wu, wd)


def _ffn_both(h_many, h_few, g, wg, wu, wd, layer):
    if h_few.shape[0] <= FFN_ROW_TILE:
        h_few, wg, wu, wd = _ffn_cast(h_few, g, wg, wu, wd, layer)
    else:
        wg, wu, wd = wg[layer].astype(BF16), wu[layer].astype(BF16), wd[layer].astype(BF16)
        h_few = _ffn(h_few, g, wg, wu, wd)
    return _ffn(h_many, g, wg, wu, wd), h_few


def _inproj_body(x_ref, g_ref, rot_ref, wa_ref, wc_ref, wk_ref, gcq_ref, gckv_ref, wuq_ref, wkv_ref,
                 qa_o, ka_o, va_o, qn_o, qr_o, kn_o, vm_o, krp_o, ckv_o, kr_o, kat_o, vat_o, u_o):
    u = _rms(x_ref[...], g_ref[...]).astype(BF16)
    u_o[...] = u

    rot = rot_ref[...]
    half = LANES // 2

    cqn = _rms(_dot(u, wc_ref[...]), gcq_ref[...]).astype(BF16)
    qn_o[...] = (_dot(cqn, wuq_ref[:, 0:MLA_WIDTH]) * MLA_SCALE).astype(BF16)
    t = _dot(cqn, wuq_ref[:, MLA_WIDTH:2 * MLA_WIDTH]) * jnp.concatenate([rot] * MLA_HEADS, axis=1)
    qr_o[...] = ((t + pltpu.roll(t, half, axis=1)) * MLA_SCALE).astype(BF16)

    zc = _dot(u, wk_ref[...])
    ckvn = _rms(zc[:, 0:KV_LORA], gckv_ref[...])
    ckv_o[...] = ckvn
    t = zc[:, KV_LORA:KV_LORA + LANES] * rot
    kr = t + pltpu.roll(t, half, axis=1)
    lane = lax.broadcasted_iota(jnp.int32, kr.shape, 1)
    krp_o[...] = jnp.where(lane >= half, kr, 0.0).astype(BF16)
    kr_o[...] = kr[:, 0:ROPE_DIM]

    kv = _dot(ckvn.astype(BF16), wkv_ref[...])
    kn_o[...] = kv[:, 0:MLA_WIDTH].astype(BF16)
    vm_o[...] = kv[:, MLA_WIDTH:2 * MLA_WIDTH].astype(BF16)

    qa_o[...] = (_dot(u, wa_ref[:, 0:A_WIDTH]) * A_SCALE).astype(BF16)
    ka = _dot(u, wa_ref[:, A_WIDTH:2 * A_WIDTH])
    ka_o[...] = ka.astype(BF16)
    kat_o[...] = ka
    va = _dot(u, wa_ref[:, 2 * A_WIDTH:3 * A_WIDTH])
    va_o[...] = va.astype(BF16)
    vat_o[...] = va


def _inproj(x, g, rot, wa, wc, wk, gcq, gckv, wuq, wkv, *, rows_per_seq, tail_rows):
    n, d = x.shape
    tm = min(INPROJ_ROW_TILE, n)
    n_tiles = n // tm
    tab_tiles = rot.shape[0] // tm
    if tm <= rows_per_seq:
        tiles_per_seq = rows_per_seq // tm
        tail_tiles = tail_rows // tm
        n_seq = n // rows_per_seq

        def tail_map(i):
            return (i // tiles_per_seq) * tail_tiles + jnp.maximum(i % tiles_per_seq - (tiles_per_seq - tail_tiles), 0), 0

        tail_n = n_seq * tail_rows
    else:
        assert tail_rows == rows_per_seq

        def tail_map(i):
            return i, 0

        tail_n = n

    def row(i):
        return i, 0

    def tab(i):
        return i % tab_tiles, 0

    wide = pl.BlockSpec((tm, A_WIDTH), row)
    out_shapes = [jax.ShapeDtypeStruct((n, A_WIDTH), BF16)] * 7 + [
        jax.ShapeDtypeStruct((n, LANES), BF16),
        jax.ShapeDtypeStruct((n, KV_LORA), F32),
        jax.ShapeDtypeStruct((n, ROPE_DIM), F32),
        jax.ShapeDtypeStruct((tail_n, A_WIDTH), F32),
        jax.ShapeDtypeStruct((tail_n, A_WIDTH), F32),
        jax.ShapeDtypeStruct((n, d), BF16),
    ]
    out_specs = [wide] * 7 + [
        pl.BlockSpec((tm, LANES), row),
        pl.BlockSpec((tm, KV_LORA), row),
        pl.BlockSpec((tm, ROPE_DIM), row),
        pl.BlockSpec((tm, A_WIDTH), tail_map),
        pl.BlockSpec((tm, A_WIDTH), tail_map),
        pl.BlockSpec((tm, d), row),
    ]
    weights = [wa, wc, wk, gcq, gckv, wuq, wkv]
    blocks = (_nbytes((tm, d), F32) + _nbytes((tm, d), BF16) + 7 * _nbytes((tm, A_WIDTH), BF16)
              + 3 * _nbytes((tm, A_WIDTH), F32)
              + 4 * _nbytes((tm, LANES), F32))
    resident = sum(_nbytes(w.shape, w.dtype) for w in weights)
    temps = 6 * _nbytes((tm, 2 * A_WIDTH), F32)
    return pl.pallas_call(
        _inproj_body,
        grid=(n_tiles,),
        in_specs=[
            pl.BlockSpec((tm, d), row),
            _resident((1, d)),
            pl.BlockSpec((tm, LANES), tab),
        ] + [_resident(w.shape) for w in weights],
        out_specs=out_specs,
        out_shape=out_shapes,
        compiler_params=pltpu.CompilerParams(
            dimension_semantics=("arbitrary",),
            vmem_limit_bytes=_vmem_limit(blocks, resident, temps)),
        name="inproj",
    )(x, g, rot, *weights)


def _toeplitz(row, nrows):
    x = jnp.broadcast_to(row * LOG2E, (nrows, TOEPLITZ_W))
    left = (BAND_Q - 1) - lax.broadcasted_iota(jnp.int32, (nrows, TOEPLITZ_W), 0)
    for b in range((BAND_Q - 1).bit_length()):
        rolled = pltpu.roll(x, TOEPLITZ_W - (1 << b), axis=1)
        x = jnp.where(((left >> b) & 1) == 1, rolled, x)
    return x


def _band_body(rb_ref, q_ref, k_ref, v_ref, o_ref, t_ref, sa_ref, sb_ref, *, n_tiles):
    @pl.when(pl.program_id(1) == 0)
    def _():
        t = _toeplitz(rb_ref[...], BAND_Q)[:, 0:BAND_K]
        i = lax.broadcasted_iota(jnp.int32, (BAND_Q, BAND_K), 0)
        j = lax.broadcasted_iota(jnp.int32, (BAND_Q, BAND_K), 1)
        first = lax.shift_left(lax.shift_right_logical(i, CHUNK.bit_length() - 1), CHUNK.bit_length() - 1)
        visible = (j >= first) & (j < first + BAND_PAST + CHUNK)
        t_ref[...] = jnp.where(visible, t, NEG)

    ones = jnp.ones((BAND_K, LANES), BF16)
    buffers = (sa_ref, sb_ref)
    past_tiles = BAND_PAST // BAND_Q

    def band(t):
        return max(t - past_tiles, 0) * BAND_Q, min(t + 1, past_tiles + 1) * BAND_Q

    def issue(t):
        k0, nk = band(t)
        s = _dot_nt(q_ref[t * BAND_Q:(t + 1) * BAND_Q, :], k_ref[k0:k0 + nk, :]) + t_ref[:, BAND_K - nk:BAND_K]
        buffers[t % 2][:, 0:nk] = s
        return jnp.max(s, axis=-1, keepdims=True)

    row_max = issue(0)
    for t in range(n_tiles):
        next_max = issue(t + 1) if t + 1 < n_tiles else None
        k0, nk = band(t)
        p = jnp.exp2(buffers[t % 2][:, 0:nk] - row_max).astype(BF16)
        acc = _dot(p, jnp.concatenate([v_ref[k0:k0 + nk, :], ones[0:nk]], axis=1))
        o_ref[t * BAND_Q:(t + 1) * BAND_Q, :] = (acc[:, 0:A_HEAD_DIM] / acc[:, A_HEAD_DIM:]).astype(BF16)
        row_max = next_max


def _band_prompt(q, k, v, rb_rows):
    b, s, _ = q.shape
    assert s % BAND_Q == 0 and s >= BAND_K
    head = pl.BlockSpec((None, s, A_HEAD_DIM), lambda h, i: (i, 0, h))
    blocks = 4 * _nbytes((s, A_HEAD_DIM), BF16)
    scratch = 3 * _nbytes((BAND_Q, BAND_K), F32)
    temps = 16 * _nbytes((BAND_Q, TOEPLITZ_W), F32)
    return pl.pallas_call(
        functools.partial(_band_body, n_tiles=s // BAND_Q),
        grid=(A_HEADS, b),
        in_specs=[pl.BlockSpec((None, 1, TOEPLITZ_W), lambda h, i: (h, 0, 0)), head, head, head],
        out_specs=head,
        out_shape=jax.ShapeDtypeStruct(q.shape, BF16),
        scratch_shapes=[pltpu.VMEM((BAND_Q, BAND_K), F32)] * 3,
        compiler_params=pltpu.CompilerParams(
            dimension_semantics=("arbitrary", "arbitrary"),
            vmem_limit_bytes=_vmem_limit(blocks, scratch, temps)),
        name="band_prompt",
    )(rb_rows, q, k, v)


def _band_sample_body(rb_ref, q_ref, kn_ref, vn_ref, ck_ref, cv_ref, o_ref):
    t_new = q_ref.shape[0]
    n_cache = ck_ref.shape[0]
    for h in range(A_HEADS):
        cols = slice(h * A_HEAD_DIM, (h + 1) * A_HEAD_DIM)
        bias = _toeplitz(rb_ref[h], t_new)
        q = q_ref[:, cols]
        s_c = _dot_nt(q, ck_ref[:, h, :].astype(BF16)) + bias[:, BAND_PAST - n_cache:BAND_PAST]
        s_n = _dot_nt(q, kn_ref[:, cols]) + bias[:, BAND_PAST:BAND_PAST + t_new]
        m = jnp.maximum(jnp.max(s_c, axis=-1, keepdims=True), jnp.max(s_n, axis=-1, keepdims=True))
        p_c = jnp.exp2(s_c - m)
        p_n = jnp.exp2(s_n - m)
        l = jnp.sum(p_c, axis=-1, keepdims=True) + jnp.sum(p_n, axis=-1, keepdims=True)
        o = _dot(p_c.astype(BF16), cv_ref[:, h, :].astype(BF16)) + _dot(p_n.astype(BF16), vn_ref[:, cols])
        o_ref[:, cols] = (o / l).astype(BF16)


def _band_sample(q, k_new, v_new, cache_k, cache_v, layer, rb_rows):
    b, t, w = q.shape
    a = cache_k.shape[2]
    assert t <= BAND_Q and a <= BAND_PAST
    new = pl.BlockSpec((None, t, w), lambda i: (i, 0, 0))
    cache = pl.BlockSpec((None, None, a, A_HEADS, A_HEAD_DIM), lambda i: (layer, i, 0, 0, 0))
    blocks = 4 * _nbytes((t, w), BF16) + 2 * _nbytes((a, w), F32)
    temps = 2 * _nbytes((a, w), F32) + 16 * _nbytes((t, TOEPLITZ_W), F32)
    return pl.pallas_call(
        _band_sample_body,
        grid=(b,),
        in_specs=[_resident(rb_rows.shape), new, new, new, cache, cache],
        out_specs=new,
        out_shape=jax.ShapeDtypeStruct(q.shape, BF16),
        compiler_params=pltpu.CompilerParams(
            dimension_semantics=("arbitrary",),
            vmem_limit_bytes=_vmem_limit(blocks, _nbytes(rb_rows.shape, F32), temps)),
        name="band_sample",
    )(rb_rows, q, k_new, v_new, cache_k, cache_v)


def _mla_body(qn_ref, qr_ref, kn_ref, kr_ref, v_ref, o_ref, sa_ref, sb_ref, *, n_tiles):
    tile = MLA_TILE
    r = lax.broadcasted_iota(jnp.int32, (tile, tile), 0)
    c = lax.broadcasted_iota(jnp.int32, (tile, tile), 1)
    chunk_bits = CHUNK.bit_length() - 1
    diag_visible = lax.shift_right_logical(c, chunk_bits) <= lax.shift_right_logical(r, chunk_bits)
    ones = jnp.ones((MLA_WIDE, LANES), BF16)
    buffers = (sa_ref, sb_ref)

    def row_max(s):
        return jnp.max(s, axis=-1, keepdims=True)

    for i in range(n_tiles):
        r0 = i * tile
        q = jnp.concatenate([qn_ref[r0:r0 + tile, :], qr_ref[r0:r0 + tile, :]], axis=1)
        blocks = [(c0, min(MLA_WIDE, r0 - c0)) for c0 in range(0, r0, MLA_WIDE)] + [(r0, tile)]

        def issue(k):
            c0, w = blocks[k]
            keys = jnp.concatenate([kn_ref[c0:c0 + w, :], kr_ref[c0:c0 + w, :]], axis=1)
            s = _dot_nt(q, keys)
            buffers[k % 2][:, 0:w] = s
            return row_max(s)

        m = jnp.full((tile, 1), NEG, F32)
        acc = jnp.zeros((tile, 2 * LANES), F32)
        s_max = issue(0)
        for k, (c0, w) in enumerate(blocks):
            last = k == len(blocks) - 1
            next_max = None if last else issue(k + 1)
            s = buffers[k % 2][:, 0:w]
            if last:
                s = jnp.where(diag_visible, s, NEG)
                s_max = row_max(s)
            m_new = jnp.maximum(m, s_max)
            p = jnp.exp2(s - m_new).astype(BF16)
            values = jnp.concatenate([v_ref[c0:c0 + w, :], ones[0:w]], axis=1)
            acc = jnp.exp2(m - m_new) * acc + _dot(p, values)
            m, s_max = m_new, next_max
        o_ref[r0:r0 + tile, :] = (acc[:, 0:V_DIM] / acc[:, V_DIM:]).astype(BF16)


def _mla_prompt(qn, qr, kn, kr, v):
    b, s, _ = qn.shape
    assert s % MLA_TILE == 0
    head = pl.BlockSpec((None, s, LANES), lambda i, h: (i, 0, h))
    shared = pl.BlockSpec((None, s, LANES), lambda i, h: (i, 0, 0))
    blocks = 6 * _nbytes((s, LANES), BF16)
    scratch = 2 * _nbytes((MLA_TILE, MLA_WIDE), F32)
    temps = 6 * _nbytes((MLA_TILE, MLA_WIDE), F32)
    return pl.pallas_call(
        functools.partial(_mla_body, n_tiles=s // MLA_TILE),
        grid=(b, MLA_HEADS),
        in_specs=[head, head, head, shared, head],
        out_specs=head,
        out_shape=jax.ShapeDtypeStruct(qn.shape, BF16),
        scratch_shapes=[pltpu.VMEM((MLA_TILE, MLA_WIDE), F32)] * 2,
        compiler_params=pltpu.CompilerParams(
            dimension_semantics=("arbitrary", "arbitrary"),
            vmem_limit_bytes=_vmem_limit(blocks, scratch, temps)),
        name="mla_prompt",
    )(qn, qr, kn, kr, v)


def _mla_sample_body(qn_ref, qr_ref, cn_ref, krn_ref, cc_ref, ckr_ref, wuk_ref, wuv_ref, o_ref):
    t_new = qn_ref.shape[0]
    q_lat = jnp.concatenate(
        [_dot(qn_ref[:, h * NOPE_DIM:(h + 1) * NOPE_DIM], wuk_ref[h]).astype(BF16) for h in range(MLA_HEADS)], axis=0)
    q_rot = jnp.concatenate(
        [qr_ref[:, (h + 1) * LANES - ROPE_DIM:(h + 1) * LANES] for h in range(MLA_HEADS)], axis=0)
    cache = cc_ref[...].astype(BF16)
    new = cn_ref[...].astype(BF16)
    s_c = _dot_nt(q_lat, cache) + _dot_nt(q_rot, ckr_ref[...].astype(BF16))
    s_n = _dot_nt(q_lat, new) + _dot_nt(q_rot, krn_ref[:, LANES - ROPE_DIM:LANES])
    m = jnp.maximum(jnp.max(s_c, axis=-1, keepdims=True), jnp.max(s_n, axis=-1, keepdims=True))
    p_c = jnp.exp2(s_c - m)
    p_n = jnp.exp2(s_n - m)
    l = jnp.sum(p_c, axis=-1, keepdims=True) + jnp.sum(p_n, axis=-1, keepdims=True)
    o_lat = ((_dot(p_c.astype(BF16), cache) + _dot(p_n.astype(BF16), new)) / l).astype(BF16)
    for h in range(MLA_HEADS):
        o_ref[:, h * V_DIM:(h + 1) * V_DIM] = _dot(o_lat[h * t_new:(h + 1) * t_new, :], wuv_ref[h]).astype(BF16)


def _mla_sample(qn, qr, ckv_new, kr_new, cache_ckv, cache_kr, layer, wuk, wuv):
    b, t, w = qn.shape
    past = cache_ckv.shape[2]

    def per_batch(shape):
        return pl.BlockSpec((None,) + shape, lambda i: (i, 0, 0))

    def cached(shape):
        return pl.BlockSpec((None, None) + shape, lambda i: (layer, i, 0, 0))

    blocks = (2 * _nbytes((t, w), BF16) + _nbytes((t, KV_LORA), F32) + _nbytes((t, LANES), BF16)
              + _nbytes((past, KV_LORA), F32) + _nbytes((past, ROPE_DIM), F32) + _nbytes((t, w), BF16))
    resident = _nbytes(wuk.shape, BF16) + _nbytes(wuv.shape, BF16)
    temps = _nbytes((past, KV_LORA), F32) + 4 * _nbytes((MLA_HEADS * t, past), F32)
    return pl.pallas_call(
        _mla_sample_body,
        grid=(b,),
        in_specs=[per_batch((t, w)), per_batch((t, w)), per_batch((t, KV_LORA)), per_batch((t, LANES)),
                  cached((past, KV_LORA)), cached((past, ROPE_DIM)),
                  _resident(wuk.shape), _resident(wuv.shape)],
        out_specs=per_batch((t, w)),
        out_shape=jax.ShapeDtypeStruct(qn.shape, BF16),
        compiler_params=pltpu.CompilerParams(
            dimension_semantics=("arbitrary",),
            vmem_limit_bytes=_vmem_limit(blocks, resident, temps)),
        name="mla_sample",
    )(qn, qr, ckv_new, kr_new, cache_ckv, cache_kr, wuk, wuv)


def _merge_body(h_ref, u_ref, oa_ref, ob_ref, wga_ref, wgb_ref, woa_ref, wob_ref, wout_ref, o_ref):
    j = pl.program_id(1)
    tn = h_ref.shape[1]
    d = o_ref.shape[1]

    @pl.when(j == 0)
    def _():
        o_ref[...] = jnp.zeros_like(o_ref)

    u = u_ref[...]
    m = (jax.nn.sigmoid(_dot(u, wga_ref[...])) * _dot(oa_ref[...], woa_ref[...])
         + jax.nn.sigmoid(_dot(u, wgb_ref[...])) * _dot(ob_ref[...], wob_ref[...])).astype(BF16)
    chunk = _col_tile(d)
    for c in range(0, d, chunk):
        o_ref[:, c:c + chunk] += _dot(m, wout_ref[:, c:c + chunk])

    for step in range(d // tn):
        @pl.when(j == step)
        def _(step=step):
            o_ref[:, step * tn:(step + 1) * tn] += h_ref[...]


def _merge(h, u, oa, ob, wga, wgb, woa, wob, wout):
    n, d = h.shape
    tm = min(MERGE_ROW_TILE, n)
    tn = MERGE_COL_TILE
    assert d % tn == 0
    blocks = (_nbytes((tm, tn), F32) + _nbytes((tm, d), F32) + _nbytes((tm, d), BF16)
              + 2 * _nbytes((tm, A_WIDTH), BF16) + 2 * _nbytes((d, tn), BF16)
              + 2 * _nbytes((A_WIDTH, tn), BF16) + _nbytes((tn, d), BF16))
    temps = 8 * _nbytes((tm, max(tn, _col_tile(d))), F32)
    return pl.pallas_call(
        _merge_body,
        grid=(n // tm, d // tn),
        in_specs=[
            pl.BlockSpec((tm, tn), lambda i, j: (i, j)),
            pl.BlockSpec((tm, d), lambda i, j: (i, 0)),
            pl.BlockSpec((tm, A_WIDTH), lambda i, j: (i, 0)),
            pl.BlockSpec((tm, MLA_WIDTH), lambda i, j: (i, 0)),
            pl.BlockSpec((d, tn), lambda i, j: (0, j)),
            pl.BlockSpec((d, tn), lambda i, j: (0, j)),
            pl.BlockSpec((A_WIDTH, tn), lambda i, j: (0, j)),
            pl.BlockSpec((MLA_WIDTH, tn), lambda i, j: (0, j)),
            pl.BlockSpec((tn, d), lambda i, j: (j, 0)),
        ],
        out_specs=pl.BlockSpec((tm, d), lambda i, j: (i, 0)),
        out_shape=jax.ShapeDtypeStruct((n, d), F32),
        compiler_params=pltpu.CompilerParams(
            dimension_semantics=("arbitrary", "arbitrary"),
            vmem_limit_bytes=_vmem_limit(blocks, 0, temps)),
        name="merge",
    )(h, u, oa, ob, wga, wgb, woa, wob, wout)


def _ple_body(h_ref, p_ref, gp_ref, wpg_ref, wp_ref, gf_ref, y_ref, *, final_norm):
    h = h_ref[...]
    gate = jax.nn.sigmoid(_dot(_rms(h, gp_ref[...]).astype(BF16), wpg_ref[...]))
    h = h + gate * _dot(p_ref[...].astype(BF16), wp_ref[...])
    y_ref[...] = _rms(h, gf_ref[...]) if final_norm else h


def _ple(h, p, layer, gp, wpg, wp, gf, *, final_norm):
    n, d = h.shape
    e = p.shape[2]
    tm = min(ROW_TILE, n)
    blocks = 2 * _nbytes((tm, d), F32) + _nbytes((tm, e), F32)
    resident = _nbytes(wpg.shape, BF16) + _nbytes(wp.shape, BF16)
    temps = 4 * _nbytes((tm, d), F32)
    return pl.pallas_call(
        functools.partial(_ple_body, final_norm=final_norm),
        grid=(n // tm,),
        in_specs=[
            pl.BlockSpec((tm, d), lambda i: (i, 0)),
            pl.BlockSpec((None, tm, e), lambda i: (layer, i, 0)),
            _resident((1, d)),
            _resident(wpg.shape),
            _resident(wp.shape),
            _resident((1, d)),
        ],
        out_specs=pl.BlockSpec((tm, d), lambda i: (i, 0)),
        out_shape=jax.ShapeDtypeStruct((n, d), F32),
        compiler_params=pltpu.CompilerParams(
            dimension_semantics=("arbitrary",),
            vmem_limit_bytes=_vmem_limit(blocks, resident, temps)),
        name="ple",
    )(h, p, gp, wpg, wp, gf)


def _rope_table(pos):
    half = ROPE_DIM // 2
    inv = ROPE_THETA ** (-jnp.arange(half, dtype=F32) / half)
    ang = pos.astype(F32)[:, None] * inv[None, :]
    cos, sin = jnp.cos(ang), jnp.sin(ang)
    return jnp.concatenate([cos, cos, -sin, sin], axis=1)


def _rope_columns(w):
    half = ROPE_DIM // 2
    x1, x2 = w[..., :half], w[..., half:]
    return jnp.concatenate([x1, x2, x2, x1], axis=-1)


def _layer_weights(i, d, g_ffn1, g_mix, w_in, g_cq, w_uq, w_uk, w_uv, g_ckv, rel_bias, w_oa, w_ob, w_out, g_ffn2,
                   g_ple, w_ple_gate, w_ple):
    bf = lambda a: a.astype(BF16)
    row = lambda a: a.reshape(1, -1).astype(F32)
    w = w_in[i]
    c0 = 3 * A_WIDTH
    c1 = c0 + Q_LORA
    c2 = c1 + KV_LORA
    c3 = c2 + ROPE_DIM
    uq = w_uq[i].reshape(Q_LORA, MLA_HEADS, NOPE_DIM + ROPE_DIM)
    rel = jnp.clip(jnp.arange(TOEPLITZ_W) - (BAND_PAST + BAND_Q - 1), -REL_CLIP, REL_CLIP) + REL_CLIP
    return dict(
        g_ffn1=row(g_ffn1[i]),
        g_ffn2=row(g_ffn2[i]),
        g_mix=row(g_mix[i]),
        inproj=(
            bf(w[:, 0:c0]),
            bf(w[:, c0:c1]),
            bf(jnp.concatenate([w[:, c1:c2], _rope_columns(w[:, c2:c3])], axis=1)),
            row(g_cq[i]),
            row(g_ckv[i]),
            bf(jnp.concatenate([uq[:, :, :NOPE_DIM].reshape(Q_LORA, MLA_WIDTH),
                                _rope_columns(uq[:, :, NOPE_DIM:]).reshape(Q_LORA, MLA_WIDTH)], axis=1)),
            bf(jnp.concatenate([jnp.transpose(w_uk[i], (2, 0, 1)).reshape(KV_LORA, MLA_WIDTH),
                                jnp.transpose(w_uv[i], (1, 0, 2)).reshape(KV_LORA, MLA_WIDTH)], axis=1)),
        ),
        rb_rows=rel_bias[i][:, rel].reshape(A_HEADS, 1, TOEPLITZ_W).astype(F32),
        w_uk=bf(w_uk[i]),
        w_uv=bf(w_uv[i]),
        merge=(bf(w[:, c3:c3 + d]), bf(w[:, c3 + d:c3 + 2 * d]), bf(w_oa[i]), bf(w_ob[i]), bf(w_out[i])),
        ple=(row(g_ple[i]), bf(w_ple_gate[i]), bf(w_ple[i])),
    )


def kernel(x_prompt, x_sample, cache_a_k, cache_a_v, cache_mla_ckv, cache_mla_krope, p_prompt, p_sample, g_ffn1, w_ffn1_gate, w_ffn1_up, w_ffn1_down, g_mix, w_in, g_cq, w_uq, w_uk, w_uv, g_ckv, rel_bias, w_oa, w_ob, w_out, g_ffn2, w_ffn2_gate, w_ffn2_up, w_ffn2_down, g_ple, w_ple_gate, w_ple, g_final):
    b, s, d = x_prompt.shape
    bs, t, _ = x_sample.shape
    depth = g_ffn1.shape[0]
    past = cache_mla_ckv.shape[2]
    a_len = cache_a_k.shape[2]
    a_keep = min(BAND_PAST, s)

    rot_p = _rope_table(jnp.arange(s))
    rot_s = jnp.tile(_rope_table(past + jnp.arange(t)), (bs, 1))
    g_fin = g_final.reshape(1, d).astype(F32)

    hp = x_prompt.reshape(b * s, d)
    hs = x_sample.reshape(bs * t, d)
    outs = [[] for _ in range(8)]
    for i in range(depth):
        lw = _layer_weights(i, d, g_ffn1, g_mix, w_in, g_cq, w_uq, w_uk, w_uv, g_ckv, rel_bias, w_oa, w_ob, w_out,
                            g_ffn2, g_ple, w_ple_gate, w_ple)

        hp, hs = _ffn_both(hp, hs, lw["g_ffn1"], w_ffn1_gate, w_ffn1_up, w_ffn1_down, i)

        qa, ka, va, qn, qr, kn, vm, krp, ckv, kr, ka_tail, va_tail, u = _inproj(
            hp, lw["g_mix"], rot_p, *lw["inproj"], rows_per_seq=s, tail_rows=a_keep)
        seq = lambda a: a.reshape(b, s, a.shape[-1])
        oa = _band_prompt(seq(qa), seq(ka), seq(va), lw["rb_rows"])
        ob = _mla_prompt(seq(qn), seq(qr), seq(kn), seq(krp), seq(vm))
        hp = _merge(hp, u, oa.reshape(b * s, A_WIDTH), ob.reshape(b * s, MLA_WIDTH), *lw["merge"])
        outs[0].append(ka_tail.reshape(b, a_keep, A_HEADS, A_HEAD_DIM))
        outs[1].append(va_tail.reshape(b, a_keep, A_HEADS, A_HEAD_DIM))
        outs[2].append(ckv.reshape(b, s, KV_LORA))
        outs[3].append(kr.reshape(b, s, ROPE_DIM))

        qa, ka, va, qn, qr, _, _, krp, ckv, kr, ka_new, va_new, u = _inproj(
            hs, lw["g_mix"], rot_s, *lw["inproj"], rows_per_seq=t, tail_rows=t)
        new = lambda a: a.reshape(bs, t, a.shape[-1])
        oa = _band_sample(new(qa), new(ka), new(va), cache_a_k, cache_a_v, i, lw["rb_rows"])
        ob = _mla_sample(new(qn), new(qr), new(ckv), new(krp), cache_mla_ckv, cache_mla_krope, i,
                         lw["w_uk"], lw["w_uv"])
        hs = _merge(hs, u, oa.reshape(bs * t, A_WIDTH), ob.reshape(bs * t, MLA_WIDTH), *lw["merge"])
        outs[4].append(ka_new.reshape(bs, t, A_HEADS, A_HEAD_DIM))
        outs[5].append(va_new.reshape(bs, t, A_HEADS, A_HEAD_DIM))
        outs[6].append(ckv.reshape(bs, t, KV_LORA))
        outs[7].append(kr.reshape(bs, t, ROPE_DIM))

        hp, hs = _ffn_both(hp, hs, lw["g_ffn2"], w_ffn2_gate, w_ffn2_up, w_ffn2_down, i)
        last = i == depth - 1
        hp = _ple(hp, p_prompt.reshape(depth, b * s, -1), i, *lw["ple"], g_fin, final_norm=last)
        hs = _ple(hs, p_sample.reshape(depth, bs * t, -1), i, *lw["ple"], g_fin, final_norm=last)

    stacked = [o[0][None] if depth == 1 else jnp.stack(o) for o in outs]
    return (hp.reshape(b, s, d), hs.reshape(bs, t, d), *stacked)
```

```python
import functools

import jax
import jax.numpy as jnp
from jax import lax
from jax.experimental import pallas as pl
from jax.experimental.pallas import tpu as pltpu

F32 = jnp.float32
BF16 = jnp.bfloat16

CHUNK = 64
BAND_PAST = 512
A_HEADS = 8
A_HEAD_DIM = 128
A_WIDTH = A_HEADS * A_HEAD_DIM
REL_CLIP = 128
MLA_HEADS = 8
Q_LORA = 768
KV_LORA = 512
NOPE_DIM = 128
ROPE_DIM = 64
V_DIM = 128
MLA_WIDTH = MLA_HEADS * V_DIM
LOG2E = 1.4426950408889634
MLA_SCALE = (NOPE_DIM + ROPE_DIM) ** -0.5 * LOG2E
A_SCALE = A_HEAD_DIM ** -0.5 * LOG2E
ROPE_THETA = 10000.0
EPS = 1e-6
NEG = -1e30

LANES = 128
V7X_VMEM_BYTES = 64 * 1024 * 1024
VMEM_CEILING = V7X_VMEM_BYTES - 6 * 1024 * 1024

ROW_TILE = 512
MERGE_ROW_TILE = 1024
MERGE_COL_TILE = 256
FFN_ROW_TILE = 1024
INPROJ_ROW_TILE = 256
COL_TILE = 512
BAND_Q = 4 * CHUNK
BAND_K = BAND_PAST + BAND_Q
TOEPLITZ_W = BAND_K + BAND_Q
MLA_TILE = 512
MLA_WIDE = 1024

def _dot(a, b):
    return jnp.dot(a, b, preferred_element_type=F32)


def _dot_nt(a, b):
    return lax.dot_general(a, b, (((1,), (1,)), ((), ())), preferred_element_type=F32)


def _rms(x, g):
    return x * lax.rsqrt(jnp.mean(x * x, axis=-1, keepdims=True) + EPS) * g


def _vmem_limit(block_bytes, scratch_bytes, temp_bytes):
    return int(min(2 * block_bytes + scratch_bytes + temp_bytes + (4 << 20), VMEM_CEILING))


def _nbytes(shape, dtype):
    n = 1
    for s in shape:
        n *= s
    return n * jnp.dtype(dtype).itemsize


def _resident(shape):
    nd = len(shape)
    return pl.BlockSpec(shape, lambda *_: (0,) * nd, pipeline_mode=pl.Buffered(1))


def _col_tile(n):
    for t in (COL_TILE, 256, LANES):
        if n % t == 0:
            return t
    raise ValueError(f"width {n} is not a multiple of {LANES}")


def _ffn_body(x_ref, g_ref, wg_ref, wu_ref, wd_ref, o_ref, u_ref):
    @pl.when(pl.program_id(1) == 0)
    def _():
        x = x_ref[...]
        u_ref[...] = _rms(x, g_ref[...]).astype(BF16)
        o_ref[...] = x

    u = u_ref[...]
    gate = _dot(u, wg_ref[...])
    up = _dot(u, wu_ref[...])
    half_act = (gate * jax.nn.sigmoid(gate) * (0.5 * up)).astype(BF16)
    d = o_ref.shape[1]
    chunk = _col_tile(d)
    for c in range(0, d, chunk):
        o_ref[:, c:c + chunk] += _dot(half_act, wd_ref[:, c:c + chunk])


def _ffn(x, g, wg, wu, wd):
    n, d = x.shape
    f = wg.shape[1]
    tm = min(FFN_ROW_TILE, n)
    tf = _col_tile(f)
    blocks = _nbytes((tm, d), F32) * 2 + 3 * _nbytes((d, tf), BF16)
    scratch = _nbytes((tm, d), BF16)
    temps = 6 * _nbytes((tm, tf), F32)
    return pl.pallas_call(
        _ffn_body,
        grid=(n // tm, f // tf),
        in_specs=[
            pl.BlockSpec((tm, d), lambda i, j: (i, 0)),
            pl.BlockSpec((1, d), lambda i, j: (0, 0)),
            pl.BlockSpec((d, tf), lambda i, j: (0, j)),
            pl.BlockSpec((d, tf), lambda i, j: (0, j)),
            pl.BlockSpec((tf, d), lambda i, j: (j, 0)),
        ],
        out_specs=pl.BlockSpec((tm, d), lambda i, j: (i, 0)),
        out_shape=jax.ShapeDtypeStruct((n, d), F32),
        scratch_shapes=[pltpu.VMEM((tm, d), BF16)],
        compiler_params=pltpu.CompilerParams(
            dimension_semantics=("arbitrary", "arbitrary"),
            vmem_limit_bytes=_vmem_limit(blocks, scratch, temps)),
        name="ffn",
    )(x, g, wg, wu, wd)


def _ffn_cast_body(x_ref, g_ref, wg_ref, wu_ref, wd_ref, o_ref, wg_o, wu_o, wd_o, u_ref):
    @pl.when(pl.program_id(1) == 0)
    def _():
        x = x_ref[...]
        u_ref[...] = _rms(x, g_ref[...]).astype(BF16)
        o_ref[...] = x

    wg = wg_ref[...].astype(BF16)
    wu = wu_ref[...].astype(BF16)
    wd = wd_ref[...].astype(BF16)
    wg_o[...] = wg
    wu_o[...] = wu
    wd_o[...] = wd
    u = u_ref[...]
    gate = _dot(u, wg)
    up = _dot(u, wu)
    half_act = (gate * jax.nn.sigmoid(gate) * (0.5 * up)).astype(BF16)
    o_ref[...] += _dot(half_act, wd)


def _ffn_cast(x, g, wg, wu, wd, layer):
    n, d = x.shape
    f = wg.shape[2]
    assert n <= FFN_ROW_TILE
    tf = _col_tile(f)
    blocks = (2 * _nbytes((n, d), F32) + 3 * _nbytes((d, tf), F32) + 3 * _nbytes((d, tf), BF16))
    scratch = _nbytes((n, d), BF16)
    temps = 6 * _nbytes((n, tf), F32) + 3 * _nbytes((d, tf), BF16) + _nbytes((n, d), F32)
    return pl.pallas_call(
        _ffn_cast_body,
        grid=(1, f // tf),
        in_specs=[
            pl.BlockSpec((n, d), lambda i, j: (0, 0)),
            pl.BlockSpec((1, d), lambda i, j: (0, 0)),
            pl.BlockSpec((None, d, tf), lambda i, j: (layer, 0, j)),
            pl.BlockSpec((None, d, tf), lambda i, j: (layer, 0, j)),
            pl.BlockSpec((None, tf, d), lambda i, j: (layer, j, 0)),
        ],
        out_specs=[
            pl.BlockSpec((n, d), lambda i, j: (0, 0)),
            pl.BlockSpec((d, tf), lambda i, j: (0, j)),
            pl.BlockSpec((d, tf), lambda i, j: (0, j)),
            pl.BlockSpec((tf, d), lambda i, j: (j, 0)),
        ],
        out_shape=[
            jax.ShapeDtypeStruct((n, d), F32),
            jax.ShapeDtypeStruct((d, f), BF16),
            jax.ShapeDtypeStruct((d, f), BF16),
            jax.ShapeDtypeStruct((f, d), BF16),
        ],
        scratch_shapes=[pltpu.VMEM((n, d), BF16)],
        compiler_params=pltpu.CompilerParams(
            dimension_semantics=("arbitrary", "arbitrary"),
            vmem_limit_bytes=_vmem_limit(blocks, scratch, temps)),
        name="ffn_cast",
    )(x, g, wg, wu, wd)


def _ffn_both(h_many, h_few, g, wg, wu, wd, layer):
    if h_few.shape[0] <= FFN_ROW_TILE:
        h_few, wg, wu, wd = _ffn_cast(h_few, g, wg, wu, wd, layer)
    else:
        wg, wu, wd = wg[layer].astype(BF16), wu[layer].astype(BF16), wd[layer].astype(BF16)
        h_few = _ffn(h_few, g, wg, wu, wd)
    return _ffn(h_many, g, wg, wu, wd), h_few


def _inproj_body(x_ref, g_ref, rot_ref, wa_ref, wc_ref, wk_ref, gcq_ref, gckv_ref, wuq_ref, wkv_ref,
                 qa_o, ka_o, va_o, qn_o, qr_o, kn_o, vm_o, krp_o, ckv_o, kr_o, kat_o, vat_o, u_o):
    u = _rms(x_ref[...], g_ref[...]).astype(BF16)
    u_o[...] = u

    rot = rot_ref[...]
    half = LANES // 2

    cqn = _rms(_dot(u, wc_ref[...]), gcq_ref[...]).astype(BF16)
    qn_o[...] = (_dot(cqn, wuq_ref[:, 0:MLA_WIDTH]) * MLA_SCALE).astype(BF16)
    t = _dot(cqn, wuq_ref[:, MLA_WIDTH:2 * MLA_WIDTH]) * jnp.concatenate([rot] * MLA_HEADS, axis=1)
    qr_o[...] = ((t + pltpu.roll(t, half, axis=1)) * MLA_SCALE).astype(BF16)

    zc = _dot(u, wk_ref[...])
    ckvn = _rms(zc[:, 0:KV_LORA], gckv_ref[...])
    ckv_o[...] = ckvn
    t = zc[:, KV_LORA:KV_LORA + LANES] * rot
    kr = t + pltpu.roll(t, half, axis=1)
    lane = lax.broadcasted_iota(jnp.int32, kr.shape, 1)
    krp_o[...] = jnp.where(lane >= half, kr, 0.0).astype(BF16)
    kr_o[...] = kr[:, 0:ROPE_DIM]

    kv = _dot(ckvn.astype(BF16), wkv_ref[...])
    kn_o[...] = kv[:, 0:MLA_WIDTH].astype(BF16)
    vm_o[...] = kv[:, MLA_WIDTH:2 * MLA_WIDTH].astype(BF16)

    qa_o[...] = (_dot(u, wa_ref[:, 0:A_WIDTH]) * A_SCALE).astype(BF16)
    ka = _dot(u, wa_ref[:, A_WIDTH:2 * A_WIDTH])
    ka_o[...] = ka.astype(BF16)
    kat_o[...] = ka
    va = _dot(u, wa_ref[:, 2 * A_WIDTH:3 * A_WIDTH])
    va_o[...] = va.astype(BF16)
    vat_o[...] = va


def _inproj(x, g, rot, wa, wc, wk, gcq, gckv, wuq, wkv, *, rows_per_seq, tail_rows):
    n, d = x.shape
    tm = min(INPROJ_ROW_TILE, n)
    n_tiles = n // tm
    tab_tiles = rot.shape[0] // tm
    if tm <= rows_per_seq:
        tiles_per_seq = rows_per_seq // tm
        tail_tiles = tail_rows // tm
        n_seq = n // rows_per_seq

        def tail_map(i):
            return (i // tiles_per_seq) * tail_tiles + jnp.maximum(i % tiles_per_seq - (tiles_per_seq - tail_tiles), 0), 0

        tail_n = n_seq * tail_rows
    else:
        assert tail_rows == rows_per_seq

        def tail_map(i):
            return i, 0

        tail_n = n

    def row(i):
        return i, 0

    def tab(i):
        return i % tab_tiles, 0

    wide = pl.BlockSpec((tm, A_WIDTH), row)
    out_shapes = [jax.ShapeDtypeStruct((n, A_WIDTH), BF16)] * 7 + [
        jax.ShapeDtypeStruct((n, LANES), BF16),
        jax.ShapeDtypeStruct((n, KV_LORA), F32),
        jax.ShapeDtypeStruct((n, ROPE_DIM), F32),
        jax.ShapeDtypeStruct((tail_n, A_WIDTH), F32),
        jax.ShapeDtypeStruct((tail_n, A_WIDTH), F32),
        jax.ShapeDtypeStruct((n, d), BF16),
    ]
    out_specs = [wide] * 7 + [
        pl.BlockSpec((tm, LANES), row),
        pl.BlockSpec((tm, KV_LORA), row),
        pl.BlockSpec((tm, ROPE_DIM), row),
        pl.BlockSpec((tm, A_WIDTH), tail_map),
        pl.BlockSpec((tm, A_WIDTH), tail_map),
        pl.BlockSpec((tm, d), row),
    ]
    weights = [wa, wc, wk, gcq, gckv, wuq, wkv]
    blocks = (_nbytes((tm, d), F32) + _nbytes((tm, d), BF16) + 7 * _nbytes((tm, A_WIDTH), BF16)
              + 3 * _nbytes((tm, A_WIDTH), F32)
              + 4 * _nbytes((tm, LANES), F32))
    resident = sum(_nbytes(w.shape, w.dtype) for w in weights)
    temps = 6 * _nbytes((tm, 2 * A_WIDTH), F32)
    return pl.pallas_call(
        _inproj_body,
        grid=(n_tiles,),
        in_specs=[
            pl.BlockSpec((tm, d), row),
            _resident((1, d)),
            pl.BlockSpec((tm, LANES), tab),
        ] + [_resident(w.shape) for w in weights],
        out_specs=out_specs,
        out_shape=out_shapes,
        compiler_params=pltpu.CompilerParams(
            dimension_semantics=("arbitrary",),
            vmem_limit_bytes=_vmem_limit(blocks, resident, temps)),
        name="inproj",
    )(x, g, rot, *weights)


def _toeplitz(row, nrows):
    x = jnp.broadcast_to(row * LOG2E, (nrows, TOEPLITZ_W))
    left = (BAND_Q - 1) - lax.broadcasted_iota(jnp.int32, (nrows, TOEPLITZ_W), 0)
    for b in range((BAND_Q - 1).bit_length()):
        rolled = pltpu.roll(x, TOEPLITZ_W - (1 << b), axis=1)
        x = jnp.where(((left >> b) & 1) == 1, rolled, x)
    return x


def _band_body(rb_ref, q_ref, k_ref, v_ref, o_ref, t_ref, sa_ref, sb_ref, *, n_tiles):
    @pl.when(pl.program_id(1) == 0)
    def _():
        t = _toeplitz(rb_ref[...], BAND_Q)[:, 0:BAND_K]
        i = lax.broadcasted_iota(jnp.int32, (BAND_Q, BAND_K), 0)
        j = lax.broadcasted_iota(jnp.int32, (BAND_Q, BAND_K), 1)
        first = lax.shift_left(lax.shift_right_logical(i, CHUNK.bit_length() - 1), CHUNK.bit_length() - 1)
        visible = (j >= first) & (j < first + BAND_PAST + CHUNK)
        t_ref[...] = jnp.where(visible, t, NEG)

    ones = jnp.ones((BAND_K, LANES), BF16)
    buffers = (sa_ref, sb_ref)
    past_tiles = BAND_PAST // BAND_Q

    def band(t):
        return max(t - past_tiles, 0) * BAND_Q, min(t + 1, past_tiles + 1) * BAND_Q

    def issue(t):
        k0, nk = band(t)
        s = _dot_nt(q_ref[t * BAND_Q:(t + 1) * BAND_Q, :], k_ref[k0:k0 + nk, :]) + t_ref[:, BAND_K - nk:BAND_K]
        buffers[t % 2][:, 0:nk] = s
        return jnp.max(s, axis=-1, keepdims=True)

    row_max = issue(0)
    for t in range(n_tiles):
        next_max = issue(t + 1) if t + 1 < n_tiles else None
        k0, nk = band(t)
        p = jnp.exp2(buffers[t % 2][:, 0:nk] - row_max).astype(BF16)
        acc = _dot(p, jnp.concatenate([v_ref[k0:k0 + nk, :], ones[0:nk]], axis=1))
        o_ref[t * BAND_Q:(t + 1) * BAND_Q, :] = (acc[:, 0:A_HEAD_DIM] / acc[:, A_HEAD_DIM:]).astype(BF16)
        row_max = next_max


def _band_prompt(q, k, v, rb_rows):
    b, s, _ = q.shape
    assert s % BAND_Q == 0 and s >= BAND_K
    head = pl.BlockSpec((None, s, A_HEAD_DIM), lambda h, i: (i, 0, h))
    blocks = 4 * _nbytes((s, A_HEAD_DIM), BF16)
    scratch = 3 * _nbytes((BAND_Q, BAND_K), F32)
    temps = 16 * _nbytes((BAND_Q, TOEPLITZ_W), F32)
    return pl.pallas_call(
        functools.partial(_band_body, n_tiles=s // BAND_Q),
        grid=(A_HEADS, b),
        in_specs=[pl.BlockSpec((None, 1, TOEPLITZ_W), lambda h, i: (h, 0, 0)), head, head, head],
        out_specs=head,
        out_shape=jax.ShapeDtypeStruct(q.shape, BF16),
        scratch_shapes=[pltpu.VMEM((BAND_Q, BAND_K), F32)] * 3,
        compiler_params=pltpu.CompilerParams(
            dimension_semantics=("arbitrary", "arbitrary"),
            vmem_limit_bytes=_vmem_limit(blocks, scratch, temps)),
        name="band_prompt",
    )(rb_rows, q, k, v)


def _band_sample_body(rb_ref, q_ref, kn_ref, vn_ref, ck_ref, cv_ref, o_ref):
    t_new = q_ref.shape[0]
    n_cache = ck_ref.shape[0]
    for h in range(A_HEADS):
        cols = slice(h * A_HEAD_DIM, (h + 1) * A_HEAD_DIM)
        bias = _toeplitz(rb_ref[h], t_new)
        q = q_ref[:, cols]
        s_c = _dot_nt(q, ck_ref[:, h, :].astype(BF16)) + bias[:, BAND_PAST - n_cache:BAND_PAST]
        s_n = _dot_nt(q, kn_ref[:, cols]) + bias[:, BAND_PAST:BAND_PAST + t_new]
        m = jnp.maximum(jnp.max(s_c, axis=-1, keepdims=True), jnp.max(s_n, axis=-1, keepdims=True))
        p_c = jnp.exp2(s_c - m)
        p_n = jnp.exp2(s_n - m)
        l = jnp.sum(p_c, axis=-1, keepdims=True) + jnp.sum(p_n, axis=-1, keepdims=True)
        o = _dot(p_c.astype(BF16), cv_ref[:, h, :].astype(BF16)) + _dot(p_n.astype(BF16), vn_ref[:, cols])
        o_ref[:, cols] = (o / l).astype(BF16)


def _band_sample(q, k_new, v_new, cache_k, cache_v, layer, rb_rows):
    b, t, w = q.shape
    a = cache_k.shape[2]
    assert t <= BAND_Q and a <= BAND_PAST
    new = pl.BlockSpec((None, t, w), lambda i: (i, 0, 0))
    cache = pl.BlockSpec((None, None, a, A_HEADS, A_HEAD_DIM), lambda i: (layer, i, 0, 0, 0))
    blocks = 4 * _nbytes((t, w), BF16) + 2 * _nbytes((a, w), F32)
    temps = 2 * _nbytes((a, w), F32) + 16 * _nbytes((t, TOEPLITZ_W), F32)
    return pl.pallas_call(
        _band_sample_body,
        grid=(b,),
        in_specs=[_resident(rb_rows.shape), new, new, new, cache, cache],
        out_specs=new,
        out_shape=jax.ShapeDtypeStruct(q.shape, BF16),
        compiler_params=pltpu.CompilerParams(
            dimension_semantics=("arbitrary",),
            vmem_limit_bytes=_vmem_limit(blocks, _nbytes(rb_rows.shape, F32), temps)),
        name="band_sample",
    )(rb_rows, q, k_new, v_new, cache_k, cache_v)


def _mla_body(qn_ref, qr_ref, kn_ref, kr_ref, v_ref, o_ref, sa_ref, sb_ref, *, n_tiles):
    tile = MLA_TILE
    r = lax.broadcasted_iota(jnp.int32, (tile, tile), 0)
    c = lax.broadcasted_iota(jnp.int32, (tile, tile), 1)
    chunk_bits = CHUNK.bit_length() - 1
    diag_visible = lax.shift_right_logical(c, chunk_bits) <= lax.shift_right_logical(r, chunk_bits)
    ones = jnp.ones((MLA_WIDE, LANES), BF16)
    buffers = (sa_ref, sb_ref)

    def row_max(s):
        return jnp.max(s, axis=-1, keepdims=True)

    for i in range(n_tiles):
        r0 = i * tile
        q = jnp.concatenate([qn_ref[r0:r0 + tile, :], qr_ref[r0:r0 + tile, :]], axis=1)
        blocks = [(c0, min(MLA_WIDE, r0 - c0)) for c0 in range(0, r0, MLA_WIDE)] + [(r0, tile)]

        def issue(k):
            c0, w = blocks[k]
            keys = jnp.concatenate([kn_ref[c0:c0 + w, :], kr_ref[c0:c0 + w, :]], axis=1)
            s = _dot_nt(q, keys)
            buffers[k % 2][:, 0:w] = s
            return row_max(s)

        m = jnp.full((tile, 1), NEG, F32)
        acc = jnp.zeros((tile, 2 * LANES), F32)
        s_max = issue(0)
        for k, (c0, w) in enumerate(blocks):
            last = k == len(blocks) - 1
            next_max = None if last else issue(k + 1)
            s = buffers[k % 2][:, 0:w]
            if last:
                s = jnp.where(diag_visible, s, NEG)
                s_max = row_max(s)
            m_new = jnp.maximum(m, s_max)
            p = jnp.exp2(s - m_new).astype(BF16)
            values = jnp.concatenate([v_ref[c0:c0 + w, :], ones[0:w]], axis=1)
            acc = jnp.exp2(m - m_new) * acc + _dot(p, values)
            m, s_max = m_new, next_max
        o_ref[r0:r0 + tile, :] = (acc[:, 0:V_DIM] / acc[:, V_DIM:]).astype(BF16)


def _mla_prompt(qn, qr, kn, kr, v):
    b, s, _ = qn.shape
    assert s % MLA_TILE == 0
    head = pl.BlockSpec((None, s, LANES), lambda i, h: (i, 0, h))
    shared = pl.BlockSpec((None, s, LANES), lambda i, h: (i, 0, 0))
    blocks = 6 * _nbytes((s, LANES), BF16)
    scratch = 2 * _nbytes((MLA_TILE, MLA_WIDE), F32)
    temps = 6 * _nbytes((MLA_TILE, MLA_WIDE), F32)
    return pl.pallas_call(
        functools.partial(_mla_body, n_tiles=s // MLA_TILE),
        grid=(b, MLA_HEADS),
        in_specs=[head, head, head, shared, head],
        out_specs=head,
        out_shape=jax.ShapeDtypeStruct(qn.shape, BF16),
        scratch_shapes=[pltpu.VMEM((MLA_TILE, MLA_WIDE), F32)] * 2,
        compiler_params=pltpu.CompilerParams(
            dimension_semantics=("arbitrary", "arbitrary"),
            vmem_limit_bytes=_vmem_limit(blocks, scratch, temps)),
        name="mla_prompt",
    )(qn, qr, kn, kr, v)


def _mla_sample_body(qn_ref, qr_ref, cn_ref, krn_ref, cc_ref, ckr_ref, wuk_ref, wuv_ref, o_ref):
    t_new = qn_ref.shape[0]
    q_lat = jnp.concatenate(
        [_dot(qn_ref[:, h * NOPE_DIM:(h + 1) * NOPE_DIM], wuk_ref[h]).astype(BF16) for h in range(MLA_HEADS)], axis=0)
    q_rot = jnp.concatenate(
        [qr_ref[:, (h + 1) * LANES - ROPE_DIM:(h + 1) * LANES] for h in range(MLA_HEADS)], axis=0)
    cache = cc_ref[...].astype(BF16)
    new = cn_ref[...].astype(BF16)
    s_c = _dot_nt(q_lat, cache) + _dot_nt(q_rot, ckr_ref[...].astype(BF16))
    s_n = _dot_nt(q_lat, new) + _dot_nt(q_rot, krn_ref[:, LANES - ROPE_DIM:LANES])
    m = jnp.maximum(jnp.max(s_c, axis=-1, keepdims=True), jnp.max(s_n, axis=-1, keepdims=True))
    p_c = jnp.exp2(s_c - m)
    p_n = jnp.exp2(s_n - m)
    l = jnp.sum(p_c, axis=-1, keepdims=True) + jnp.sum(p_n, axis=-1, keepdims=True)
    o_lat = ((_dot(p_c.astype(BF16), cache) + _dot(p_n.astype(BF16), new)) / l).astype(BF16)
    for h in range(MLA_HEADS):
        o_ref[:, h * V_DIM:(h + 1) * V_DIM] = _dot(o_lat[h * t_new:(h + 1) * t_new, :], wuv_ref[h]).astype(BF16)


def _mla_sample(qn, qr, ckv_new, kr_new, cache_ckv, cache_kr, layer, wuk, wuv):
    b, t, w = qn.shape
    past = cache_ckv.shape[2]

    def per_batch(shape):
        return pl.BlockSpec((None,) + shape, lambda i: (i, 0, 0))

    def cached(shape):
        return pl.BlockSpec((None, None) + shape, lambda i: (layer, i, 0, 0))

    blocks = (2 * _nbytes((t, w), BF16) + _nbytes((t, KV_LORA), F32) + _nbytes((t, LANES), BF16)
              + _nbytes((past, KV_LORA), F32) + _nbytes((past, ROPE_DIM), F32) + _nbytes((t, w), BF16))
    resident = _nbytes(wuk.shape, BF16) + _nbytes(wuv.shape, BF16)
    temps = _nbytes((past, KV_LORA), F32) + 4 * _nbytes((MLA_HEADS * t, past), F32)
    return pl.pallas_call(
        _mla_sample_body,
        grid=(b,),
        in_specs=[per_batch((t, w)), per_batch((t, w)), per_batch((t, KV_LORA)), per_batch((t, LANES)),
                  cached((past, KV_LORA)), cached((past, ROPE_DIM)),
                  _resident(wuk.shape), _resident(wuv.shape)],
        out_specs=per_batch((t, w)),
        out_shape=jax.ShapeDtypeStruct(qn.shape, BF16),
        compiler_params=pltpu.CompilerParams(
            dimension_semantics=("arbitrary",),
            vmem_limit_bytes=_vmem_limit(blocks, resident, temps)),
        name="mla_sample",
    )(qn, qr, ckv_new, kr_new, cache_ckv, cache_kr, wuk, wuv)


def _merge_body(h_ref, u_ref, oa_ref, ob_ref, wga_ref, wgb_ref, woa_ref, wob_ref, wout_ref, o_ref):
    j = pl.program_id(1)
    slab = h_ref.shape[0]
    d = o_ref.shape[1]

    @pl.when(j == 0)
    def _():
        o_ref[...] = jnp.zeros_like(o_ref)

    u = u_ref[...]
    m = (jax.nn.sigmoid(_dot(u, wga_ref[...])) * _dot(oa_ref[...], woa_ref[...])
         + jax.nn.sigmoid(_dot(u, wgb_ref[...])) * _dot(ob_ref[...], wob_ref[...])).astype(BF16)
    chunk = _col_tile(d)
    for c in range(0, d, chunk):
        o_ref[:, c:c + chunk] += _dot(m, wout_ref[:, c:c + chunk])
    o_ref[pl.ds(pl.multiple_of(j * slab, slab), slab), :] += h_ref[...]


def _merge(h, u, oa, ob, wga, wgb, woa, wob, wout):
    n, d = h.shape
    tm = min(MERGE_ROW_TILE, n)
    tn = MERGE_COL_TILE
    n_steps = d // tn
    slab = tm // n_steps
    assert d % tn == 0 and tm % n_steps == 0 and slab % 8 == 0
    blocks = (_nbytes((slab, d), F32) + _nbytes((tm, d), F32) + _nbytes((tm, d), BF16)
              + 2 * _nbytes((tm, A_WIDTH), BF16) + 2 * _nbytes((d, tn), BF16)
              + 2 * _nbytes((A_WIDTH, tn), BF16) + _nbytes((tn, d), BF16))
    temps = 8 * _nbytes((tm, max(tn, _col_tile(d))), F32)
    return pl.pallas_call(
        _merge_body,
        grid=(n // tm, n_steps),
        in_specs=[
            pl.BlockSpec((slab, d), lambda i, j: (i * n_steps + j, 0)),
            pl.BlockSpec((tm, d), lambda i, j: (i, 0)),
            pl.BlockSpec((tm, A_WIDTH), lambda i, j: (i, 0)),
            pl.BlockSpec((tm, MLA_WIDTH), lambda i, j: (i, 0)),
            pl.BlockSpec((d, tn), lambda i, j: (0, j)),
            pl.BlockSpec((d, tn), lambda i, j: (0, j)),
            pl.BlockSpec((A_WIDTH, tn), lambda i, j: (0, j)),
            pl.BlockSpec((MLA_WIDTH, tn), lambda i, j: (0, j)),
            pl.BlockSpec((tn, d), lambda i, j: (j, 0)),
        ],
        out_specs=pl.BlockSpec((tm, d), lambda i, j: (i, 0)),
        out_shape=jax.ShapeDtypeStruct((n, d), F32),
        compiler_params=pltpu.CompilerParams(
            dimension_semantics=("arbitrary", "arbitrary"),
            vmem_limit_bytes=_vmem_limit(blocks, 0, temps)),
        name="merge",
    )(h, u, oa, ob, wga, wgb, woa, wob, wout)


def _ple_body(h_ref, p_ref, gp_ref, wpg_ref, wp_ref, gf_ref, y_ref, *, final_norm):
    h = h_ref[...]
    gate = jax.nn.sigmoid(_dot(_rms(h, gp_ref[...]).astype(BF16), wpg_ref[...]))
    h = h + gate * _dot(p_ref[...].astype(BF16), wp_ref[...])
    y_ref[...] = _rms(h, gf_ref[...]) if final_norm else h


def _ple(h, p, layer, gp, wpg, wp, gf, *, final_norm):
    n, d = h.shape
    e = p.shape[2]
    tm = min(ROW_TILE, n)
    blocks = 2 * _nbytes((tm, d), F32) + _nbytes((tm, e), F32)
    resident = _nbytes(wpg.shape, BF16) + _nbytes(wp.shape, BF16)
    temps = 4 * _nbytes((tm, d), F32)
    return pl.pallas_call(
        functools.partial(_ple_body, final_norm=final_norm),
        grid=(n // tm,),
        in_specs=[
            pl.BlockSpec((tm, d), lambda i: (i, 0)),
            pl.BlockSpec((None, tm, e), lambda i: (layer, i, 0)),
            _resident((1, d)),
            _resident(wpg.shape),
            _resident(wp.shape),
            _resident((1, d)),
        ],
        out_specs=pl.BlockSpec((tm, d), lambda i: (i, 0)),
        out_shape=jax.ShapeDtypeStruct((n, d), F32),
        compiler_params=pltpu.CompilerParams(
            dimension_semantics=("arbitrary",),
            vmem_limit_bytes=_vmem_limit(blocks, resident, temps)),
        name="ple",
    )(h, p, gp, wpg, wp, gf)


def _rope_table(pos):
    half = ROPE_DIM // 2
    inv = ROPE_THETA ** (-jnp.arange(half, dtype=F32) / half)
    ang = pos.astype(F32)[:, None] * inv[None, :]
    cos, sin = jnp.cos(ang), jnp.sin(ang)
    return jnp.concatenate([cos, cos, -sin, sin], axis=1)


def _rope_columns(w):
    half = ROPE_DIM // 2
    x1, x2 = w[..., :half], w[..., half:]
    return jnp.concatenate([x1, x2, x2, x1], axis=-1)


def _layer_weights(i, d, g_ffn1, g_mix, w_in, g_cq, w_uq, w_uk, w_uv, g_ckv, rel_bias, w_oa, w_ob, w_out, g_ffn2,
                   g_ple, w_ple_gate, w_ple):
    bf = lambda a: a.astype(BF16)
    row = lambda a: a.reshape(1, -1).astype(F32)
    w = w_in[i]
    c0 = 3 * A_WIDTH
    c1 = c0 + Q_LORA
    c2 = c1 + KV_LORA
    c3 = c2 + ROPE_DIM
    uq = w_uq[i].reshape(Q_LORA, MLA_HEADS, NOPE_DIM + ROPE_DIM)
    rel = jnp.clip(jnp.arange(TOEPLITZ_W) - (BAND_PAST + BAND_Q - 1), -REL_CLIP, REL_CLIP) + REL_CLIP
    return dict(
        g_ffn1=row(g_ffn1[i]),
        g_ffn2=row(g_ffn2[i]),
        g_mix=row(g_mix[i]),
        inproj=(
            bf(w[:, 0:c0]),
            bf(w[:, c0:c1]),
            bf(jnp.concatenate([w[:, c1:c2], _rope_columns(w[:, c2:c3])], axis=1)),
            row(g_cq[i]),
            row(g_ckv[i]),
            bf(jnp.concatenate([uq[:, :, :NOPE_DIM].reshape(Q_LORA, MLA_WIDTH),
                                _rope_columns(uq[:, :, NOPE_DIM:]).reshape(Q_LORA, MLA_WIDTH)], axis=1)),
            bf(jnp.concatenate([jnp.transpose(w_uk[i], (2, 0, 1)).reshape(KV_LORA, MLA_WIDTH),
                                jnp.transpose(w_uv[i], (1, 0, 2)).reshape(KV_LORA, MLA_WIDTH)], axis=1)),
        ),
        rb_rows=rel_bias[i][:, rel].reshape(A_HEADS, 1, TOEPLITZ_W).astype(F32),
        w_uk=bf(w_uk[i]),
        w_uv=bf(w_uv[i]),
        merge=(bf(w[:, c3:c3 + d]), bf(w[:, c3 + d:c3 + 2 * d]), bf(w_oa[i]), bf(w_ob[i]), bf(w_out[i])),
        ple=(row(g_ple[i]), bf(w_ple_gate[i]), bf(w_ple[i])),
    )


def kernel(x_prompt, x_sample, cache_a_k, cache_a_v, cache_mla_ckv, cache_mla_krope, p_prompt, p_sample, g_ffn1, w_ffn1_gate, w_ffn1_up, w_ffn1_down, g_mix, w_in, g_cq, w_uq, w_uk, w_uv, g_ckv, rel_bias, w_oa, w_ob, w_out, g_ffn2, w_ffn2_gate, w_ffn2_up, w_ffn2_down, g_ple, w_ple_gate, w_ple, g_final):
    b, s, d = x_prompt.shape
    bs, t, _ = x_sample.shape
    depth = g_ffn1.shape[0]
    past = cache_mla_ckv.shape[2]
    a_len = cache_a_k.shape[2]
    a_keep = min(BAND_PAST, s)

    rot_p = _rope_table(jnp.arange(s))
    rot_s = jnp.tile(_rope_table(past + jnp.arange(t)), (bs, 1))
    g_fin = g_final.reshape(1, d).astype(F32)

    hp = x_prompt.reshape(b * s, d)
    hs = x_sample.reshape(bs * t, d)
    outs = [[] for _ in range(8)]
    for i in range(depth):
        lw = _layer_weights(i, d, g_ffn1, g_mix, w_in, g_cq, w_uq, w_uk, w_uv, g_ckv, rel_bias, w_oa, w_ob, w_out,
                            g_ffn2, g_ple, w_ple_gate, w_ple)

        hp, hs = _ffn_both(hp, hs, lw["g_ffn1"], w_ffn1_gate, w_ffn1_up, w_ffn1_down, i)

        qa, ka, va, qn, qr, kn, vm, krp, ckv, kr, ka_tail, va_tail, u = _inproj(
            hp, lw["g_mix"], rot_p, *lw["inproj"], rows_per_seq=s, tail_rows=a_keep)
        seq = lambda a: a.reshape(b, s, a.shape[-1])
        oa = _band_prompt(seq(qa), seq(ka), seq(va), lw["rb_rows"])
        ob = _mla_prompt(seq(qn), seq(qr), seq(kn), seq(krp), seq(vm))
        hp = _merge(hp, u, oa.reshape(b * s, A_WIDTH), ob.reshape(b * s, MLA_WIDTH), *lw["merge"])
        outs[0].append(ka_tail.reshape(b, a_keep, A_HEADS, A_HEAD_DIM))
        outs[1].append(va_tail.reshape(b, a_keep, A_HEADS, A_HEAD_DIM))
        outs[2].append(ckv.reshape(b, s, KV_LORA))
        outs[3].append(kr.reshape(b, s, ROPE_DIM))

        qa, ka, va, qn, qr, _, _, krp, ckv, kr, ka_new, va_new, u = _inproj(
            hs, lw["g_mix"], rot_s, *lw["inproj"], rows_per_seq=t, tail_rows=t)
        new = lambda a: a.reshape(bs, t, a.shape[-1])
        oa = _band_sample(new(qa), new(ka), new(va), cache_a_k, cache_a_v, i, lw["rb_rows"])
        ob = _mla_sample(new(qn), new(qr), new(ckv), new(krp), cache_mla_ckv, cache_mla_krope, i,
                         lw["w_uk"], lw["w_uv"])
        hs = _merge(hs, u, oa.reshape(bs * t, A_WIDTH), ob.reshape(bs * t, MLA_WIDTH), *lw["merge"])
        outs[4].append(ka_new.reshape(bs, t, A_HEADS, A_HEAD_DIM))
        outs[5].append(va_new.reshape(bs, t, A_HEADS, A_HEAD_DIM))
        outs[6].append(ckv.reshape(bs, t, KV_LORA))
        outs[7].append(kr.reshape(bs, t, ROPE_DIM))

        hp, hs = _ffn_both(hp, hs, lw["g_ffn2"], w_ffn2_gate, w_ffn2_up, w_ffn2_down, i)
        last = i == depth - 1
        hp = _ple(hp, p_prompt.reshape(depth, b * s, -1), i, *lw["ple"], g_fin, final_norm=last)
        hs = _ple(hs, p_sample.reshape(depth, bs * t, -1), i, *lw["ple"], g_fin, final_norm=last)

    stacked = [o[0][None] if depth == 1 else jnp.stack(o) for o in outs]
    return (hp.reshape(b, s, d), hs.reshape(bs, t, d), *stacked)
```

```python
import functools

import jax
import jax.numpy as jnp
from jax import lax
from jax.experimental import pallas as pl
from jax.experimental.pallas import tpu as pltpu

F32 = jnp.float32
BF16 = jnp.bfloat16

CHUNK = 64
BAND_PAST = 512
A_HEADS = 8
A_HEAD_DIM = 128
A_WIDTH = A_HEADS * A_HEAD_DIM
REL_CLIP = 128
MLA_HEADS = 8
Q_LORA = 768
KV_LORA = 512
NOPE_DIM = 128
ROPE_DIM = 64
V_DIM = 128
MLA_WIDTH = MLA_HEADS * V_DIM
LOG2E = 1.4426950408889634
MLA_SCALE = (NOPE_DIM + ROPE_DIM) ** -0.5 * LOG2E
A_SCALE = A_HEAD_DIM ** -0.5 * LOG2E
ROPE_THETA = 10000.0
EPS = 1e-6
NEG = -1e30

LANES = 128
V7X_VMEM_BYTES = 64 * 1024 * 1024
VMEM_CEILING = V7X_VMEM_BYTES - 6 * 1024 * 1024

ROW_TILE = 512
MERGE_ROW_TILE = 1024
MERGE_COL_TILE = 256
FFN_ROW_TILE = 1024
INPROJ_ROW_TILE = 256
COL_TILE = 512
BAND_Q = 4 * CHUNK
BAND_K = BAND_PAST + BAND_Q
TOEPLITZ_W = BAND_K + BAND_Q
MLA_TILE = 512
MLA_WIDE = 1024

def _dot(a, b):
    return jnp.dot(a, b, preferred_element_type=F32)


def _dot_nt(a, b):
    return lax.dot_general(a, b, (((1,), (1,)), ((), ())), preferred_element_type=F32)


def _rms(x, g):
    return x * lax.rsqrt(jnp.mean(x * x, axis=-1, keepdims=True) + EPS) * g


def _vmem_limit(block_bytes, scratch_bytes, temp_bytes):
    return int(min(2 * block_bytes + scratch_bytes + temp_bytes + (4 << 20), VMEM_CEILING))


def _nbytes(shape, dtype):
    n = 1
    for s in shape:
        n *= s
    return n * jnp.dtype(dtype).itemsize


def _resident(shape):
    nd = len(shape)
    return pl.BlockSpec(shape, lambda *_: (0,) * nd, pipeline_mode=pl.Buffered(1))


def _col_tile(n):
    for t in (COL_TILE, 256, LANES):
        if n % t == 0:
            return t
    raise ValueError(f"width {n} is not a multiple of {LANES}")


def _ffn_body(x_ref, g_ref, wg_ref, wu_ref, wd_ref, o_ref, u_ref):
    @pl.when(pl.program_id(1) == 0)
    def _():
        x = x_ref[...]
        u_ref[...] = _rms(x, g_ref[...]).astype(BF16)
        o_ref[...] = x

    u = u_ref[...]
    gate = _dot(u, wg_ref[...])
    up = _dot(u, wu_ref[...])
    half_act = (gate * jax.nn.sigmoid(gate) * (0.5 * up)).astype(BF16)
    d = o_ref.shape[1]
    chunk = _col_tile(d)
    for c in range(0, d, chunk):
        o_ref[:, c:c + chunk] += _dot(half_act, wd_ref[:, c:c + chunk])


def _ffn(x, g, wg, wu, wd):
    n, d = x.shape
    f = wg.shape[1]
    tm = min(FFN_ROW_TILE, n)
    tf = _col_tile(f)
    blocks = _nbytes((tm, d), F32) * 2 + 3 * _nbytes((d, tf), BF16)
    scratch = _nbytes((tm, d), BF16)
    temps = 6 * _nbytes((tm, tf), F32)
    return pl.pallas_call(
        _ffn_body,
        grid=(n // tm, f // tf),
        in_specs=[
            pl.BlockSpec((tm, d), lambda i, j: (i, 0)),
            pl.BlockSpec((1, d), lambda i, j: (0, 0)),
            pl.BlockSpec((d, tf), lambda i, j: (0, j)),
            pl.BlockSpec((d, tf), lambda i, j: (0, j)),
            pl.BlockSpec((tf, d), lambda i, j: (j, 0)),
        ],
        out_specs=pl.BlockSpec((tm, d), lambda i, j: (i, 0)),
        out_shape=jax.ShapeDtypeStruct((n, d), F32),
        scratch_shapes=[pltpu.VMEM((tm, d), BF16)],
        compiler_params=pltpu.CompilerParams(
            dimension_semantics=("arbitrary", "arbitrary"),
            vmem_limit_bytes=_vmem_limit(blocks, scratch, temps)),
        name="ffn",
    )(x, g, wg, wu, wd)


def _ffn_cast_body(x_ref, g_ref, wg_ref, wu_ref, wd_ref, o_ref, wg_o, wu_o, wd_o, u_ref):
    @pl.when(pl.program_id(1) == 0)
    def _():
        x = x_ref[...]
        u_ref[...] = _rms(x, g_ref[...]).astype(BF16)
        o_ref[...] = x

    wg = wg_ref[...].astype(BF16)
    wu = wu_ref[...].astype(BF16)
    wd = wd_ref[...].astype(BF16)
    wg_o[...] = wg
    wu_o[...] = wu
    wd_o[...] = wd
    u = u_ref[...]
    gate = _dot(u, wg)
    up = _dot(u, wu)
    half_act = (gate * jax.nn.sigmoid(gate) * (0.5 * up)).astype(BF16)
    o_ref[...] += _dot(half_act, wd)


def _ffn_cast(x, g, wg, wu, wd, layer):
    n, d = x.shape
    f = wg.shape[2]
    assert n <= FFN_ROW_TILE
    tf = _col_tile(f)
    blocks = (2 * _nbytes((n, d), F32) + 3 * _nbytes((d, tf), F32) + 3 * _nbytes((d, tf), BF16))
    scratch = _nbytes((n, d), BF16)
    temps = 6 * _nbytes((n, tf), F32) + 3 * _nbytes((d, tf), BF16) + _nbytes((n, d), F32)
    return pl.pallas_call(
        _ffn_cast_body,
        grid=(1, f // tf),
        in_specs=[
            pl.BlockSpec((n, d), lambda i, j: (0, 0)),
            pl.BlockSpec((1, d), lambda i, j: (0, 0)),
            pl.BlockSpec((None, d, tf), lambda i, j: (layer, 0, j)),
            pl.BlockSpec((None, d, tf), lambda i, j: (layer, 0, j)),
            pl.BlockSpec((None, tf, d), lambda i, j: (layer, j, 0)),
        ],
        out_specs=[
            pl.BlockSpec((n, d), lambda i, j: (0, 0)),
            pl.BlockSpec((d, tf), lambda i, j: (0, j)),
            pl.BlockSpec((d, tf), lambda i, j: (0, j)),
            pl.BlockSpec((tf, d), lambda i, j: (j, 0)),
        ],
        out_shape=[
            jax.ShapeDtypeStruct((n, d), F32),
            jax.ShapeDtypeStruct((d, f), BF16),
            jax.ShapeDtypeStruct((d, f), BF16),
            jax.ShapeDtypeStruct((f, d), BF16),
        ],
        scratch_shapes=[pltpu.VMEM((n, d), BF16)],
        compiler_params=pltpu.CompilerParams(
            dimension_semantics=("arbitrary", "arbitrary"),
            vmem_limit_bytes=_vmem_limit(blocks, scratch, temps)),
        name="ffn_cast",
    )(x, g, wg, wu, wd)


def _ffn_both(h_many, h_few, g, wg, wu, wd, layer):
    if h_few.shape[0] <= FFN_ROW_TILE:
        h_few, wg, wu, wd = _ffn_cast(h_few, g, wg, wu, wd, layer)
    else:
        wg, wu, wd = wg[layer].astype(BF16), wu[layer].astype(BF16), wd[layer].astype(BF16)
        h_few = _ffn(h_few, g, wg, wu, wd)
    return _ffn(h_many, g, wg, wu, wd), h_few


def _inproj_body(x_ref, g_ref, rot_ref, wa_ref, wc_ref, wk_ref, gcq_ref, gckv_ref, wuq_ref, wkv_ref,
                 qa_o, ka_o, va_o, qn_o, qr_o, kn_o, vm_o, krp_o, ckv_o, kr_o, kat_o, vat_o, u_o,
                 *, tiles_per_seq, first_tail):
    u = _rms(x_ref[...], g_ref[...]).astype(BF16)
    u_o[...] = u

    rot = rot_ref[...]
    half = LANES // 2

    cqn = _rms(_dot(u, wc_ref[...]), gcq_ref[...]).astype(BF16)
    qn_o[...] = (_dot(cqn, wuq_ref[:, 0:MLA_WIDTH]) * MLA_SCALE).astype(BF16)
    t = _dot(cqn, wuq_ref[:, MLA_WIDTH:2 * MLA_WIDTH]) * jnp.concatenate([rot] * MLA_HEADS, axis=1)
    qr_o[...] = ((t + pltpu.roll(t, half, axis=1)) * MLA_SCALE).astype(BF16)

    zc = _dot(u, wk_ref[...])
    ckvn = _rms(zc[:, 0:KV_LORA], gckv_ref[...])
    ckv_o[...] = ckvn
    t = zc[:, KV_LORA:KV_LORA + LANES] * rot
    kr = t + pltpu.roll(t, half, axis=1)
    lane = lax.broadcasted_iota(jnp.int32, kr.shape, 1)
    krp_o[...] = jnp.where(lane >= half, kr, 0.0).astype(BF16)
    kr_o[...] = kr[:, 0:ROPE_DIM]

    kv = _dot(ckvn.astype(BF16), wkv_ref[...])
    kn_o[...] = kv[:, 0:MLA_WIDTH].astype(BF16)
    vm_o[...] = kv[:, MLA_WIDTH:2 * MLA_WIDTH].astype(BF16)

    qa_o[...] = (_dot(u, wa_ref[:, 0:A_WIDTH]) * A_SCALE).astype(BF16)
    ka = _dot(u, wa_ref[:, A_WIDTH:2 * A_WIDTH])
    ka_o[...] = ka.astype(BF16)
    va = _dot(u, wa_ref[:, 2 * A_WIDTH:3 * A_WIDTH])
    va_o[...] = va.astype(BF16)

    @pl.when(pl.program_id(0) % tiles_per_seq >= first_tail)
    def _():
        for h in range(A_HEADS):
            kat_o[:, h, :] = ka[:, h * A_HEAD_DIM:(h + 1) * A_HEAD_DIM]
            vat_o[:, h, :] = va[:, h * A_HEAD_DIM:(h + 1) * A_HEAD_DIM]


def _inproj(x, g, rot, wa, wc, wk, gcq, gckv, wuq, wkv, *, rows_per_seq, tail_rows):
    n, d = x.shape
    tm = min(INPROJ_ROW_TILE, n)
    n_tiles = n // tm
    tab_tiles = rot.shape[0] // tm
    if tm <= rows_per_seq:
        tiles_per_seq = rows_per_seq // tm
        tail_tiles = tail_rows // tm
        n_seq = n // rows_per_seq
        first_tail = tiles_per_seq - tail_tiles

        def tail_map(i):
            return (i // tiles_per_seq) * tail_tiles + jnp.maximum(i % tiles_per_seq - first_tail, 0), 0, 0

        tail_n = n_seq * tail_rows
    else:
        assert tail_rows == rows_per_seq
        tiles_per_seq, first_tail = 1, 0

        def tail_map(i):
            return i, 0, 0

        tail_n = n

    def row(i):
        return i, 0

    def tab(i):
        return i % tab_tiles, 0

    wide = pl.BlockSpec((tm, A_WIDTH), row)
    out_shapes = [jax.ShapeDtypeStruct((n, A_WIDTH), BF16)] * 7 + [
        jax.ShapeDtypeStruct((n, LANES), BF16),
        jax.ShapeDtypeStruct((n, KV_LORA), F32),
        jax.ShapeDtypeStruct((n, ROPE_DIM), F32),
        jax.ShapeDtypeStruct((tail_n, A_HEADS, A_HEAD_DIM), F32),
        jax.ShapeDtypeStruct((tail_n, A_HEADS, A_HEAD_DIM), F32),
        jax.ShapeDtypeStruct((n, d), BF16),
    ]
    out_specs = [wide] * 7 + [
        pl.BlockSpec((tm, LANES), row),
        pl.BlockSpec((tm, KV_LORA), row),
        pl.BlockSpec((tm, ROPE_DIM), row),
        pl.BlockSpec((tm, A_HEADS, A_HEAD_DIM), tail_map),
        pl.BlockSpec((tm, A_HEADS, A_HEAD_DIM), tail_map),
        pl.BlockSpec((tm, d), row),
    ]
    weights = [wa, wc, wk, gcq, gckv, wuq, wkv]
    blocks = (_nbytes((tm, d), F32) + _nbytes((tm, d), BF16) + 7 * _nbytes((tm, A_WIDTH), BF16)
              + 3 * _nbytes((tm, A_WIDTH), F32)
              + 4 * _nbytes((tm, LANES), F32))
    resident = sum(_nbytes(w.shape, w.dtype) for w in weights)
    temps = 6 * _nbytes((tm, 2 * A_WIDTH), F32)
    return pl.pallas_call(
        functools.partial(_inproj_body, tiles_per_seq=tiles_per_seq, first_tail=first_tail),
        grid=(n_tiles,),
        in_specs=[
            pl.BlockSpec((tm, d), row),
            _resident((1, d)),
            pl.BlockSpec((tm, LANES), tab),
        ] + [_resident(w.shape) for w in weights],
        out_specs=out_specs,
        out_shape=out_shapes,
        compiler_params=pltpu.CompilerParams(
            dimension_semantics=("arbitrary",),
            vmem_limit_bytes=_vmem_limit(blocks, resident, temps)),
        name="inproj",
    )(x, g, rot, *weights)


def _toeplitz(row, nrows):
    x = jnp.broadcast_to(row * LOG2E, (nrows, TOEPLITZ_W))
    left = (BAND_Q - 1) - lax.broadcasted_iota(jnp.int32, (nrows, TOEPLITZ_W), 0)
    for b in range((BAND_Q - 1).bit_length()):
        rolled = pltpu.roll(x, TOEPLITZ_W - (1 << b), axis=1)
        x = jnp.where(((left >> b) & 1) == 1, rolled, x)
    return x


def _band_body(rb_ref, q_ref, k_ref, v_ref, o_ref, t_ref, sa_ref, sb_ref, *, n_tiles):
    @pl.when(pl.program_id(1) == 0)
    def _():
        t = _toeplitz(rb_ref[...], BAND_Q)[:, 0:BAND_K]
        i = lax.broadcasted_iota(jnp.int32, (BAND_Q, BAND_K), 0)
        j = lax.broadcasted_iota(jnp.int32, (BAND_Q, BAND_K), 1)
        first = lax.shift_left(lax.shift_right_logical(i, CHUNK.bit_length() - 1), CHUNK.bit_length() - 1)
        visible = (j >= first) & (j < first + BAND_PAST + CHUNK)
        t_ref[...] = jnp.where(visible, t, NEG)

    ones = jnp.ones((BAND_K, LANES), BF16)
    buffers = (sa_ref, sb_ref)
    past_tiles = BAND_PAST // BAND_Q

    def band(t):
        return max(t - past_tiles, 0) * BAND_Q, min(t + 1, past_tiles + 1) * BAND_Q

    def issue(t):
        k0, nk = band(t)
        s = _dot_nt(q_ref[t * BAND_Q:(t + 1) * BAND_Q, :], k_ref[k0:k0 + nk, :]) + t_ref[:, BAND_K - nk:BAND_K]
        buffers[t % 2][:, 0:nk] = s
        return jnp.max(s, axis=-1, keepdims=True)

    row_max = issue(0)
    for t in range(n_tiles):
        next_max = issue(t + 1) if t + 1 < n_tiles else None
        k0, nk = band(t)
        p = jnp.exp2(buffers[t % 2][:, 0:nk] - row_max).astype(BF16)
        acc = _dot(p, jnp.concatenate([v_ref[k0:k0 + nk, :], ones[0:nk]], axis=1))
        o_ref[t * BAND_Q:(t + 1) * BAND_Q, :] = (acc[:, 0:A_HEAD_DIM] / acc[:, A_HEAD_DIM:]).astype(BF16)
        row_max = next_max


def _band_prompt(q, k, v, rb_rows):
    b, s, _ = q.shape
    assert s % BAND_Q == 0 and s >= BAND_K
    head = pl.BlockSpec((None, s, A_HEAD_DIM), lambda h, i: (i, 0, h))
    blocks = 4 * _nbytes((s, A_HEAD_DIM), BF16)
    scratch = 3 * _nbytes((BAND_Q, BAND_K), F32)
    temps = 16 * _nbytes((BAND_Q, TOEPLITZ_W), F32)
    return pl.pallas_call(
        functools.partial(_band_body, n_tiles=s // BAND_Q),
        grid=(A_HEADS, b),
        in_specs=[pl.BlockSpec((None, 1, TOEPLITZ_W), lambda h, i: (h, 0, 0)), head, head, head],
        out_specs=head,
        out_shape=jax.ShapeDtypeStruct(q.shape, BF16),
        scratch_shapes=[pltpu.VMEM((BAND_Q, BAND_K), F32)] * 3,
        compiler_params=pltpu.CompilerParams(
            dimension_semantics=("arbitrary", "arbitrary"),
            vmem_limit_bytes=_vmem_limit(blocks, scratch, temps)),
        name="band_prompt",
    )(rb_rows, q, k, v)


def _band_sample_body(rb_ref, q_ref, kn_ref, vn_ref, ck_ref, cv_ref, o_ref):
    t_new = q_ref.shape[0]
    n_cache = ck_ref.shape[0]
    for h in range(A_HEADS):
        cols = slice(h * A_HEAD_DIM, (h + 1) * A_HEAD_DIM)
        bias = _toeplitz(rb_ref[h], t_new)
        q = q_ref[:, cols]
        s_c = _dot_nt(q, ck_ref[:, h, :].astype(BF16)) + bias[:, BAND_PAST - n_cache:BAND_PAST]
        s_n = _dot_nt(q, kn_ref[:, cols]) + bias[:, BAND_PAST:BAND_PAST + t_new]
        m = jnp.maximum(jnp.max(s_c, axis=-1, keepdims=True), jnp.max(s_n, axis=-1, keepdims=True))
        p_c = jnp.exp2(s_c - m)
        p_n = jnp.exp2(s_n - m)
        l = jnp.sum(p_c, axis=-1, keepdims=True) + jnp.sum(p_n, axis=-1, keepdims=True)
        o = _dot(p_c.astype(BF16), cv_ref[:, h, :].astype(BF16)) + _dot(p_n.astype(BF16), vn_ref[:, cols])
        o_ref[:, cols] = (o / l).astype(BF16)


def _band_sample(q, k_new, v_new, cache_k, cache_v, layer, rb_rows):
    b, t, w = q.shape
    a = cache_k.shape[2]
    assert t <= BAND_Q and a <= BAND_PAST
    new = pl.BlockSpec((None, t, w), lambda i: (i, 0, 0))
    cache = pl.BlockSpec((None, None, a, A_HEADS, A_HEAD_DIM), lambda i: (layer, i, 0, 0, 0))
    blocks = 4 * _nbytes((t, w), BF16) + 2 * _nbytes((a, w), F32)
    temps = 2 * _nbytes((a, w), F32) + 16 * _nbytes((t, TOEPLITZ_W), F32)
    return pl.pallas_call(
        _band_sample_body,
        grid=(b,),
        in_specs=[_resident(rb_rows.shape), new, new, new, cache, cache],
        out_specs=new,
        out_shape=jax.ShapeDtypeStruct(q.shape, BF16),
        compiler_params=pltpu.CompilerParams(
            dimension_semantics=("arbitrary",),
            vmem_limit_bytes=_vmem_limit(blocks, _nbytes(rb_rows.shape, F32), temps)),
        name="band_sample",
    )(rb_rows, q, k_new, v_new, cache_k, cache_v)


def _mla_body(qn_ref, qr_ref, kn_ref, kr_ref, v_ref, o_ref, sa_ref, sb_ref, *, n_tiles):
    tile = MLA_TILE
    r = lax.broadcasted_iota(jnp.int32, (tile, tile), 0)
    c = lax.broadcasted_iota(jnp.int32, (tile, tile), 1)
    chunk_bits = CHUNK.bit_length() - 1
    diag_visible = lax.shift_right_logical(c, chunk_bits) <= lax.shift_right_logical(r, chunk_bits)
    ones = jnp.ones((MLA_WIDE, LANES), BF16)
    buffers = (sa_ref, sb_ref)

    def row_max(s):
        return jnp.max(s, axis=-1, keepdims=True)

    for i in range(n_tiles):
        r0 = i * tile
        q = jnp.concatenate([qn_ref[r0:r0 + tile, :], qr_ref[r0:r0 + tile, :]], axis=1)
        blocks = [(c0, min(MLA_WIDE, r0 - c0)) for c0 in range(0, r0, MLA_WIDE)] + [(r0, tile)]

        def issue(k):
            c0, w = blocks[k]
            keys = jnp.concatenate([kn_ref[c0:c0 + w, :], kr_ref[c0:c0 + w, :]], axis=1)
            s = _dot_nt(q, keys)
            buffers[k % 2][:, 0:w] = s
            return row_max(s)

        m = jnp.full((tile, 1), NEG, F32)
        acc = jnp.zeros((tile, 2 * LANES), F32)
        s_max = issue(0)
        for k, (c0, w) in enumerate(blocks):
            last = k == len(blocks) - 1
            next_max = None if last else issue(k + 1)
            s = buffers[k % 2][:, 0:w]
            if last:
                s = jnp.where(diag_visible, s, NEG)
                s_max = row_max(s)
            m_new = jnp.maximum(m, s_max)
            p = jnp.exp2(s - m_new).astype(BF16)
            values = jnp.concatenate([v_ref[c0:c0 + w, :], ones[0:w]], axis=1)
            acc = jnp.exp2(m - m_new) * acc + _dot(p, values)
            m, s_max = m_new, next_max
        o_ref[r0:r0 + tile, :] = (acc[:, 0:V_DIM] / acc[:, V_DIM:]).astype(BF16)


def _mla_prompt(qn, qr, kn, kr, v):
    b, s, _ = qn.shape
    assert s % MLA_TILE == 0
    head = pl.BlockSpec((None, s, LANES), lambda i, h: (i, 0, h))
    shared = pl.BlockSpec((None, s, LANES), lambda i, h: (i, 0, 0))
    blocks = 6 * _nbytes((s, LANES), BF16)
    scratch = 2 * _nbytes((MLA_TILE, MLA_WIDE), F32)
    temps = 6 * _nbytes((MLA_TILE, MLA_WIDE), F32)
    return pl.pallas_call(
        functools.partial(_mla_body, n_tiles=s // MLA_TILE),
        grid=(b, MLA_HEADS),
        in_specs=[head, head, head, shared, head],
        out_specs=head,
        out_shape=jax.ShapeDtypeStruct(qn.shape, BF16),
        scratch_shapes=[pltpu.VMEM((MLA_TILE, MLA_WIDE), F32)] * 2,
        compiler_params=pltpu.CompilerParams(
            dimension_semantics=("arbitrary", "arbitrary"),
            vmem_limit_bytes=_vmem_limit(blocks, scratch, temps)),
        name="mla_prompt",
    )(qn, qr, kn, kr, v)


def _mla_sample_body(qn_ref, qr_ref, cn_ref, krn_ref, cc_ref, ckr_ref, wuk_ref, wuv_ref, o_ref):
    t_new = qn_ref.shape[0]
    q_lat = jnp.concatenate(
        [_dot(qn_ref[:, h * NOPE_DIM:(h + 1) * NOPE_DIM], wuk_ref[h]).astype(BF16) for h in range(MLA_HEADS)], axis=0)
    q_rot = jnp.concatenate(
        [qr_ref[:, (h + 1) * LANES - ROPE_DIM:(h + 1) * LANES] for h in range(MLA_HEADS)], axis=0)
    cache = cc_ref[...].astype(BF16)
    new = cn_ref[...].astype(BF16)
    s_c = _dot_nt(q_lat, cache) + _dot_nt(q_rot, ckr_ref[...].astype(BF16))
    s_n = _dot_nt(q_lat, new) + _dot_nt(q_rot, krn_ref[:, LANES - ROPE_DIM:LANES])
    m = jnp.maximum(jnp.max(s_c, axis=-1, keepdims=True), jnp.max(s_n, axis=-1, keepdims=True))
    p_c = jnp.exp2(s_c - m)
    p_n = jnp.exp2(s_n - m)
    l = jnp.sum(p_c, axis=-1, keepdims=True) + jnp.sum(p_n, axis=-1, keepdims=True)
    o_lat = ((_dot(p_c.astype(BF16), cache) + _dot(p_n.astype(BF16), new)) / l).astype(BF16)
    for h in range(MLA_HEADS):
        o_ref[:, h * V_DIM:(h + 1) * V_DIM] = _dot(o_lat[h * t_new:(h + 1) * t_new, :], wuv_ref[h]).astype(BF16)


def _mla_sample(qn, qr, ckv_new, kr_new, cache_ckv, cache_kr, layer, wuk, wuv):
    b, t, w = qn.shape
    past = cache_ckv.shape[2]

    def per_batch(shape):
        return pl.BlockSpec((None,) + shape, lambda i: (i, 0, 0))

    def cached(shape):
        return pl.BlockSpec((None, None) + shape, lambda i: (layer, i, 0, 0))

    blocks = (2 * _nbytes((t, w), BF16) + _nbytes((t, KV_LORA), F32) + _nbytes((t, LANES), BF16)
              + _nbytes((past, KV_LORA), F32) + _nbytes((past, ROPE_DIM), F32) + _nbytes((t, w), BF16))
    resident = _nbytes(wuk.shape, BF16) + _nbytes(wuv.shape, BF16)
    temps = _nbytes((past, KV_LORA), F32) + 4 * _nbytes((MLA_HEADS * t, past), F32)
    return pl.pallas_call(
        _mla_sample_body,
        grid=(b,),
        in_specs=[per_batch((t, w)), per_batch((t, w)), per_batch((t, KV_LORA)), per_batch((t, LANES)),
                  cached((past, KV_LORA)), cached((past, ROPE_DIM)),
                  _resident(wuk.shape), _resident(wuv.shape)],
        out_specs=per_batch((t, w)),
        out_shape=jax.ShapeDtypeStruct(qn.shape, BF16),
        compiler_params=pltpu.CompilerParams(
            dimension_semantics=("arbitrary",),
            vmem_limit_bytes=_vmem_limit(blocks, resident, temps)),
        name="mla_sample",
    )(qn, qr, ckv_new, kr_new, cache_ckv, cache_kr, wuk, wuv)


def _merge_body(h_ref, u_ref, oa_ref, ob_ref, wga_ref, wgb_ref, woa_ref, wob_ref, wout_ref, o_ref):
    j = pl.program_id(1)
    slab = h_ref.shape[0]
    d = o_ref.shape[1]

    @pl.when(j == 0)
    def _():
        o_ref[...] = jnp.zeros_like(o_ref)

    u = u_ref[...]
    m = (jax.nn.sigmoid(_dot(u, wga_ref[...])) * _dot(oa_ref[...], woa_ref[...])
         + jax.nn.sigmoid(_dot(u, wgb_ref[...])) * _dot(ob_ref[...], wob_ref[...])).astype(BF16)
    chunk = _col_tile(d)
    for c in range(0, d, chunk):
        o_ref[:, c:c + chunk] += _dot(m, wout_ref[:, c:c + chunk])
    o_ref[pl.ds(pl.multiple_of(j * slab, slab), slab), :] += h_ref[...]


def _merge(h, u, oa, ob, wga, wgb, woa, wob, wout):
    n, d = h.shape
    tm = min(MERGE_ROW_TILE, n)
    tn = MERGE_COL_TILE
    n_steps = d // tn
    slab = tm // n_steps
    assert d % tn == 0 and tm % n_steps == 0 and slab % 8 == 0
    blocks = (_nbytes((slab, d), F32) + _nbytes((tm, d), F32) + _nbytes((tm, d), BF16)
              + 2 * _nbytes((tm, A_WIDTH), BF16) + 2 * _nbytes((d, tn), BF16)
              + 2 * _nbytes((A_WIDTH, tn), BF16) + _nbytes((tn, d), BF16))
    temps = 8 * _nbytes((tm, max(tn, _col_tile(d))), F32)
    return pl.pallas_call(
        _merge_body,
        grid=(n // tm, n_steps),
        in_specs=[
            pl.BlockSpec((slab, d), lambda i, j: (i * n_steps + j, 0)),
            pl.BlockSpec((tm, d), lambda i, j: (i, 0)),
            pl.BlockSpec((tm, A_WIDTH), lambda i, j: (i, 0)),
            pl.BlockSpec((tm, MLA_WIDTH), lambda i, j: (i, 0)),
            pl.BlockSpec((d, tn), lambda i, j: (0, j)),
            pl.BlockSpec((d, tn), lambda i, j: (0, j)),
            pl.BlockSpec((A_WIDTH, tn), lambda i, j: (0, j)),
            pl.BlockSpec((MLA_WIDTH, tn), lambda i, j: (0, j)),
            pl.BlockSpec((tn, d), lambda i, j: (j, 0)),
        ],
        out_specs=pl.BlockSpec((tm, d), lambda i, j: (i, 0)),
        out_shape=jax.ShapeDtypeStruct((n, d), F32),
        compiler_params=pltpu.CompilerParams(
            dimension_semantics=("arbitrary", "arbitrary"),
            vmem_limit_bytes=_vmem_limit(blocks, 0, temps)),
        name="merge",
    )(h, u, oa, ob, wga, wgb, woa, wob, wout)


def _ple_body(h_ref, p_ref, gp_ref, wpg_ref, wp_ref, gf_ref, y_ref, *, final_norm):
    h = h_ref[...]
    gate = jax.nn.sigmoid(_dot(_rms(h, gp_ref[...]).astype(BF16), wpg_ref[...]))
    h = h + gate * _dot(p_ref[...].astype(BF16), wp_ref[...])
    y_ref[...] = _rms(h, gf_ref[...]) if final_norm else h


def _ple(h, p, layer, gp, wpg, wp, gf, *, final_norm):
    n, d = h.shape
    e = p.shape[2]
    tm = min(ROW_TILE, n)
    blocks = 2 * _nbytes((tm, d), F32) + _nbytes((tm, e), F32)
    resident = _nbytes(wpg.shape, BF16) + _nbytes(wp.shape, BF16)
    temps = 4 * _nbytes((tm, d), F32)
    return pl.pallas_call(
        functools.partial(_ple_body, final_norm=final_norm),
        grid=(n // tm,),
        in_specs=[
            pl.BlockSpec((tm, d), lambda i: (i, 0)),
            pl.BlockSpec((None, tm, e), lambda i: (layer, i, 0)),
            _resident((1, d)),
            _resident(wpg.shape),
            _resident(wp.shape),
            _resident((1, d)),
        ],
        out_specs=pl.BlockSpec((tm, d), lambda i: (i, 0)),
        out_shape=jax.ShapeDtypeStruct((n, d), F32),
        compiler_params=pltpu.CompilerParams(
            dimension_semantics=("arbitrary",),
            vmem_limit_bytes=_vmem_limit(blocks, resident, temps)),
        name="ple",
    )(h, p, gp, wpg, wp, gf)


def _rope_table(pos):
    half = ROPE_DIM // 2
    inv = ROPE_THETA ** (-jnp.arange(half, dtype=F32) / half)
    ang = pos.astype(F32)[:, None] * inv[None, :]
    cos, sin = jnp.cos(ang), jnp.sin(ang)
    return jnp.concatenate([cos, cos, -sin, sin], axis=1)


def _rope_columns(w):
    half = ROPE_DIM // 2
    x1, x2 = w[..., :half], w[..., half:]
    return jnp.concatenate([x1, x2, x2, x1], axis=-1)


def _layer_weights(i, d, g_ffn1, g_mix, w_in, g_cq, w_uq, w_uk, w_uv, g_ckv, rel_bias, w_oa, w_ob, w_out, g_ffn2,
                   g_ple, w_ple_gate, w_ple):
    bf = lambda a: a.astype(BF16)
    row = lambda a: a.reshape(1, -1).astype(F32)
    w = w_in[i]
    c0 = 3 * A_WIDTH
    c1 = c0 + Q_LORA
    c2 = c1 + KV_LORA
    c3 = c2 + ROPE_DIM
    uq = w_uq[i].reshape(Q_LORA, MLA_HEADS, NOPE_DIM + ROPE_DIM)
    rel = jnp.clip(jnp.arange(TOEPLITZ_W) - (BAND_PAST + BAND_Q - 1), -REL_CLIP, REL_CLIP) + REL_CLIP
    return dict(
        g_ffn1=row(g_ffn1[i]),
        g_ffn2=row(g_ffn2[i]),
        g_mix=row(g_mix[i]),
        inproj=(
            bf(w[:, 0:c0]),
            bf(w[:, c0:c1]),
            bf(jnp.concatenate([w[:, c1:c2], _rope_columns(w[:, c2:c3])], axis=1)),
            row(g_cq[i]),
            row(g_ckv[i]),
            bf(jnp.concatenate([uq[:, :, :NOPE_DIM].reshape(Q_LORA, MLA_WIDTH),
                                _rope_columns(uq[:, :, NOPE_DIM:]).reshape(Q_LORA, MLA_WIDTH)], axis=1)),
            bf(jnp.concatenate([jnp.transpose(w_uk[i], (2, 0, 1)).reshape(KV_LORA, MLA_WIDTH),
                                jnp.transpose(w_uv[i], (1, 0, 2)).reshape(KV_LORA, MLA_WIDTH)], axis=1)),
        ),
        rb_rows=rel_bias[i][:, rel].reshape(A_HEADS, 1, TOEPLITZ_W).astype(F32),
        w_uk=bf(w_uk[i]),
        w_uv=bf(w_uv[i]),
        merge=(bf(w[:, c3:c3 + d]), bf(w[:, c3 + d:c3 + 2 * d]), bf(w_oa[i]), bf(w_ob[i]), bf(w_out[i])),
        ple=(row(g_ple[i]), bf(w_ple_gate[i]), bf(w_ple[i])),
    )


def kernel(x_prompt, x_sample, cache_a_k, cache_a_v, cache_mla_ckv, cache_mla_krope, p_prompt, p_sample, g_ffn1, w_ffn1_gate, w_ffn1_up, w_ffn1_down, g_mix, w_in, g_cq, w_uq, w_uk, w_uv, g_ckv, rel_bias, w_oa, w_ob, w_out, g_ffn2, w_ffn2_gate, w_ffn2_up, w_ffn2_down, g_ple, w_ple_gate, w_ple, g_final):
    b, s, d = x_prompt.shape
    bs, t, _ = x_sample.shape
    depth = g_ffn1.shape[0]
    past = cache_mla_ckv.shape[2]
    a_len = cache_a_k.shape[2]
    a_keep = min(BAND_PAST, s)

    rot_p = _rope_table(jnp.arange(s))
    rot_s = jnp.tile(_rope_table(past + jnp.arange(t)), (bs, 1))
    g_fin = g_final.reshape(1, d).astype(F32)

    hp = x_prompt.reshape(b * s, d)
    hs = x_sample.reshape(bs * t, d)
    outs = [[] for _ in range(8)]
    for i in range(depth):
        lw = _layer_weights(i, d, g_ffn1, g_mix, w_in, g_cq, w_uq, w_uk, w_uv, g_ckv, rel_bias, w_oa, w_ob, w_out,
                            g_ffn2, g_ple, w_ple_gate, w_ple)

        hp, hs = _ffn_both(hp, hs, lw["g_ffn1"], w_ffn1_gate, w_ffn1_up, w_ffn1_down, i)

        qa, ka, va, qn, qr, kn, vm, krp, ckv, kr, ka_tail, va_tail, u = _inproj(
            hp, lw["g_mix"], rot_p, *lw["inproj"], rows_per_seq=s, tail_rows=a_keep)
        seq = lambda a: a.reshape(b, s, a.shape[-1])
        oa = _band_prompt(seq(qa), seq(ka), seq(va), lw["rb_rows"])
        ob = _mla_prompt(seq(qn), seq(qr), seq(kn), seq(krp), seq(vm))
        hp = _merge(hp, u, oa.reshape(b * s, A_WIDTH), ob.reshape(b * s, MLA_WIDTH), *lw["merge"])
        outs[0].append(ka_tail.reshape(b, a_keep, A_HEADS, A_HEAD_DIM))
        outs[1].append(va_tail.reshape(b, a_keep, A_HEADS, A_HEAD_DIM))
        outs[2].append(ckv.reshape(b, s, KV_LORA))
        outs[3].append(kr.reshape(b, s, ROPE_DIM))

        qa, ka, va, qn, qr, _, _, krp, ckv, kr, ka_new, va_new, u = _inproj(
            hs, lw["g_mix"], rot_s, *lw["inproj"], rows_per_seq=t, tail_rows=t)
        new = lambda a: a.reshape(bs, t, a.shape[-1])
        oa = _band_sample(new(qa), new(ka), new(va), cache_a_k, cache_a_v, i, lw["rb_rows"])
        ob = _mla_sample(new(qn), new(qr), new(ckv), new(krp), cache_mla_ckv, cache_mla_krope, i,
                         lw["w_uk"], lw["w_uv"])
        hs = _merge(hs, u, oa.reshape(bs * t, A_WIDTH), ob.reshape(bs * t, MLA_WIDTH), *lw["merge"])
        outs[4].append(ka_new.reshape(bs, t, A_HEADS, A_HEAD_DIM))
        outs[5].append(va_new.reshape(bs, t, A_HEADS, A_HEAD_DIM))
        outs[6].append(ckv.reshape(bs, t, KV_LORA))
        outs[7].append(kr.reshape(bs, t, ROPE_DIM))

        hp, hs = _ffn_both(hp, hs, lw["g_ffn2"], w_ffn2_gate, w_ffn2_up, w_ffn2_down, i)
        last = i == depth - 1
        hp = _ple(hp, p_prompt.reshape(depth, b * s, -1), i, *lw["ple"], g_fin, final_norm=last)
        hs = _ple(hs, p_sample.reshape(depth, bs * t, -1), i, *lw["ple"], g_fin, final_norm=last)

    stacked = [o[0][None] if depth == 1 else jnp.stack(o) for o in outs]
    return (hp.reshape(b, s, d), hs.reshape(bs, t, d), *stacked)
```

```python
import functools

import jax
import jax.numpy as jnp
from jax import lax
from jax.experimental import pallas as pl
from jax.experimental.pallas import tpu as pltpu

F32 = jnp.float32
BF16 = jnp.bfloat16

CHUNK = 64
BAND_PAST = 512
A_HEADS = 8
A_HEAD_DIM = 128
A_WIDTH = A_HEADS * A_HEAD_DIM
REL_CLIP = 128
MLA_HEADS = 8
Q_LORA = 768
KV_LORA = 512
NOPE_DIM = 128
ROPE_DIM = 64
V_DIM = 128
MLA_WIDTH = MLA_HEADS * V_DIM
LOG2E = 1.4426950408889634
MLA_SCALE = (NOPE_DIM + ROPE_DIM) ** -0.5 * LOG2E
A_SCALE = A_HEAD_DIM ** -0.5 * LOG2E
ROPE_THETA = 10000.0
EPS = 1e-6
NEG = -1e30

LANES = 128
V7X_VMEM_BYTES = 64 * 1024 * 1024
VMEM_COMPILER_RESERVE = 6 * 1024 * 1024
VMEM_CEILING = V7X_VMEM_BYTES - VMEM_COMPILER_RESERVE
VMEM_ESTIMATE_SLACK = 4 * 1024 * 1024

ROW_TILE = 512
MERGE_ROW_TILE = 1024
MERGE_COL_TILE = 256
FFN_ROW_TILE = 1024
INPROJ_ROW_TILE = 256
COL_TILE = 512
BAND_Q = 4 * CHUNK
BAND_K = BAND_PAST + BAND_Q
TOEPLITZ_W = BAND_K + BAND_Q
MLA_TILE = 512
MLA_WIDE = 1024


def _dot(a, b):
    return jnp.dot(a, b, preferred_element_type=F32)


def _dot_nt(a, b):
    return lax.dot_general(a, b, (((1,), (1,)), ((), ())), preferred_element_type=F32)


def _rms(x, g):
    return x * lax.rsqrt(jnp.mean(x * x, axis=-1, keepdims=True) + EPS) * g


def _vmem_limit(block_bytes, scratch_bytes, temp_bytes):
    return int(min(2 * block_bytes + scratch_bytes + temp_bytes + VMEM_ESTIMATE_SLACK, VMEM_CEILING))


def _nbytes(shape, dtype):
    n = 1
    for s in shape:
        n *= s
    return n * jnp.dtype(dtype).itemsize


def _resident(shape):
    nd = len(shape)
    return pl.BlockSpec(shape, lambda *_: (0,) * nd, pipeline_mode=pl.Buffered(1))


def _col_tile(n):
    for t in (COL_TILE, 256, LANES):
        if n % t == 0:
            return t
    raise ValueError(f"width {n} is not a multiple of {LANES}")


def _ffn_body(x_ref, g_ref, wg_ref, wu_ref, wd_ref, o_ref, u_ref):
    @pl.when(pl.program_id(1) == 0)
    def _():
        x = x_ref[...]
        u_ref[...] = _rms(x, g_ref[...]).astype(BF16)
        o_ref[...] = x

    u = u_ref[...]
    gate = _dot(u, wg_ref[...])
    up = _dot(u, wu_ref[...])
    half_act = (gate * jax.nn.sigmoid(gate) * (0.5 * up)).astype(BF16)
    d = o_ref.shape[1]
    chunk = _col_tile(d)
    for c in range(0, d, chunk):
        o_ref[:, c:c + chunk] += _dot(half_act, wd_ref[:, c:c + chunk])


def _ffn(x, g, wg, wu, wd):
    n, d = x.shape
    f = wg.shape[1]
    tm = min(FFN_ROW_TILE, n)
    tf = _col_tile(f)
    blocks = _nbytes((tm, d), F32) * 2 + 3 * _nbytes((d, tf), BF16)
    scratch = _nbytes((tm, d), BF16)
    temps = 6 * _nbytes((tm, tf), F32)
    return pl.pallas_call(
        _ffn_body,
        grid=(n // tm, f // tf),
        in_specs=[
            pl.BlockSpec((tm, d), lambda i, j: (i, 0)),
            pl.BlockSpec((1, d), lambda i, j: (0, 0)),
            pl.BlockSpec((d, tf), lambda i, j: (0, j)),
            pl.BlockSpec((d, tf), lambda i, j: (0, j)),
            pl.BlockSpec((tf, d), lambda i, j: (j, 0)),
        ],
        out_specs=pl.BlockSpec((tm, d), lambda i, j: (i, 0)),
        out_shape=jax.ShapeDtypeStruct((n, d), F32),
        scratch_shapes=[pltpu.VMEM((tm, d), BF16)],
        compiler_params=pltpu.CompilerParams(
            dimension_semantics=("arbitrary", "arbitrary"),
            vmem_limit_bytes=_vmem_limit(blocks, scratch, temps)),
        name="ffn",
    )(x, g, wg, wu, wd)


def _ffn_cast_body(x_ref, g_ref, wg_ref, wu_ref, wd_ref, o_ref, wg_o, wu_o, wd_o, u_ref):
    @pl.when(pl.program_id(1) == 0)
    def _():
        x = x_ref[...]
        u_ref[...] = _rms(x, g_ref[...]).astype(BF16)
        o_ref[...] = x

    wg = wg_ref[...].astype(BF16)
    wu = wu_ref[...].astype(BF16)
    wd = wd_ref[...].astype(BF16)
    wg_o[...] = wg
    wu_o[...] = wu
    wd_o[...] = wd
    u = u_ref[...]
    gate = _dot(u, wg)
    up = _dot(u, wu)
    half_act = (gate * jax.nn.sigmoid(gate) * (0.5 * up)).astype(BF16)
    o_ref[...] += _dot(half_act, wd)


def _ffn_cast(x, g, wg, wu, wd, layer):
    n, d = x.shape
    f = wg.shape[2]
    assert n <= FFN_ROW_TILE
    tf = _col_tile(f)
    blocks = (2 * _nbytes((n, d), F32) + 3 * _nbytes((d, tf), F32) + 3 * _nbytes((d, tf), BF16))
    scratch = _nbytes((n, d), BF16)
    temps = 6 * _nbytes((n, tf), F32) + 3 * _nbytes((d, tf), BF16) + _nbytes((n, d), F32)
    return pl.pallas_call(
        _ffn_cast_body,
        grid=(1, f // tf),
        in_specs=[
            pl.BlockSpec((n, d), lambda i, j: (0, 0)),
            pl.BlockSpec((1, d), lambda i, j: (0, 0)),
            pl.BlockSpec((None, d, tf), lambda i, j: (layer, 0, j)),
            pl.BlockSpec((None, d, tf), lambda i, j: (layer, 0, j)),
            pl.BlockSpec((None, tf, d), lambda i, j: (layer, j, 0)),
        ],
        out_specs=[
            pl.BlockSpec((n, d), lambda i, j: (0, 0)),
            pl.BlockSpec((d, tf), lambda i, j: (0, j)),
            pl.BlockSpec((d, tf), lambda i, j: (0, j)),
            pl.BlockSpec((tf, d), lambda i, j: (j, 0)),
        ],
        out_shape=[
            jax.ShapeDtypeStruct((n, d), F32),
            jax.ShapeDtypeStruct((d, f), BF16),
            jax.ShapeDtypeStruct((d, f), BF16),
            jax.ShapeDtypeStruct((f, d), BF16),
        ],
        scratch_shapes=[pltpu.VMEM((n, d), BF16)],
        compiler_params=pltpu.CompilerParams(
            dimension_semantics=("arbitrary", "arbitrary"),
            vmem_limit_bytes=_vmem_limit(blocks, scratch, temps)),
        name="ffn_cast",
    )(x, g, wg, wu, wd)


def _ffn_both(h_many, h_few, g, wg, wu, wd, layer):
    if h_few.shape[0] <= FFN_ROW_TILE:
        h_few, wg, wu, wd = _ffn_cast(h_few, g, wg, wu, wd, layer)
    else:
        wg, wu, wd = wg[layer].astype(BF16), wu[layer].astype(BF16), wd[layer].astype(BF16)
        h_few = _ffn(h_few, g, wg, wu, wd)
    return _ffn(h_many, g, wg, wu, wd), h_few


def _inproj_body(x_ref, g_ref, rot_ref, wa_ref, wc_ref, wk_ref, gcq_ref, gckv_ref, wuq_ref, wkv_ref,
                 qa_o, ka_o, va_o, qn_o, qr_o, kn_o, vm_o, krp_o, ckv_o, kr_o, kat_o, vat_o, u_o):
    u = _rms(x_ref[...], g_ref[...]).astype(BF16)
    u_o[...] = u

    rot = rot_ref[...]
    half = LANES // 2

    cqn = _rms(_dot(u, wc_ref[...]), gcq_ref[...]).astype(BF16)
    qn_o[...] = (_dot(cqn, wuq_ref[:, 0:MLA_WIDTH]) * MLA_SCALE).astype(BF16)
    t = _dot(cqn, wuq_ref[:, MLA_WIDTH:2 * MLA_WIDTH]) * jnp.concatenate([rot] * MLA_HEADS, axis=1)
    qr_o[...] = ((t + pltpu.roll(t, half, axis=1)) * MLA_SCALE).astype(BF16)

    zc = _dot(u, wk_ref[...])
    ckvn = _rms(zc[:, 0:KV_LORA], gckv_ref[...])
    ckv_o[...] = ckvn
    t = zc[:, KV_LORA:KV_LORA + LANES] * rot
    kr = t + pltpu.roll(t, half, axis=1)
    lane = lax.broadcasted_iota(jnp.int32, kr.shape, 1)
    krp_o[...] = jnp.where(lane >= half, kr, 0.0).astype(BF16)
    kr_o[...] = kr[:, 0:ROPE_DIM]

    kv = _dot(ckvn.astype(BF16), wkv_ref[...])
    kn_o[...] = kv[:, 0:MLA_WIDTH].astype(BF16)
    vm_o[...] = kv[:, MLA_WIDTH:2 * MLA_WIDTH].astype(BF16)

    qa_o[...] = (_dot(u, wa_ref[:, 0:A_WIDTH]) * A_SCALE).astype(BF16)
    ka = _dot(u, wa_ref[:, A_WIDTH:2 * A_WIDTH])
    ka_o[...] = ka.astype(BF16)
    kat_o[...] = ka
    va = _dot(u, wa_ref[:, 2 * A_WIDTH:3 * A_WIDTH])
    va_o[...] = va.astype(BF16)
    vat_o[...] = va


def _inproj(x, g, rot, wa, wc, wk, gcq, gckv, wuq, wkv, *, rows_per_seq, tail_rows):
    n, d = x.shape
    tm = min(INPROJ_ROW_TILE, n)
    n_tiles = n // tm
    tab_tiles = rot.shape[0] // tm
    if tm <= rows_per_seq:
        tiles_per_seq = rows_per_seq // tm
        tail_tiles = tail_rows // tm
        n_seq = n // rows_per_seq

        def tail_map(i):
            return (i // tiles_per_seq) * tail_tiles + jnp.maximum(i % tiles_per_seq - (tiles_per_seq - tail_tiles), 0), 0

        tail_n = n_seq * tail_rows
    else:
        assert tail_rows == rows_per_seq

        def tail_map(i):
            return i, 0

        tail_n = n

    def row(i):
        return i, 0

    def tab(i):
        return i % tab_tiles, 0

    wide = pl.BlockSpec((tm, A_WIDTH), row)
    out_shapes = [jax.ShapeDtypeStruct((n, A_WIDTH), BF16)] * 7 + [
        jax.ShapeDtypeStruct((n, LANES), BF16),
        jax.ShapeDtypeStruct((n, KV_LORA), F32),
        jax.ShapeDtypeStruct((n, ROPE_DIM), F32),
        jax.ShapeDtypeStruct((tail_n, A_WIDTH), F32),
        jax.ShapeDtypeStruct((tail_n, A_WIDTH), F32),
        jax.ShapeDtypeStruct((n, d), BF16),
    ]
    out_specs = [wide] * 7 + [
        pl.BlockSpec((tm, LANES), row),
        pl.BlockSpec((tm, KV_LORA), row),
        pl.BlockSpec((tm, ROPE_DIM), row),
        pl.BlockSpec((tm, A_WIDTH), tail_map),
        pl.BlockSpec((tm, A_WIDTH), tail_map),
        pl.BlockSpec((tm, d), row),
    ]
    weights = [wa, wc, wk, gcq, gckv, wuq, wkv]
    blocks = (_nbytes((tm, d), F32) + _nbytes((tm, d), BF16) + 7 * _nbytes((tm, A_WIDTH), BF16)
              + 3 * _nbytes((tm, A_WIDTH), F32)
              + 4 * _nbytes((tm, LANES), F32))
    resident = sum(_nbytes(w.shape, w.dtype) for w in weights)
    temps = 6 * _nbytes((tm, 2 * A_WIDTH), F32)
    return pl.pallas_call(
        _inproj_body,
        grid=(n_tiles,),
        in_specs=[
            pl.BlockSpec((tm, d), row),
            _resident((1, d)),
            pl.BlockSpec((tm, LANES), tab),
        ] + [_resident(w.shape) for w in weights],
        out_specs=out_specs,
        out_shape=out_shapes,
        compiler_params=pltpu.CompilerParams(
            dimension_semantics=("arbitrary",),
            vmem_limit_bytes=_vmem_limit(blocks, resident, temps)),
        name="inproj",
    )(x, g, rot, *weights)


def _toeplitz(row, nrows):
    x = jnp.broadcast_to(row * LOG2E, (nrows, TOEPLITZ_W))
    left = (BAND_Q - 1) - lax.broadcasted_iota(jnp.int32, (nrows, TOEPLITZ_W), 0)
    for b in range((BAND_Q - 1).bit_length()):
        rolled = pltpu.roll(x, TOEPLITZ_W - (1 << b), axis=1)
        x = jnp.where(((left >> b) & 1) == 1, rolled, x)
    return x


def _band_body(rb_ref, q_ref, k_ref, v_ref, o_ref, t_ref, sa_ref, sb_ref, *, n_tiles):
    @pl.when(pl.program_id(1) == 0)
    def _():
        t = _toeplitz(rb_ref[...], BAND_Q)[:, 0:BAND_K]
        i = lax.broadcasted_iota(jnp.int32, (BAND_Q, BAND_K), 0)
        j = lax.broadcasted_iota(jnp.int32, (BAND_Q, BAND_K), 1)
        first = lax.shift_left(lax.shift_right_logical(i, CHUNK.bit_length() - 1), CHUNK.bit_length() - 1)
        visible = (j >= first) & (j < first + BAND_PAST + CHUNK)
        t_ref[...] = jnp.where(visible, t, NEG)

    ones = jnp.ones((BAND_K, LANES), BF16)
    buffers = (sa_ref, sb_ref)
    past_tiles = BAND_PAST // BAND_Q

    def band(t):
        return max(t - past_tiles, 0) * BAND_Q, min(t + 1, past_tiles + 1) * BAND_Q

    def issue(t):
        k0, nk = band(t)
        s = _dot_nt(q_ref[t * BAND_Q:(t + 1) * BAND_Q, :], k_ref[k0:k0 + nk, :]) + t_ref[:, BAND_K - nk:BAND_K]
        buffers[t % 2][:, 0:nk] = s
        return jnp.max(s, axis=-1, keepdims=True)

    row_max = issue(0)
    for t in range(n_tiles):
        next_max = issue(t + 1) if t + 1 < n_tiles else None
        k0, nk = band(t)
        p = jnp.exp2(buffers[t % 2][:, 0:nk] - row_max).astype(BF16)
        acc = _dot(p, jnp.concatenate([v_ref[k0:k0 + nk, :], ones[0:nk]], axis=1))
        o_ref[t * BAND_Q:(t + 1) * BAND_Q, :] = (acc[:, 0:A_HEAD_DIM] / acc[:, A_HEAD_DIM:]).astype(BF16)
        row_max = next_max


def _band_prompt(q, k, v, rb_rows):
    b, s, _ = q.shape
    assert s % BAND_Q == 0 and s >= BAND_K
    head = pl.BlockSpec((None, s, A_HEAD_DIM), lambda h, i: (i, 0, h))
    blocks = 4 * _nbytes((s, A_HEAD_DIM), BF16)
    scratch = 3 * _nbytes((BAND_Q, BAND_K), F32)
    temps = 16 * _nbytes((BAND_Q, TOEPLITZ_W), F32)
    return pl.pallas_call(
        functools.partial(_band_body, n_tiles=s // BAND_Q),
        grid=(A_HEADS, b),
        in_specs=[pl.BlockSpec((None, 1, TOEPLITZ_W), lambda h, i: (h, 0, 0)), head, head, head],
        out_specs=head,
        out_shape=jax.ShapeDtypeStruct(q.shape, BF16),
        scratch_shapes=[pltpu.VMEM((BAND_Q, BAND_K), F32)] * 3,
        compiler_params=pltpu.CompilerParams(
            dimension_semantics=("arbitrary", "arbitrary"),
            vmem_limit_bytes=_vmem_limit(blocks, scratch, temps)),
        name="band_prompt",
    )(rb_rows, q, k, v)


def _band_sample_body(rb_ref, q_ref, kn_ref, vn_ref, ck_ref, cv_ref, o_ref):
    t_new = q_ref.shape[0]
    n_cache = ck_ref.shape[0]
    for h in range(A_HEADS):
        cols = slice(h * A_HEAD_DIM, (h + 1) * A_HEAD_DIM)
        bias = _toeplitz(rb_ref[h], t_new)
        q = q_ref[:, cols]
        s_c = _dot_nt(q, ck_ref[:, h, :].astype(BF16)) + bias[:, BAND_PAST - n_cache:BAND_PAST]
        s_n = _dot_nt(q, kn_ref[:, cols]) + bias[:, BAND_PAST:BAND_PAST + t_new]
        m = jnp.maximum(jnp.max(s_c, axis=-1, keepdims=True), jnp.max(s_n, axis=-1, keepdims=True))
        p_c = jnp.exp2(s_c - m)
        p_n = jnp.exp2(s_n - m)
        l = jnp.sum(p_c, axis=-1, keepdims=True) + jnp.sum(p_n, axis=-1, keepdims=True)
        o = _dot(p_c.astype(BF16), cv_ref[:, h, :].astype(BF16)) + _dot(p_n.astype(BF16), vn_ref[:, cols])
        o_ref[:, cols] = (o / l).astype(BF16)


def _band_sample(q, k_new, v_new, cache_k, cache_v, layer, rb_rows):
    b, t, w = q.shape
    a = cache_k.shape[2]
    assert t <= BAND_Q and a <= BAND_PAST
    new = pl.BlockSpec((None, t, w), lambda i: (i, 0, 0))
    cache = pl.BlockSpec((None, None, a, A_HEADS, A_HEAD_DIM), lambda i: (layer, i, 0, 0, 0))
    blocks = 4 * _nbytes((t, w), BF16) + 2 * _nbytes((a, w), F32)
    temps = 2 * _nbytes((a, w), F32) + 16 * _nbytes((t, TOEPLITZ_W), F32)
    return pl.pallas_call(
        _band_sample_body,
        grid=(b,),
        in_specs=[_resident(rb_rows.shape), new, new, new, cache, cache],
        out_specs=new,
        out_shape=jax.ShapeDtypeStruct(q.shape, BF16),
        compiler_params=pltpu.CompilerParams(
            dimension_semantics=("arbitrary",),
            vmem_limit_bytes=_vmem_limit(blocks, _nbytes(rb_rows.shape, F32), temps)),
        name="band_sample",
    )(rb_rows, q, k_new, v_new, cache_k, cache_v)


def _mla_body(qn_ref, qr_ref, kn_ref, kr_ref, v_ref, o_ref, sa_ref, sb_ref, *, n_tiles):
    tile = MLA_TILE
    r = lax.broadcasted_iota(jnp.int32, (tile, tile), 0)
    c = lax.broadcasted_iota(jnp.int32, (tile, tile), 1)
    chunk_bits = CHUNK.bit_length() - 1
    diag_visible = lax.shift_right_logical(c, chunk_bits) <= lax.shift_right_logical(r, chunk_bits)
    ones = jnp.ones((MLA_WIDE, LANES), BF16)
    buffers = (sa_ref, sb_ref)

    def row_max(s):
        return jnp.max(s, axis=-1, keepdims=True)

    for i in range(n_tiles):
        r0 = i * tile
        q = jnp.concatenate([qn_ref[r0:r0 + tile, :], qr_ref[r0:r0 + tile, :]], axis=1)
        blocks = [(c0, min(MLA_WIDE, r0 - c0)) for c0 in range(0, r0, MLA_WIDE)] + [(r0, tile)]

        def issue(k):
            c0, w = blocks[k]
            keys = jnp.concatenate([kn_ref[c0:c0 + w, :], kr_ref[c0:c0 + w, :]], axis=1)
            s = _dot_nt(q, keys)
            buffers[k % 2][:, 0:w] = s
            return row_max(s)

        m = jnp.full((tile, 1), NEG, F32)
        acc = jnp.zeros((tile, 2 * LANES), F32)
        s_max = issue(0)
        for k, (c0, w) in enumerate(blocks):
            last = k == len(blocks) - 1
            next_max = None if last else issue(k + 1)
            s = buffers[k % 2][:, 0:w]
            if last:
                s = jnp.where(diag_visible, s, NEG)
                s_max = row_max(s)
            m_new = jnp.maximum(m, s_max)
            p = jnp.exp2(s - m_new).astype(BF16)
            values = jnp.concatenate([v_ref[c0:c0 + w, :], ones[0:w]], axis=1)
            acc = jnp.exp2(m - m_new) * acc + _dot(p, values)
            m, s_max = m_new, next_max
        o_ref[r0:r0 + tile, :] = (acc[:, 0:V_DIM] / acc[:, V_DIM:]).astype(BF16)


def _mla_prompt(qn, qr, kn, kr, v):
    b, s, _ = qn.shape
    assert s % MLA_TILE == 0
    head = pl.BlockSpec((None, s, LANES), lambda i, h: (i, 0, h))
    shared = pl.BlockSpec((None, s, LANES), lambda i, h: (i, 0, 0))
    blocks = 6 * _nbytes((s, LANES), BF16)
    scratch = 2 * _nbytes((MLA_TILE, MLA_WIDE), F32)
    temps = 6 * _nbytes((MLA_TILE, MLA_WIDE), F32)
    return pl.pallas_call(
        functools.partial(_mla_body, n_tiles=s // MLA_TILE),
        grid=(b, MLA_HEADS),
        in_specs=[head, head, head, shared, head],
        out_specs=head,
        out_shape=jax.ShapeDtypeStruct(qn.shape, BF16),
        scratch_shapes=[pltpu.VMEM((MLA_TILE, MLA_WIDE), F32)] * 2,
        compiler_params=pltpu.CompilerParams(
            dimension_semantics=("arbitrary", "arbitrary"),
            vmem_limit_bytes=_vmem_limit(blocks, scratch, temps)),
        name="mla_prompt",
    )(qn, qr, kn, kr, v)


def _mla_sample_body(qn_ref, qr_ref, cn_ref, krn_ref, cc_ref, ckr_ref, wuk_ref, wuv_ref, o_ref):
    t_new = qn_ref.shape[0]
    q_lat = jnp.concatenate(
        [_dot(qn_ref[:, h * NOPE_DIM:(h + 1) * NOPE_DIM], wuk_ref[h]).astype(BF16) for h in range(MLA_HEADS)], axis=0)
    q_rot = jnp.concatenate(
        [qr_ref[:, (h + 1) * LANES - ROPE_DIM:(h + 1) * LANES] for h in range(MLA_HEADS)], axis=0)
    cache = cc_ref[...].astype(BF16)
    new = cn_ref[...].astype(BF16)
    s_c = _dot_nt(q_lat, cache) + _dot_nt(q_rot, ckr_ref[...].astype(BF16))
    s_n = _dot_nt(q_lat, new) + _dot_nt(q_rot, krn_ref[:, LANES - ROPE_DIM:LANES])
    m = jnp.maximum(jnp.max(s_c, axis=-1, keepdims=True), jnp.max(s_n, axis=-1, keepdims=True))
    p_c = jnp.exp2(s_c - m)
    p_n = jnp.exp2(s_n - m)
    l = jnp.sum(p_c, axis=-1, keepdims=True) + jnp.sum(p_n, axis=-1, keepdims=True)
    o_lat = ((_dot(p_c.astype(BF16), cache) + _dot(p_n.astype(BF16), new)) / l).astype(BF16)
    for h in range(MLA_HEADS):
        o_ref[:, h * V_DIM:(h + 1) * V_DIM] = _dot(o_lat[h * t_new:(h + 1) * t_new, :], wuv_ref[h]).astype(BF16)


def _mla_sample(qn, qr, ckv_new, kr_new, cache_ckv, cache_kr, layer, wuk, wuv):
    b, t, w = qn.shape
    past = cache_ckv.shape[2]

    def per_batch(shape):
        return pl.BlockSpec((None,) + shape, lambda i: (i, 0, 0))

    def cached(shape):
        return pl.BlockSpec((None, None) + shape, lambda i: (layer, i, 0, 0))

    blocks = (2 * _nbytes((t, w), BF16) + _nbytes((t, KV_LORA), F32) + _nbytes((t, LANES), BF16)
              + _nbytes((past, KV_LORA), F32) + _nbytes((past, ROPE_DIM), F32) + _nbytes((t, w), BF16))
    resident = _nbytes(wuk.shape, BF16) + _nbytes(wuv.shape, BF16)
    temps = _nbytes((past, KV_LORA), F32) + 4 * _nbytes((MLA_HEADS * t, past), F32)
    return pl.pallas_call(
        _mla_sample_body,
        grid=(b,),
        in_specs=[per_batch((t, w)), per_batch((t, w)), per_batch((t, KV_LORA)), per_batch((t, LANES)),
                  cached((past, KV_LORA)), cached((past, ROPE_DIM)),
                  _resident(wuk.shape), _resident(wuv.shape)],
        out_specs=per_batch((t, w)),
        out_shape=jax.ShapeDtypeStruct(qn.shape, BF16),
        compiler_params=pltpu.CompilerParams(
            dimension_semantics=("arbitrary",),
            vmem_limit_bytes=_vmem_limit(blocks, resident, temps)),
        name="mla_sample",
    )(qn, qr, ckv_new, kr_new, cache_ckv, cache_kr, wuk, wuv)


def _merge_body(h_ref, u_ref, oa_ref, ob_ref, wga_ref, wgb_ref, woa_ref, wob_ref, wout_ref, o_ref):
    j = pl.program_id(1)
    slab = h_ref.shape[0]
    d = o_ref.shape[1]

    @pl.when(j == 0)
    def _():
        o_ref[...] = jnp.zeros_like(o_ref)

    u = u_ref[...]
    m = (jax.nn.sigmoid(_dot(u, wga_ref[...])) * _dot(oa_ref[...], woa_ref[...])
         + jax.nn.sigmoid(_dot(u, wgb_ref[...])) * _dot(ob_ref[...], wob_ref[...])).astype(BF16)
    chunk = _col_tile(d)
    for c in range(0, d, chunk):
        o_ref[:, c:c + chunk] += _dot(m, wout_ref[:, c:c + chunk])
    o_ref[pl.ds(pl.multiple_of(j * slab, slab), slab), :] += h_ref[...]


def _merge(h, u, oa, ob, wga, wgb, woa, wob, wout):
    n, d = h.shape
    tm = min(MERGE_ROW_TILE, n)
    tn = MERGE_COL_TILE
    n_steps = d // tn
    slab = tm // n_steps
    assert d % tn == 0 and tm % n_steps == 0 and slab % 8 == 0
    blocks = (_nbytes((slab, d), F32) + _nbytes((tm, d), F32) + _nbytes((tm, d), BF16)
              + 2 * _nbytes((tm, A_WIDTH), BF16) + 2 * _nbytes((d, tn), BF16)
              + 2 * _nbytes((A_WIDTH, tn), BF16) + _nbytes((tn, d), BF16))
    temps = 8 * _nbytes((tm, max(tn, _col_tile(d))), F32)
    return pl.pallas_call(
        _merge_body,
        grid=(n // tm, n_steps),
        in_specs=[
            pl.BlockSpec((slab, d), lambda i, j: (i * n_steps + j, 0)),
            pl.BlockSpec((tm, d), lambda i, j: (i, 0)),
            pl.BlockSpec((tm, A_WIDTH), lambda i, j: (i, 0)),
            pl.BlockSpec((tm, MLA_WIDTH), lambda i, j: (i, 0)),
            pl.BlockSpec((d, tn), lambda i, j: (0, j)),
            pl.BlockSpec((d, tn), lambda i, j: (0, j)),
            pl.BlockSpec((A_WIDTH, tn), lambda i, j: (0, j)),
            pl.BlockSpec((MLA_WIDTH, tn), lambda i, j: (0, j)),
            pl.BlockSpec((tn, d), lambda i, j: (j, 0)),
        ],
        out_specs=pl.BlockSpec((tm, d), lambda i, j: (i, 0)),
        out_shape=jax.ShapeDtypeStruct((n, d), F32),
        compiler_params=pltpu.CompilerParams(
            dimension_semantics=("arbitrary", "arbitrary"),
            vmem_limit_bytes=_vmem_limit(blocks, 0, temps)),
        name="merge",
    )(h, u, oa, ob, wga, wgb, woa, wob, wout)


def _ple_body(h_ref, p_ref, gp_ref, wpg_ref, wp_ref, gf_ref, y_ref, *, final_norm):
    h = h_ref[...]
    gate = jax.nn.sigmoid(_dot(_rms(h, gp_ref[...]).astype(BF16), wpg_ref[...]))
    h = h + gate * _dot(p_ref[...].astype(BF16), wp_ref[...])
    y_ref[...] = _rms(h, gf_ref[...]) if final_norm else h


def _ple(h, p, layer, gp, wpg, wp, gf, *, final_norm):
    n, d = h.shape
    e = p.shape[2]
    tm = min(ROW_TILE, n)
    blocks = 2 * _nbytes((tm, d), F32) + _nbytes((tm, e), F32)
    resident = _nbytes(wpg.shape, BF16) + _nbytes(wp.shape, BF16)
    temps = 4 * _nbytes((tm, d), F32)
    return pl.pallas_call(
        functools.partial(_ple_body, final_norm=final_norm),
        grid=(n // tm,),
        in_specs=[
            pl.BlockSpec((tm, d), lambda i: (i, 0)),
            pl.BlockSpec((None, tm, e), lambda i: (layer, i, 0)),
            _resident((1, d)),
            _resident(wpg.shape),
            _resident(wp.shape),
            _resident((1, d)),
        ],
        out_specs=pl.BlockSpec((tm, d), lambda i: (i, 0)),
        out_shape=jax.ShapeDtypeStruct((n, d), F32),
        compiler_params=pltpu.CompilerParams(
            dimension_semantics=("arbitrary",),
            vmem_limit_bytes=_vmem_limit(blocks, resident, temps)),
        name="ple",
    )(h, p, gp, wpg, wp, gf)


def _rope_table(pos):
    half = ROPE_DIM // 2
    inv = ROPE_THETA ** (-jnp.arange(half, dtype=F32) / half)
    ang = pos.astype(F32)[:, None] * inv[None, :]
    cos, sin = jnp.cos(ang), jnp.sin(ang)
    return jnp.concatenate([cos, cos, -sin, sin], axis=1)


def _rope_columns(w):
    half = ROPE_DIM // 2
    x1, x2 = w[..., :half], w[..., half:]
    return jnp.concatenate([x1, x2, x2, x1], axis=-1)


def _layer_weights(i, d, g_ffn1, g_mix, w_in, g_cq, w_uq, w_uk, w_uv, g_ckv, rel_bias, w_oa, w_ob, w_out, g_ffn2,
                   g_ple, w_ple_gate, w_ple):
    bf = lambda a: a.astype(BF16)
    row = lambda a: a.reshape(1, -1).astype(F32)
    w = w_in[i]
    c0 = 3 * A_WIDTH
    c1 = c0 + Q_LORA
    c2 = c1 + KV_LORA
    c3 = c2 + ROPE_DIM
    uq = w_uq[i].reshape(Q_LORA, MLA_HEADS, NOPE_DIM + ROPE_DIM)
    rel = jnp.clip(jnp.arange(TOEPLITZ_W) - (BAND_PAST + BAND_Q - 1), -REL_CLIP, REL_CLIP) + REL_CLIP
    return dict(
        g_ffn1=row(g_ffn1[i]),
        g_ffn2=row(g_ffn2[i]),
        g_mix=row(g_mix[i]),
        inproj=(
            bf(w[:, 0:c0]),
            bf(w[:, c0:c1]),
            bf(jnp.concatenate([w[:, c1:c2], _rope_columns(w[:, c2:c3])], axis=1)),
            row(g_cq[i]),
            row(g_ckv[i]),
            bf(jnp.concatenate([uq[:, :, :NOPE_DIM].reshape(Q_LORA, MLA_WIDTH),
                                _rope_columns(uq[:, :, NOPE_DIM:]).reshape(Q_LORA, MLA_WIDTH)], axis=1)),
            bf(jnp.concatenate([jnp.transpose(w_uk[i], (2, 0, 1)).reshape(KV_LORA, MLA_WIDTH),
                                jnp.transpose(w_uv[i], (1, 0, 2)).reshape(KV_LORA, MLA_WIDTH)], axis=1)),
        ),
        rb_rows=rel_bias[i][:, rel].reshape(A_HEADS, 1, TOEPLITZ_W).astype(F32),
        w_uk=bf(w_uk[i]),
        w_uv=bf(w_uv[i]),
        merge=(bf(w[:, c3:c3 + d]), bf(w[:, c3 + d:c3 + 2 * d]), bf(w_oa[i]), bf(w_ob[i]), bf(w_out[i])),
        ple=(row(g_ple[i]), bf(w_ple_gate[i]), bf(w_ple[i])),
    )


def kernel(x_prompt, x_sample, cache_a_k, cache_a_v, cache_mla_ckv, cache_mla_krope, p_prompt, p_sample, g_ffn1, w_ffn1_gate, w_ffn1_up, w_ffn1_down, g_mix, w_in, g_cq, w_uq, w_uk, w_uv, g_ckv, rel_bias, w_oa, w_ob, w_out, g_ffn2, w_ffn2_gate, w_ffn2_up, w_ffn2_down, g_ple, w_ple_gate, w_ple, g_final):
    b, s, d = x_prompt.shape
    bs, t, _ = x_sample.shape
    depth = g_ffn1.shape[0]
    past = cache_mla_ckv.shape[2]
    a_len = cache_a_k.shape[2]
    a_keep = min(BAND_PAST, s)

    rot_p = _rope_table(jnp.arange(s))
    rot_s = jnp.tile(_rope_table(past + jnp.arange(t)), (bs, 1))
    g_fin = g_final.reshape(1, d).astype(F32)

    hp = x_prompt.reshape(b * s, d)
    hs = x_sample.reshape(bs * t, d)
    outs = [[] for _ in range(8)]
    for i in range(depth):
        lw = _layer_weights(i, d, g_ffn1, g_mix, w_in, g_cq, w_uq, w_uk, w_uv, g_ckv, rel_bias, w_oa, w_ob, w_out,
                            g_ffn2, g_ple, w_ple_gate, w_ple)

        hp, hs = _ffn_both(hp, hs, lw["g_ffn1"], w_ffn1_gate, w_ffn1_up, w_ffn1_down, i)

        qa, ka, va, qn, qr, kn, vm, krp, ckv, kr, ka_tail, va_tail, u = _inproj(
            hp, lw["g_mix"], rot_p, *lw["inproj"], rows_per_seq=s, tail_rows=a_keep)
        seq = lambda a: a.reshape(b, s, a.shape[-1])
        oa = _band_prompt(seq(qa), seq(ka), seq(va), lw["rb_rows"])
        ob = _mla_prompt(seq(qn), seq(qr), seq(kn), seq(krp), seq(vm))
        hp = _merge(hp, u, oa.reshape(b * s, A_WIDTH), ob.reshape(b * s, MLA_WIDTH), *lw["merge"])
        outs[0].append(ka_tail.reshape(b, a_keep, A_HEADS, A_HEAD_DIM))
        outs[1].append(va_tail.reshape(b, a_keep, A_HEADS, A_HEAD_DIM))
        outs[2].append(ckv.reshape(b, s, KV_LORA))
        outs[3].append(kr.reshape(b, s, ROPE_DIM))

        qa, ka, va, qn, qr, _, _, krp, ckv, kr, ka_new, va_new, u = _inproj(
            hs, lw["g_mix"], rot_s, *lw["inproj"], rows_per_seq=t, tail_rows=t)
        new = lambda a: a.reshape(bs, t, a.shape[-1])
        oa = _band_sample(new(qa), new(ka), new(va), cache_a_k, cache_a_v, i, lw["rb_rows"])
        ob = _mla_sample(new(qn), new(qr), new(ckv), new(krp), cache_mla_ckv, cache_mla_krope, i,
                         lw["w_uk"], lw["w_uv"])
        hs = _merge(hs, u, oa.reshape(bs * t, A_WIDTH), ob.reshape(bs * t, MLA_WIDTH), *lw["merge"])
        outs[4].append(ka_new.reshape(bs, t, A_HEADS, A_HEAD_DIM))
        outs[5].append(va_new.reshape(bs, t, A_HEADS, A_HEAD_DIM))
        outs[6].append(ckv.reshape(bs, t, KV_LORA))
        outs[7].append(kr.reshape(bs, t, ROPE_DIM))

        hp, hs = _ffn_both(hp, hs, lw["g_ffn2"], w_ffn2_gate, w_ffn2_up, w_ffn2_down, i)
        last = i == depth - 1
        hp = _ple(hp, p_prompt.reshape(depth, b * s, -1), i, *lw["ple"], g_fin, final_norm=last)
        hs = _ple(hs, p_sample.reshape(depth, bs * t, -1), i, *lw["ple"], g_fin, final_norm=last)

    stacked = [o[0][None] if depth == 1 else jnp.stack(o) for o in outs]
    return (hp.reshape(b, s, d), hs.reshape(bs, t, d), *stacked)
```

```python
import functools

import jax
import jax.numpy as jnp
from jax import lax
from jax.experimental import pallas as pl
from jax.experimental.pallas import tpu as pltpu

F32 = jnp.float32
BF16 = jnp.bfloat16

CHUNK = 64
BAND_PAST = 512
A_HEADS = 8
A_HEAD_DIM = 128
A_WIDTH = A_HEADS * A_HEAD_DIM
REL_CLIP = 128
MLA_HEADS = 8
Q_LORA = 768
KV_LORA = 512
NOPE_DIM = 128
ROPE_DIM = 64
V_DIM = 128
MLA_WIDTH = MLA_HEADS * V_DIM
LOG2E = 1.4426950408889634
MLA_SCALE = (NOPE_DIM + ROPE_DIM) ** -0.5 * LOG2E
A_SCALE = A_HEAD_DIM ** -0.5 * LOG2E
ROPE_THETA = 10000.0
EPS = 1e-6
NEG = -1e30

LANES = 128
V7X_VMEM_BYTES = 64 * 1024 * 1024
VMEM_COMPILER_RESERVE = 6 * 1024 * 1024
VMEM_CEILING = V7X_VMEM_BYTES - VMEM_COMPILER_RESERVE
VMEM_ESTIMATE_SLACK = 4 * 1024 * 1024

ROW_TILE = 512
MERGE_ROW_TILE = 1024
MERGE_COL_TILE = 256
FFN_ROW_TILE = 1024
INPROJ_ROW_TILE = 256
COL_TILE = 512
BAND_Q = 4 * CHUNK
BAND_K = BAND_PAST + BAND_Q
TOEPLITZ_W = BAND_K + BAND_Q
MLA_TILE = 512
MLA_WIDE = 1024


def _dot(a, b):
    return jnp.dot(a, b, preferred_element_type=F32)


def _dot_nt(a, b):
    return lax.dot_general(a, b, (((1,), (1,)), ((), ())), preferred_element_type=F32)


def _rms(x, g):
    return x * lax.rsqrt(jnp.mean(x * x, axis=-1, keepdims=True) + EPS) * g


def _vmem_limit(block_bytes, scratch_bytes, temp_bytes):
    return int(min(2 * block_bytes + scratch_bytes + temp_bytes + VMEM_ESTIMATE_SLACK, VMEM_CEILING))


def _nbytes(shape, dtype):
    n = 1
    for s in shape:
        n *= s
    return n * jnp.dtype(dtype).itemsize


def _resident(shape):
    nd = len(shape)
    return pl.BlockSpec(shape, lambda *_: (0,) * nd, pipeline_mode=pl.Buffered(1))


def _col_tile(n):
    for t in (COL_TILE, 256, LANES):
        if n % t == 0:
            return t
    raise ValueError(f"width {n} is not a multiple of {LANES}")


def _ffn_body(x_ref, g_ref, wgu_ref, wd_ref, o_ref, u_ref):
    @pl.when(pl.program_id(1) == 0)
    def _():
        x = x_ref[...]
        u_ref[...] = _rms(x, g_ref[...]).astype(BF16)
        o_ref[...] = x

    u = u_ref[...]
    tf = wd_ref.shape[0]
    gate = _dot(u, wgu_ref[:, 0:tf])
    up = _dot(u, wgu_ref[:, tf:2 * tf])
    half_act = (gate * jax.nn.sigmoid(gate) * (0.5 * up)).astype(BF16)
    d = o_ref.shape[1]
    chunk = _col_tile(d)
    for c in range(0, d, chunk):
        o_ref[:, c:c + chunk] += _dot(half_act, wd_ref[:, c:c + chunk])


def _ffn(x, g, wgu, wd):
    n, d = x.shape
    f = wd.shape[0]
    tm = min(FFN_ROW_TILE, n)
    tf = _col_tile(f)
    blocks = _nbytes((tm, d), F32) * 2 + 3 * _nbytes((d, tf), BF16)
    scratch = _nbytes((tm, d), BF16)
    temps = 6 * _nbytes((tm, tf), F32)
    return pl.pallas_call(
        _ffn_body,
        grid=(n // tm, f // tf),
        in_specs=[
            pl.BlockSpec((tm, d), lambda i, j: (i, 0)),
            pl.BlockSpec((1, d), lambda i, j: (0, 0)),
            pl.BlockSpec((d, 2 * tf), lambda i, j: (0, j)),
            pl.BlockSpec((tf, d), lambda i, j: (j, 0)),
        ],
        out_specs=pl.BlockSpec((tm, d), lambda i, j: (i, 0)),
        out_shape=jax.ShapeDtypeStruct((n, d), F32),
        scratch_shapes=[pltpu.VMEM((tm, d), BF16)],
        compiler_params=pltpu.CompilerParams(
            dimension_semantics=("arbitrary", "arbitrary"),
            vmem_limit_bytes=_vmem_limit(blocks, scratch, temps)),
        name="ffn",
    )(x, g, wgu, wd)


def _ffn_cast_body(x_ref, g_ref, wg_ref, wu_ref, wd_ref, o_ref, wgu_o, wd_o, u_ref):
    @pl.when(pl.program_id(1) == 0)
    def _():
        x = x_ref[...]
        u_ref[...] = _rms(x, g_ref[...]).astype(BF16)
        o_ref[...] = x

    wg = wg_ref[...].astype(BF16)
    wu = wu_ref[...].astype(BF16)
    wd = wd_ref[...].astype(BF16)
    tf = wd.shape[0]
    wgu_o[:, 0:tf] = wg
    wgu_o[:, tf:2 * tf] = wu
    wd_o[...] = wd
    u = u_ref[...]
    gate = _dot(u, wg)
    up = _dot(u, wu)
    half_act = (gate * jax.nn.sigmoid(gate) * (0.5 * up)).astype(BF16)
    o_ref[...] += _dot(half_act, wd)


def _ffn_cast(x, g, wg, wu, wd, layer):
    n, d = x.shape
    f = wg.shape[2]
    assert n <= FFN_ROW_TILE
    tf = _col_tile(f)
    blocks = (2 * _nbytes((n, d), F32) + 3 * _nbytes((d, tf), F32) + 3 * _nbytes((d, tf), BF16))
    scratch = _nbytes((n, d), BF16)
    temps = 6 * _nbytes((n, tf), F32) + 3 * _nbytes((d, tf), BF16) + _nbytes((n, d), F32)
    return pl.pallas_call(
        _ffn_cast_body,
        grid=(1, f // tf),
        in_specs=[
            pl.BlockSpec((n, d), lambda i, j: (0, 0)),
            pl.BlockSpec((1, d), lambda i, j: (0, 0)),
            pl.BlockSpec((None, d, tf), lambda i, j: (layer, 0, j)),
            pl.BlockSpec((None, d, tf), lambda i, j: (layer, 0, j)),
            pl.BlockSpec((None, tf, d), lambda i, j: (layer, j, 0)),
        ],
        out_specs=[
            pl.BlockSpec((n, d), lambda i, j: (0, 0)),
            pl.BlockSpec((d, 2 * tf), lambda i, j: (0, j)),
            pl.BlockSpec((tf, d), lambda i, j: (j, 0)),
        ],
        out_shape=[
            jax.ShapeDtypeStruct((n, d), F32),
            jax.ShapeDtypeStruct((d, 2 * f), BF16),
            jax.ShapeDtypeStruct((f, d), BF16),
        ],
        scratch_shapes=[pltpu.VMEM((n, d), BF16)],
        compiler_params=pltpu.CompilerParams(
            dimension_semantics=("arbitrary", "arbitrary"),
            vmem_limit_bytes=_vmem_limit(blocks, scratch, temps)),
        name="ffn_cast",
    )(x, g, wg, wu, wd)


def _ffn_both(h_many, h_few, g, wg, wu, wd, layer):
    if h_few.shape[0] <= FFN_ROW_TILE:
        h_few, wgu, wd = _ffn_cast(h_few, g, wg, wu, wd, layer)
    else:
        d, f = wg.shape[1:]
        tf = _col_tile(f)
        tiles = lambda w: w[layer].astype(BF16).reshape(d, f // tf, tf)
        wgu = jnp.concatenate([tiles(wg), tiles(wu)], axis=2).reshape(d, 2 * f)
        wd = wd[layer].astype(BF16)
        h_few = _ffn(h_few, g, wgu, wd)
    return _ffn(h_many, g, wgu, wd), h_few


def _inproj_body(x_ref, g_ref, rot_ref, wa_ref, wc_ref, wk_ref, gcq_ref, gckv_ref, wuq_ref, wkv_ref,
                 qa_o, ka_o, va_o, qn_o, qr_o, kn_o, vm_o, krp_o, ckv_o, kr_o, kat_o, vat_o, u_o):
    u = _rms(x_ref[...], g_ref[...]).astype(BF16)
    u_o[...] = u

    rot = rot_ref[...]
    half = LANES // 2

    cqn = _rms(_dot(u, wc_ref[...]), gcq_ref[...]).astype(BF16)
    qn_o[...] = (_dot(cqn, wuq_ref[:, 0:MLA_WIDTH]) * MLA_SCALE).astype(BF16)
    t = _dot(cqn, wuq_ref[:, MLA_WIDTH:2 * MLA_WIDTH]) * jnp.concatenate([rot] * MLA_HEADS, axis=1)
    qr_o[...] = ((t + pltpu.roll(t, half, axis=1)) * MLA_SCALE).astype(BF16)

    zc = _dot(u, wk_ref[...])
    ckvn = _rms(zc[:, 0:KV_LORA], gckv_ref[...])
    ckv_o[...] = ckvn
    t = zc[:, KV_LORA:KV_LORA + LANES] * rot
    kr = t + pltpu.roll(t, half, axis=1)
    lane = lax.broadcasted_iota(jnp.int32, kr.shape, 1)
    krp_o[...] = jnp.where(lane >= half, kr, 0.0).astype(BF16)
    kr_o[...] = kr[:, 0:ROPE_DIM]

    kv = _dot(ckvn.astype(BF16), wkv_ref[...])
    kn_o[...] = kv[:, 0:MLA_WIDTH].astype(BF16)
    vm_o[...] = kv[:, MLA_WIDTH:2 * MLA_WIDTH].astype(BF16)

    qa_o[...] = (_dot(u, wa_ref[:, 0:A_WIDTH]) * A_SCALE).astype(BF16)
    ka = _dot(u, wa_ref[:, A_WIDTH:2 * A_WIDTH])
    ka_o[...] = ka.astype(BF16)
    kat_o[...] = ka
    va = _dot(u, wa_ref[:, 2 * A_WIDTH:3 * A_WIDTH])
    va_o[...] = va.astype(BF16)
    vat_o[...] = va


def _inproj(x, g, rot, wa, wc, wk, gcq, gckv, wuq, wkv, *, rows_per_seq, tail_rows):
    n, d = x.shape
    tm = min(INPROJ_ROW_TILE, n)
    n_tiles = n // tm
    tab_tiles = rot.shape[0] // tm
    if tm <= rows_per_seq:
        tiles_per_seq = rows_per_seq // tm
        tail_tiles = tail_rows // tm
        n_seq = n // rows_per_seq

        def tail_map(i):
            return (i // tiles_per_seq) * tail_tiles + jnp.maximum(i % tiles_per_seq - (tiles_per_seq - tail_tiles), 0), 0

        tail_n = n_seq * tail_rows
    else:
        assert tail_rows == rows_per_seq

        def tail_map(i):
            return i, 0

        tail_n = n

    def row(i):
        return i, 0

    def tab(i):
        return i % tab_tiles, 0

    wide = pl.BlockSpec((tm, A_WIDTH), row)
    out_shapes = [jax.ShapeDtypeStruct((n, A_WIDTH), BF16)] * 7 + [
        jax.ShapeDtypeStruct((n, LANES), BF16),
        jax.ShapeDtypeStruct((n, KV_LORA), F32),
        jax.ShapeDtypeStruct((n, ROPE_DIM), F32),
        jax.ShapeDtypeStruct((tail_n, A_WIDTH), F32),
        jax.ShapeDtypeStruct((tail_n, A_WIDTH), F32),
        jax.ShapeDtypeStruct((n, d), BF16),
    ]
    out_specs = [wide] * 7 + [
        pl.BlockSpec((tm, LANES), row),
        pl.BlockSpec((tm, KV_LORA), row),
        pl.BlockSpec((tm, ROPE_DIM), row),
        pl.BlockSpec((tm, A_WIDTH), tail_map),
        pl.BlockSpec((tm, A_WIDTH), tail_map),
        pl.BlockSpec((tm, d), row),
    ]
    weights = [wa, wc, wk, gcq, gckv, wuq, wkv]
    blocks = (_nbytes((tm, d), F32) + _nbytes((tm, d), BF16) + 7 * _nbytes((tm, A_WIDTH), BF16)
              + 3 * _nbytes((tm, A_WIDTH), F32)
              + 4 * _nbytes((tm, LANES), F32))
    resident = sum(_nbytes(w.shape, w.dtype) for w in weights)
    temps = 6 * _nbytes((tm, 2 * A_WIDTH), F32)
    return pl.pallas_call(
        _inproj_body,
        grid=(n_tiles,),
        in_specs=[
            pl.BlockSpec((tm, d), row),
            _resident((1, d)),
            pl.BlockSpec((tm, LANES), tab),
        ] + [_resident(w.shape) for w in weights],
        out_specs=out_specs,
        out_shape=out_shapes,
        compiler_params=pltpu.CompilerParams(
            dimension_semantics=("arbitrary",),
            vmem_limit_bytes=_vmem_limit(blocks, resident, temps)),
        name="inproj",
    )(x, g, rot, *weights)


def _toeplitz(row, nrows):
    x = jnp.broadcast_to(row * LOG2E, (nrows, TOEPLITZ_W))
    left = (BAND_Q - 1) - lax.broadcasted_iota(jnp.int32, (nrows, TOEPLITZ_W), 0)
    for b in range((BAND_Q - 1).bit_length()):
        rolled = pltpu.roll(x, TOEPLITZ_W - (1 << b), axis=1)
        x = jnp.where(((left >> b) & 1) == 1, rolled, x)
    return x


def _band_body(rb_ref, q_ref, k_ref, v_ref, o_ref, t_ref, sa_ref, sb_ref, *, n_tiles):
    @pl.when(pl.program_id(1) == 0)
    def _():
        t = _toeplitz(rb_ref[...], BAND_Q)[:, 0:BAND_K]
        i = lax.broadcasted_iota(jnp.int32, (BAND_Q, BAND_K), 0)
        j = lax.broadcasted_iota(jnp.int32, (BAND_Q, BAND_K), 1)
        first = lax.shift_left(lax.shift_right_logical(i, CHUNK.bit_length() - 1), CHUNK.bit_length() - 1)
        visible = (j >= first) & (j < first + BAND_PAST + CHUNK)
        t_ref[...] = jnp.where(visible, t, NEG)

    ones = jnp.ones((BAND_K, LANES), BF16)
    buffers = (sa_ref, sb_ref)
    past_tiles = BAND_PAST // BAND_Q

    def band(t):
        return max(t - past_tiles, 0) * BAND_Q, min(t + 1, past_tiles + 1) * BAND_Q

    def issue(t):
        k0, nk = band(t)
        s = _dot_nt(q_ref[t * BAND_Q:(t + 1) * BAND_Q, :], k_ref[k0:k0 + nk, :]) + t_ref[:, BAND_K - nk:BAND_K]
        buffers[t % 2][:, 0:nk] = s
        return jnp.max(s, axis=-1, keepdims=True)

    row_max = issue(0)
    for t in range(n_tiles):
        next_max = issue(t + 1) if t + 1 < n_tiles else None
        k0, nk = band(t)
        p = jnp.exp2(buffers[t % 2][:, 0:nk] - row_max).astype(BF16)
        acc = _dot(p, jnp.concatenate([v_ref[k0:k0 + nk, :], ones[0:nk]], axis=1))
        o_ref[t * BAND_Q:(t + 1) * BAND_Q, :] = (acc[:, 0:A_HEAD_DIM] / acc[:, A_HEAD_DIM:]).astype(BF16)
        row_max = next_max


def _band_prompt(q, k, v, rb_rows):
    b, s, _ = q.shape
    assert s % BAND_Q == 0 and s >= BAND_K
    head = pl.BlockSpec((None, s, A_HEAD_DIM), lambda h, i: (i, 0, h))
    blocks = 4 * _nbytes((s, A_HEAD_DIM), BF16)
    scratch = 3 * _nbytes((BAND_Q, BAND_K), F32)
    temps = 16 * _nbytes((BAND_Q, TOEPLITZ_W), F32)
    return pl.pallas_call(
        functools.partial(_band_body, n_tiles=s // BAND_Q),
        grid=(A_HEADS, b),
        in_specs=[pl.BlockSpec((None, 1, TOEPLITZ_W), lambda h, i: (h, 0, 0)), head, head, head],
        out_specs=head,
        out_shape=jax.ShapeDtypeStruct(q.shape, BF16),
        scratch_shapes=[pltpu.VMEM((BAND_Q, BAND_K), F32)] * 3,
        compiler_params=pltpu.CompilerParams(
            dimension_semantics=("arbitrary", "arbitrary"),
            vmem_limit_bytes=_vmem_limit(blocks, scratch, temps)),
        name="band_prompt",
    )(rb_rows, q, k, v)


def _band_sample_body(rb_ref, q_ref, kn_ref, vn_ref, ck_ref, cv_ref, o_ref):
    t_new = q_ref.shape[0]
    n_cache = ck_ref.shape[0]
    for h in range(A_HEADS):
        cols = slice(h * A_HEAD_DIM, (h + 1) * A_HEAD_DIM)
        bias = _toeplitz(rb_ref[h], t_new)
        q = q_ref[:, cols]
        s_c = _dot_nt(q, ck_ref[:, h, :].astype(BF16)) + bias[:, BAND_PAST - n_cache:BAND_PAST]
        s_n = _dot_nt(q, kn_ref[:, cols]) + bias[:, BAND_PAST:BAND_PAST + t_new]
        m = jnp.maximum(jnp.max(s_c, axis=-1, keepdims=True), jnp.max(s_n, axis=-1, keepdims=True))
        p_c = jnp.exp2(s_c - m)
        p_n = jnp.exp2(s_n - m)
        l = jnp.sum(p_c, axis=-1, keepdims=True) + jnp.sum(p_n, axis=-1, keepdims=True)
        o = _dot(p_c.astype(BF16), cv_ref[:, h, :].astype(BF16)) + _dot(p_n.astype(BF16), vn_ref[:, cols])
        o_ref[:, cols] = (o / l).astype(BF16)


def _band_sample(q, k_new, v_new, cache_k, cache_v, layer, rb_rows):
    b, t, w = q.shape
    a = cache_k.shape[2]
    assert t <= BAND_Q and a <= BAND_PAST
    new = pl.BlockSpec((None, t, w), lambda i: (i, 0, 0))
    cache = pl.BlockSpec((None, None, a, A_HEADS, A_HEAD_DIM), lambda i: (layer, i, 0, 0, 0))
    blocks = 4 * _nbytes((t, w), BF16) + 2 * _nbytes((a, w), F32)
    temps = 2 * _nbytes((a, w), F32) + 16 * _nbytes((t, TOEPLITZ_W), F32)
    return pl.pallas_call(
        _band_sample_body,
        grid=(b,),
        in_specs=[_resident(rb_rows.shape), new, new, new, cache, cache],
        out_specs=new,
        out_shape=jax.ShapeDtypeStruct(q.shape, BF16),
        compiler_params=pltpu.CompilerParams(
            dimension_semantics=("arbitrary",),
            vmem_limit_bytes=_vmem_limit(blocks, _nbytes(rb_rows.shape, F32), temps)),
        name="band_sample",
    )(rb_rows, q, k_new, v_new, cache_k, cache_v)


def _mla_body(qn_ref, qr_ref, kn_ref, kr_ref, v_ref, o_ref, sa_ref, sb_ref, *, n_tiles):
    tile = MLA_TILE
    r = lax.broadcasted_iota(jnp.int32, (tile, tile), 0)
    c = lax.broadcasted_iota(jnp.int32, (tile, tile), 1)
    chunk_bits = CHUNK.bit_length() - 1
    diag_visible = lax.shift_right_logical(c, chunk_bits) <= lax.shift_right_logical(r, chunk_bits)
    ones = jnp.ones((MLA_WIDE, LANES), BF16)
    buffers = (sa_ref, sb_ref)

    def row_max(s):
        return jnp.max(s, axis=-1, keepdims=True)

    for i in range(n_tiles):
        r0 = i * tile
        q = jnp.concatenate([qn_ref[r0:r0 + tile, :], qr_ref[r0:r0 + tile, :]], axis=1)
        blocks = [(c0, min(MLA_WIDE, r0 - c0)) for c0 in range(0, r0, MLA_WIDE)] + [(r0, tile)]

        def issue(k):
            c0, w = blocks[k]
            keys = jnp.concatenate([kn_ref[c0:c0 + w, :], kr_ref[c0:c0 + w, :]], axis=1)
            s = _dot_nt(q, keys)
            buffers[k % 2][:, 0:w] = s
            return row_max(s)

        m = jnp.full((tile, 1), NEG, F32)
        acc = jnp.zeros((tile, 2 * LANES), F32)
        s_max = issue(0)
        for k, (c0, w) in enumerate(blocks):
            last = k == len(blocks) - 1
            next_max = None if last else issue(k + 1)
            s = buffers[k % 2][:, 0:w]
            if last:
                s = jnp.where(diag_visible, s, NEG)
                s_max = row_max(s)
            m_new = jnp.maximum(m, s_max)
            p = jnp.exp2(s - m_new).astype(BF16)
            values = jnp.concatenate([v_ref[c0:c0 + w, :], ones[0:w]], axis=1)
            acc = jnp.exp2(m - m_new) * acc + _dot(p, values)
            m, s_max = m_new, next_max
        o_ref[r0:r0 + tile, :] = (acc[:, 0:V_DIM] / acc[:, V_DIM:]).astype(BF16)


def _mla_prompt(qn, qr, kn, kr, v):
    b, s, _ = qn.shape
    assert s % MLA_TILE == 0
    head = pl.BlockSpec((None, s, LANES), lambda i, h: (i, 0, h))
    shared = pl.BlockSpec((None, s, LANES), lambda i, h: (i, 0, 0))
    blocks = 6 * _nbytes((s, LANES), BF16)
    scratch = 2 * _nbytes((MLA_TILE, MLA_WIDE), F32)
    temps = 6 * _nbytes((MLA_TILE, MLA_WIDE), F32)
    return pl.pallas_call(
        functools.partial(_mla_body, n_tiles=s // MLA_TILE),
        grid=(b, MLA_HEADS),
        in_specs=[head, head, head, shared, head],
        out_specs=head,
        out_shape=jax.ShapeDtypeStruct(qn.shape, BF16),
        scratch_shapes=[pltpu.VMEM((MLA_TILE, MLA_WIDE), F32)] * 2,
        compiler_params=pltpu.CompilerParams(
            dimension_semantics=("arbitrary", "arbitrary"),
            vmem_limit_bytes=_vmem_limit(blocks, scratch, temps)),
        name="mla_prompt",
    )(qn, qr, kn, kr, v)


def _mla_sample_body(qn_ref, qr_ref, cn_ref, krn_ref, cc_ref, ckr_ref, wuk_ref, wuv_ref, o_ref):
    t_new = qn_ref.shape[0]
    q_lat = jnp.concatenate(
        [_dot(qn_ref[:, h * NOPE_DIM:(h + 1) * NOPE_DIM], wuk_ref[h]).astype(BF16) for h in range(MLA_HEADS)], axis=0)
    q_rot = jnp.concatenate(
        [qr_ref[:, (h + 1) * LANES - ROPE_DIM:(h + 1) * LANES] for h in range(MLA_HEADS)], axis=0)
    cache = cc_ref[...].astype(BF16)
    new = cn_ref[...].astype(BF16)
    s_c = _dot_nt(q_lat, cache) + _dot_nt(q_rot, ckr_ref[...].astype(BF16))
    s_n = _dot_nt(q_lat, new) + _dot_nt(q_rot, krn_ref[:, LANES - ROPE_DIM:LANES])
    m = jnp.maximum(jnp.max(s_c, axis=-1, keepdims=True), jnp.max(s_n, axis=-1, keepdims=True))
    p_c = jnp.exp2(s_c - m)
    p_n = jnp.exp2(s_n - m)
    l = jnp.sum(p_c, axis=-1, keepdims=True) + jnp.sum(p_n, axis=-1, keepdims=True)
    o_lat = ((_dot(p_c.astype(BF16), cache) + _dot(p_n.astype(BF16), new)) / l).astype(BF16)
    for h in range(MLA_HEADS):
        o_ref[:, h * V_DIM:(h + 1) * V_DIM] = _dot(o_lat[h * t_new:(h + 1) * t_new, :], wuv_ref[h]).astype(BF16)


def _mla_sample(qn, qr, ckv_new, kr_new, cache_ckv, cache_kr, layer, wuk, wuv):
    b, t, w = qn.shape
    past = cache_ckv.shape[2]

    def per_batch(shape):
        return pl.BlockSpec((None,) + shape, lambda i: (i, 0, 0))

    def cached(shape):
        return pl.BlockSpec((None, None) + shape, lambda i: (layer, i, 0, 0))

    blocks = (2 * _nbytes((t, w), BF16) + _nbytes((t, KV_LORA), F32) + _nbytes((t, LANES), BF16)
              + _nbytes((past, KV_LORA), F32) + _nbytes((past, ROPE_DIM), F32) + _nbytes((t, w), BF16))
    resident = _nbytes(wuk.shape, BF16) + _nbytes(wuv.shape, BF16)
    temps = _nbytes((past, KV_LORA), F32) + 4 * _nbytes((MLA_HEADS * t, past), F32)
    return pl.pallas_call(
        _mla_sample_body,
        grid=(b,),
        in_specs=[per_batch((t, w)), per_batch((t, w)), per_batch((t, KV_LORA)), per_batch((t, LANES)),
                  cached((past, KV_LORA)), cached((past, ROPE_DIM)),
                  _resident(wuk.shape), _resident(wuv.shape)],
        out_specs=per_batch((t, w)),
        out_shape=jax.ShapeDtypeStruct(qn.shape, BF16),
        compiler_params=pltpu.CompilerParams(
            dimension_semantics=("arbitrary",),
            vmem_limit_bytes=_vmem_limit(blocks, resident, temps)),
        name="mla_sample",
    )(qn, qr, ckv_new, kr_new, cache_ckv, cache_kr, wuk, wuv)


def _merge_body(h_ref, u_ref, oa_ref, ob_ref, wg_ref, wo_ref, wout_ref, o_ref):
    j = pl.program_id(1)
    slab = h_ref.shape[0]
    d = o_ref.shape[1]

    @pl.when(j == 0)
    def _():
        o_ref[...] = jnp.zeros_like(o_ref)

    u = u_ref[...]
    tn = wout_ref.shape[0]
    ka = oa_ref.shape[1]
    m = (jax.nn.sigmoid(_dot(u, wg_ref[:, 0:tn])) * _dot(oa_ref[...], wo_ref[0:ka, :])
         + jax.nn.sigmoid(_dot(u, wg_ref[:, tn:2 * tn])) * _dot(ob_ref[...], wo_ref[ka:, :])).astype(BF16)
    chunk = _col_tile(d)
    for c in range(0, d, chunk):
        o_ref[:, c:c + chunk] += _dot(m, wout_ref[:, c:c + chunk])
    o_ref[pl.ds(pl.multiple_of(j * slab, slab), slab), :] += h_ref[...]


def _merge(h, u, oa, ob, wg, wo, wout):
    n, d = h.shape
    tm = min(MERGE_ROW_TILE, n)
    tn = MERGE_COL_TILE
    n_steps = d // tn
    slab = tm // n_steps
    assert d % tn == 0 and tm % n_steps == 0 and slab % 8 == 0
    blocks = (_nbytes((slab, d), F32) + _nbytes((tm, d), F32) + _nbytes((tm, d), BF16)
              + 2 * _nbytes((tm, A_WIDTH), BF16) + 2 * _nbytes((d, tn), BF16)
              + 2 * _nbytes((A_WIDTH, tn), BF16) + _nbytes((tn, d), BF16))
    temps = 8 * _nbytes((tm, max(tn, _col_tile(d))), F32)
    return pl.pallas_call(
        _merge_body,
        grid=(n // tm, n_steps),
        in_specs=[
            pl.BlockSpec((slab, d), lambda i, j: (i * n_steps + j, 0)),
            pl.BlockSpec((tm, d), lambda i, j: (i, 0)),
            pl.BlockSpec((tm, A_WIDTH), lambda i, j: (i, 0)),
            pl.BlockSpec((tm, MLA_WIDTH), lambda i, j: (i, 0)),
            pl.BlockSpec((d, 2 * tn), lambda i, j: (0, j)),
            pl.BlockSpec((A_WIDTH + MLA_WIDTH, tn), lambda i, j: (0, j)),
            pl.BlockSpec((tn, d), lambda i, j: (j, 0)),
        ],
        out_specs=pl.BlockSpec((tm, d), lambda i, j: (i, 0)),
        out_shape=jax.ShapeDtypeStruct((n, d), F32),
        compiler_params=pltpu.CompilerParams(
            dimension_semantics=("arbitrary", "arbitrary"),
            vmem_limit_bytes=_vmem_limit(blocks, 0, temps)),
        name="merge",
    )(h, u, oa, ob, wg, wo, wout)


def _ple_body(h_ref, p_ref, gp_ref, wpg_ref, wp_ref, gf_ref, y_ref, *, final_norm):
    h = h_ref[...]
    gate = jax.nn.sigmoid(_dot(_rms(h, gp_ref[...]).astype(BF16), wpg_ref[...]))
    h = h + gate * _dot(p_ref[...].astype(BF16), wp_ref[...])
    y_ref[...] = _rms(h, gf_ref[...]) if final_norm else h


def _ple(h, p, layer, gp, wpg, wp, gf, *, final_norm):
    n, d = h.shape
    e = p.shape[2]
    tm = min(ROW_TILE, n)
    blocks = 2 * _nbytes((tm, d), F32) + _nbytes((tm, e), F32)
    resident = _nbytes(wpg.shape, BF16) + _nbytes(wp.shape, BF16)
    temps = 4 * _nbytes((tm, d), F32)
    return pl.pallas_call(
        functools.partial(_ple_body, final_norm=final_norm),
        grid=(n // tm,),
        in_specs=[
            pl.BlockSpec((tm, d), lambda i: (i, 0)),
            pl.BlockSpec((None, tm, e), lambda i: (layer, i, 0)),
            _resident((1, d)),
            _resident(wpg.shape),
            _resident(wp.shape),
            _resident((1, d)),
        ],
        out_specs=pl.BlockSpec((tm, d), lambda i: (i, 0)),
        out_shape=jax.ShapeDtypeStruct((n, d), F32),
        compiler_params=pltpu.CompilerParams(
            dimension_semantics=("arbitrary",),
            vmem_limit_bytes=_vmem_limit(blocks, resident, temps)),
        name="ple",
    )(h, p, gp, wpg, wp, gf)


def _rope_table(pos):
    half = ROPE_DIM // 2
    inv = ROPE_THETA ** (-jnp.arange(half, dtype=F32) / half)
    ang = pos.astype(F32)[:, None] * inv[None, :]
    cos, sin = jnp.cos(ang), jnp.sin(ang)
    return jnp.concatenate([cos, cos, -sin, sin], axis=1)


def _rope_columns(w):
    half = ROPE_DIM // 2
    x1, x2 = w[..., :half], w[..., half:]
    return jnp.concatenate([x1, x2, x2, x1], axis=-1)


def _layer_weights(i, d, g_ffn1, g_mix, w_in, g_cq, w_uq, w_uk, w_uv, g_ckv, rel_bias, w_oa, w_ob, w_out, g_ffn2,
                   g_ple, w_ple_gate, w_ple):
    bf = lambda a: a.astype(BF16)
    row = lambda a: a.reshape(1, -1).astype(F32)
    w = w_in[i]
    c0 = 3 * A_WIDTH
    c1 = c0 + Q_LORA
    c2 = c1 + KV_LORA
    c3 = c2 + ROPE_DIM
    uq = w_uq[i].reshape(Q_LORA, MLA_HEADS, NOPE_DIM + ROPE_DIM)
    rel = jnp.clip(jnp.arange(TOEPLITZ_W) - (BAND_PAST + BAND_Q - 1), -REL_CLIP, REL_CLIP) + REL_CLIP
    return dict(
        g_ffn1=row(g_ffn1[i]),
        g_ffn2=row(g_ffn2[i]),
        g_mix=row(g_mix[i]),
        inproj=(
            bf(w[:, 0:c0]),
            bf(w[:, c0:c1]),
            bf(jnp.concatenate([w[:, c1:c2], _rope_columns(w[:, c2:c3])], axis=1)),
            row(g_cq[i]),
            row(g_ckv[i]),
            bf(jnp.concatenate([uq[:, :, :NOPE_DIM].reshape(Q_LORA, MLA_WIDTH),
                                _rope_columns(uq[:, :, NOPE_DIM:]).reshape(Q_LORA, MLA_WIDTH)], axis=1)),
            bf(jnp.concatenate([jnp.transpose(w_uk[i], (2, 0, 1)).reshape(KV_LORA, MLA_WIDTH),
                                jnp.transpose(w_uv[i], (1, 0, 2)).reshape(KV_LORA, MLA_WIDTH)], axis=1)),
        ),
        rb_rows=rel_bias[i][:, rel].reshape(A_HEADS, 1, TOEPLITZ_W).astype(F32),
        w_uk=bf(w_uk[i]),
        w_uv=bf(w_uv[i]),
        merge=(
            bf(jnp.concatenate([w[:, c3:c3 + d].reshape(d, d // MERGE_COL_TILE, MERGE_COL_TILE),
                                w[:, c3 + d:c3 + 2 * d].reshape(d, d // MERGE_COL_TILE, MERGE_COL_TILE)],
                               axis=2).reshape(d, 2 * d)),
            bf(jnp.concatenate([w_oa[i], w_ob[i]], axis=0)),
            bf(w_out[i]),
        ),
        ple=(row(g_ple[i]), bf(w_ple_gate[i]), bf(w_ple[i])),
    )


def kernel(x_prompt, x_sample, cache_a_k, cache_a_v, cache_mla_ckv, cache_mla_krope, p_prompt, p_sample, g_ffn1, w_ffn1_gate, w_ffn1_up, w_ffn1_down, g_mix, w_in, g_cq, w_uq, w_uk, w_uv, g_ckv, rel_bias, w_oa, w_ob, w_out, g_ffn2, w_ffn2_gate, w_ffn2_up, w_ffn2_down, g_ple, w_ple_gate, w_ple, g_final):
    b, s, d = x_prompt.shape
    bs, t, _ = x_sample.shape
    depth = g_ffn1.shape[0]
    past = cache_mla_ckv.shape[2]
    a_len = cache_a_k.shape[2]
    a_keep = min(BAND_PAST, s)

    rot_p = _rope_table(jnp.arange(s))
    rot_s = jnp.tile(_rope_table(past + jnp.arange(t)), (bs, 1))
    g_fin = g_final.reshape(1, d).astype(F32)

    hp = x_prompt.reshape(b * s, d)
    hs = x_sample.reshape(bs * t, d)
    outs = [[] for _ in range(8)]
    for i in range(depth):
        lw = _layer_weights(i, d, g_ffn1, g_mix, w_in, g_cq, w_uq, w_uk, w_uv, g_ckv, rel_bias, w_oa, w_ob, w_out,
                            g_ffn2, g_ple, w_ple_gate, w_ple)

        hp, hs = _ffn_both(hp, hs, lw["g_ffn1"], w_ffn1_gate, w_ffn1_up, w_ffn1_down, i)

        qa, ka, va, qn, qr, kn, vm, krp, ckv, kr, ka_tail, va_tail, u = _inproj(
            hp, lw["g_mix"], rot_p, *lw["inproj"], rows_per_seq=s, tail_rows=a_keep)
        seq = lambda a: a.reshape(b, s, a.shape[-1])
        oa = _band_prompt(seq(qa), seq(ka), seq(va), lw["rb_rows"])
        ob = _mla_prompt(seq(qn), seq(qr), seq(kn), seq(krp), seq(vm))
        hp = _merge(hp, u, oa.reshape(b * s, A_WIDTH), ob.reshape(b * s, MLA_WIDTH), *lw["merge"])
        outs[0].append(ka_tail.reshape(b, a_keep, A_HEADS, A_HEAD_DIM))
        outs[1].append(va_tail.reshape(b, a_keep, A_HEADS, A_HEAD_DIM))
        outs[2].append(ckv.reshape(b, s, KV_LORA))
        outs[3].append(kr.reshape(b, s, ROPE_DIM))

        qa, ka, va, qn, qr, _, _, krp, ckv, kr, ka_new, va_new, u = _inproj(
            hs, lw["g_mix"], rot_s, *lw["inproj"], rows_per_seq=t, tail_rows=t)
        new = lambda a: a.reshape(bs, t, a.shape[-1])
        oa = _band_sample(new(qa), new(ka), new(va), cache_a_k, cache_a_v, i, lw["rb_rows"])
        ob = _mla_sample(new(qn), new(qr), new(ckv), new(krp), cache_mla_ckv, cache_mla_krope, i,
                         lw["w_uk"], lw["w_uv"])
        hs = _merge(hs, u, oa.reshape(bs * t, A_WIDTH), ob.reshape(bs * t, MLA_WIDTH), *lw["merge"])
        outs[4].append(ka_new.reshape(bs, t, A_HEADS, A_HEAD_DIM))
        outs[5].append(va_new.reshape(bs, t, A_HEADS, A_HEAD_DIM))
        outs[6].append(ckv.reshape(bs, t, KV_LORA))
        outs[7].append(kr.reshape(bs, t, ROPE_DIM))

        hp, hs = _ffn_both(hp, hs, lw["g_ffn2"], w_ffn2_gate, w_ffn2_up, w_ffn2_down, i)
        last = i == depth - 1
        hp = _ple(hp, p_prompt.reshape(depth, b * s, -1), i, *lw["ple"], g_fin, final_norm=last)
        hs = _ple(hs, p_sample.reshape(depth, bs * t, -1), i, *lw["ple"], g_fin, final_norm=last)

    stacked = [o[0][None] if depth == 1 else jnp.stack(o) for o in outs]
    return (hp.reshape(b, s, d), hs.reshape(bs, t, d), *stacked)
```

```python
import functools

import jax
import jax.numpy as jnp
from jax import lax
from jax.experimental import pallas as pl
from jax.experimental.pallas import tpu as pltpu

F32 = jnp.float32
BF16 = jnp.bfloat16

CHUNK = 64
BAND_PAST = 512
A_HEADS = 8
A_HEAD_DIM = 128
A_WIDTH = A_HEADS * A_HEAD_DIM
REL_CLIP = 128
MLA_HEADS = 8
Q_LORA = 768
KV_LORA = 512
NOPE_DIM = 128
ROPE_DIM = 64
V_DIM = 128
MLA_WIDTH = MLA_HEADS * V_DIM
LOG2E = 1.4426950408889634
MLA_SCALE = (NOPE_DIM + ROPE_DIM) ** -0.5 * LOG2E
A_SCALE = A_HEAD_DIM ** -0.5 * LOG2E
ROPE_THETA = 10000.0
EPS = 1e-6
NEG = -1e30

LANES = 128
V7X_VMEM_BYTES = 64 * 1024 * 1024
VMEM_COMPILER_RESERVE = 4 * 1024 * 1024
VMEM_CEILING = V7X_VMEM_BYTES - VMEM_COMPILER_RESERVE
VMEM_ESTIMATE_SLACK = 4 * 1024 * 1024

ROW_TILE = 512
MERGE_ROW_TILE = 1024
MERGE_COL_TILE = 512
FFN_ROW_TILE = 1024
INPROJ_ROW_TILE = 256
COL_TILE = 512
BAND_Q = 4 * CHUNK
BAND_K = BAND_PAST + BAND_Q
TOEPLITZ_W = BAND_K + BAND_Q
MLA_TILE = 512
MLA_WIDE = 1024


def _dot(a, b):
    return jnp.dot(a, b, preferred_element_type=F32)


def _dot_nt(a, b):
    return lax.dot_general(a, b, (((1,), (1,)), ((), ())), preferred_element_type=F32)


def _rms(x, g):
    return x * lax.rsqrt(jnp.mean(x * x, axis=-1, keepdims=True) + EPS) * g


def _vmem_limit(block_bytes, scratch_bytes, temp_bytes):
    return int(min(2 * block_bytes + scratch_bytes + temp_bytes + VMEM_ESTIMATE_SLACK, VMEM_CEILING))


def _nbytes(shape, dtype):
    n = 1
    for s in shape:
        n *= s
    return n * jnp.dtype(dtype).itemsize


def _resident(shape):
    nd = len(shape)
    return pl.BlockSpec(shape, lambda *_: (0,) * nd, pipeline_mode=pl.Buffered(1))


def _col_tile(n):
    for t in (COL_TILE, 256, LANES):
        if n % t == 0:
            return t
    raise ValueError(f"width {n} is not a multiple of {LANES}")


def _ffn_body(x_ref, g_ref, wg_ref, wu_ref, wd_ref, o_ref, u_ref):
    @pl.when(pl.program_id(1) == 0)
    def _():
        x = x_ref[...]
        u_ref[...] = _rms(x, g_ref[...]).astype(BF16)
        o_ref[...] = x

    u = u_ref[...]
    gate = _dot(u, wg_ref[...])
    up = _dot(u, wu_ref[...])
    half_act = (gate * jax.nn.sigmoid(gate) * (0.5 * up)).astype(BF16)
    d = o_ref.shape[1]
    chunk = _col_tile(d)
    for c in range(0, d, chunk):
        o_ref[:, c:c + chunk] += _dot(half_act, wd_ref[:, c:c + chunk])


def _ffn(x, g, wg, wu, wd):
    n, d = x.shape
    f = wg.shape[1]
    tm = min(FFN_ROW_TILE, n)
    tf = _col_tile(f)
    blocks = _nbytes((tm, d), F32) * 2 + 3 * _nbytes((d, tf), BF16)
    scratch = _nbytes((tm, d), BF16)
    temps = 6 * _nbytes((tm, tf), F32)
    return pl.pallas_call(
        _ffn_body,
        grid=(n // tm, f // tf),
        in_specs=[
            pl.BlockSpec((tm, d), lambda i, j: (i, 0)),
            pl.BlockSpec((1, d), lambda i, j: (0, 0)),
            pl.BlockSpec((d, tf), lambda i, j: (0, j)),
            pl.BlockSpec((d, tf), lambda i, j: (0, j)),
            pl.BlockSpec((tf, d), lambda i, j: (j, 0)),
        ],
        out_specs=pl.BlockSpec((tm, d), lambda i, j: (i, 0)),
        out_shape=jax.ShapeDtypeStruct((n, d), F32),
        scratch_shapes=[pltpu.VMEM((tm, d), BF16)],
        compiler_params=pltpu.CompilerParams(
            dimension_semantics=("arbitrary", "arbitrary"),
            vmem_limit_bytes=_vmem_limit(blocks, scratch, temps)),
        name="ffn",
    )(x, g, wg, wu, wd)


def _ffn_cast_body(x_ref, g_ref, wg_ref, wu_ref, wd_ref, o_ref, wg_o, wu_o, wd_o, u_ref):
    @pl.when(pl.program_id(1) == 0)
    def _():
        x = x_ref[...]
        u_ref[...] = _rms(x, g_ref[...]).astype(BF16)
        o_ref[...] = x

    wg = wg_ref[...].astype(BF16)
    wu = wu_ref[...].astype(BF16)
    wd = wd_ref[...].astype(BF16)
    wg_o[...] = wg
    wu_o[...] = wu
    wd_o[...] = wd
    u = u_ref[...]
    gate = _dot(u, wg)
    up = _dot(u, wu)
    half_act = (gate * jax.nn.sigmoid(gate) * (0.5 * up)).astype(BF16)
    o_ref[...] += _dot(half_act, wd)


def _ffn_cast(x, g, wg, wu, wd, layer):
    n, d = x.shape
    f = wg.shape[2]
    assert n <= FFN_ROW_TILE
    tf = _col_tile(f)
    blocks = (2 * _nbytes((n, d), F32) + 3 * _nbytes((d, tf), F32) + 3 * _nbytes((d, tf), BF16))
    scratch = _nbytes((n, d), BF16)
    temps = 6 * _nbytes((n, tf), F32) + 3 * _nbytes((d, tf), BF16) + _nbytes((n, d), F32)
    return pl.pallas_call(
        _ffn_cast_body,
        grid=(1, f // tf),
        in_specs=[
            pl.BlockSpec((n, d), lambda i, j: (0, 0)),
            pl.BlockSpec((1, d), lambda i, j: (0, 0)),
            pl.BlockSpec((None, d, tf), lambda i, j: (layer, 0, j)),
            pl.BlockSpec((None, d, tf), lambda i, j: (layer, 0, j)),
            pl.BlockSpec((None, tf, d), lambda i, j: (layer, j, 0)),
        ],
        out_specs=[
            pl.BlockSpec((n, d), lambda i, j: (0, 0)),
            pl.BlockSpec((d, tf), lambda i, j: (0, j)),
            pl.BlockSpec((d, tf), lambda i, j: (0, j)),
            pl.BlockSpec((tf, d), lambda i, j: (j, 0)),
        ],
        out_shape=[
            jax.ShapeDtypeStruct((n, d), F32),
            jax.ShapeDtypeStruct((d, f), BF16),
            jax.ShapeDtypeStruct((d, f), BF16),
            jax.ShapeDtypeStruct((f, d), BF16),
        ],
        scratch_shapes=[pltpu.VMEM((n, d), BF16)],
        compiler_params=pltpu.CompilerParams(
            dimension_semantics=("arbitrary", "arbitrary"),
            vmem_limit_bytes=_vmem_limit(blocks, scratch, temps)),
        name="ffn_cast",
    )(x, g, wg, wu, wd)


def _ffn_both(h_many, h_few, g, wg, wu, wd, layer):
    if h_few.shape[0] <= FFN_ROW_TILE:
        h_few, wg, wu, wd = _ffn_cast(h_few, g, wg, wu, wd, layer)
    else:
        wg, wu, wd = wg[layer].astype(BF16), wu[layer].astype(BF16), wd[layer].astype(BF16)
        h_few = _ffn(h_few, g, wg, wu, wd)
    return _ffn(h_many, g, wg, wu, wd), h_few


def _inproj_body(x_ref, g_ref, rot_ref, wa_ref, wc_ref, wk_ref, gcq_ref, gckv_ref, wuq_ref, wkv_ref,
                 qa_o, ka_o, va_o, qn_o, qr_o, kn_o, vm_o, krp_o, ckv_o, kr_o, kat_o, vat_o, u_o):
    u = _rms(x_ref[...], g_ref[...]).astype(BF16)
    u_o[...] = u

    rot = rot_ref[...]
    half = LANES // 2

    cqn = _rms(_dot(u, wc_ref[...]), gcq_ref[...]).astype(BF16)
    qn_o[...] = (_dot(cqn, wuq_ref[:, 0:MLA_WIDTH]) * MLA_SCALE).astype(BF16)
    t = _dot(cqn, wuq_ref[:, MLA_WIDTH:2 * MLA_WIDTH]) * jnp.concatenate([rot] * MLA_HEADS, axis=1)
    qr_o[...] = ((t + pltpu.roll(t, half, axis=1)) * MLA_SCALE).astype(BF16)

    zc = _dot(u, wk_ref[...])
    ckvn = _rms(zc[:, 0:KV_LORA], gckv_ref[...])
    ckv_o[...] = ckvn
    t = zc[:, KV_LORA:KV_LORA + LANES] * rot
    kr = t + pltpu.roll(t, half, axis=1)
    lane = lax.broadcasted_iota(jnp.int32, kr.shape, 1)
    krp_o[...] = jnp.where(lane >= half, kr, 0.0).astype(BF16)
    kr_o[...] = kr[:, 0:ROPE_DIM]

    kv = _dot(ckvn.astype(BF16), wkv_ref[...])
    kn_o[...] = kv[:, 0:MLA_WIDTH].astype(BF16)
    vm_o[...] = kv[:, MLA_WIDTH:2 * MLA_WIDTH].astype(BF16)

    qa_o[...] = (_dot(u, wa_ref[:, 0:A_WIDTH]) * A_SCALE).astype(BF16)
    ka = _dot(u, wa_ref[:, A_WIDTH:2 * A_WIDTH])
    ka_o[...] = ka.astype(BF16)
    kat_o[...] = ka
    va = _dot(u, wa_ref[:, 2 * A_WIDTH:3 * A_WIDTH])
    va_o[...] = va.astype(BF16)
    vat_o[...] = va


def _inproj(x, g, rot, wa, wc, wk, gcq, gckv, wuq, wkv, *, rows_per_seq, tail_rows):
    n, d = x.shape
    tm = min(INPROJ_ROW_TILE, n)
    n_tiles = n // tm
    tab_tiles = rot.shape[0] // tm
    if tm <= rows_per_seq:
        tiles_per_seq = rows_per_seq // tm
        tail_tiles = tail_rows // tm
        n_seq = n // rows_per_seq

        def tail_map(i):
            return (i // tiles_per_seq) * tail_tiles + jnp.maximum(i % tiles_per_seq - (tiles_per_seq - tail_tiles), 0), 0

        tail_n = n_seq * tail_rows
    else:
        assert tail_rows == rows_per_seq

        def tail_map(i):
            return i, 0

        tail_n = n

    def row(i):
        return i, 0

    def tab(i):
        return i % tab_tiles, 0

    wide = pl.BlockSpec((tm, A_WIDTH), row)
    out_shapes = [jax.ShapeDtypeStruct((n, A_WIDTH), BF16)] * 7 + [
        jax.ShapeDtypeStruct((n, LANES), BF16),
        jax.ShapeDtypeStruct((n, KV_LORA), F32),
        jax.ShapeDtypeStruct((n, ROPE_DIM), F32),
        jax.ShapeDtypeStruct((tail_n, A_WIDTH), F32),
        jax.ShapeDtypeStruct((tail_n, A_WIDTH), F32),
        jax.ShapeDtypeStruct((n, d), BF16),
    ]
    out_specs = [wide] * 7 + [
        pl.BlockSpec((tm, LANES), row),
        pl.BlockSpec((tm, KV_LORA), row),
        pl.BlockSpec((tm, ROPE_DIM), row),
        pl.BlockSpec((tm, A_WIDTH), tail_map),
        pl.BlockSpec((tm, A_WIDTH), tail_map),
        pl.BlockSpec((tm, d), row),
    ]
    weights = [wa, wc, wk, gcq, gckv, wuq, wkv]
    blocks = (_nbytes((tm, d), F32) + _nbytes((tm, d), BF16) + 7 * _nbytes((tm, A_WIDTH), BF16)
              + 3 * _nbytes((tm, A_WIDTH), F32)
              + 4 * _nbytes((tm, LANES), F32))
    resident = sum(_nbytes(w.shape, w.dtype) for w in weights)
    temps = 6 * _nbytes((tm, 2 * A_WIDTH), F32)
    return pl.pallas_call(
        _inproj_body,
        grid=(n_tiles,),
        in_specs=[
            pl.BlockSpec((tm, d), row),
            _resident((1, d)),
            pl.BlockSpec((tm, LANES), tab),
        ] + [_resident(w.shape) for w in weights],
        out_specs=out_specs,
        out_shape=out_shapes,
        compiler_params=pltpu.CompilerParams(
            dimension_semantics=("arbitrary",),
            vmem_limit_bytes=_vmem_limit(blocks, resident, temps)),
        name="inproj",
    )(x, g, rot, *weights)


def _toeplitz(row, nrows):
    x = jnp.broadcast_to(row * LOG2E, (nrows, TOEPLITZ_W))
    left = (BAND_Q - 1) - lax.broadcasted_iota(jnp.int32, (nrows, TOEPLITZ_W), 0)
    for b in range((BAND_Q - 1).bit_length()):
        rolled = pltpu.roll(x, TOEPLITZ_W - (1 << b), axis=1)
        x = jnp.where(((left >> b) & 1) == 1, rolled, x)
    return x


def _band_body(rb_ref, q_ref, k_ref, v_ref, o_ref, t_ref, sa_ref, sb_ref, *, n_tiles):
    @pl.when(pl.program_id(1) == 0)
    def _():
        t = _toeplitz(rb_ref[...], BAND_Q)[:, 0:BAND_K]
        i = lax.broadcasted_iota(jnp.int32, (BAND_Q, BAND_K), 0)
        j = lax.broadcasted_iota(jnp.int32, (BAND_Q, BAND_K), 1)
        first = lax.shift_left(lax.shift_right_logical(i, CHUNK.bit_length() - 1), CHUNK.bit_length() - 1)
        visible = (j >= first) & (j < first + BAND_PAST + CHUNK)
        t_ref[...] = jnp.where(visible, t, NEG)

    ones = jnp.ones((BAND_K, LANES), BF16)
    buffers = (sa_ref, sb_ref)
    past_tiles = BAND_PAST // BAND_Q

    def band(t):
        return max(t - past_tiles, 0) * BAND_Q, min(t + 1, past_tiles + 1) * BAND_Q

    def issue(t):
        k0, nk = band(t)
        s = _dot_nt(q_ref[t * BAND_Q:(t + 1) * BAND_Q, :], k_ref[k0:k0 + nk, :]) + t_ref[:, BAND_K - nk:BAND_K]
        buffers[t % 2][:, 0:nk] = s
        return jnp.max(s, axis=-1, keepdims=True)

    row_max = issue(0)
    for t in range(n_tiles):
        next_max = issue(t + 1) if t + 1 < n_tiles else None
        k0, nk = band(t)
        p = jnp.exp2(buffers[t % 2][:, 0:nk] - row_max).astype(BF16)
        acc = _dot(p, jnp.concatenate([v_ref[k0:k0 + nk, :], ones[0:nk]], axis=1))
        o_ref[t * BAND_Q:(t + 1) * BAND_Q, :] = (acc[:, 0:A_HEAD_DIM] / acc[:, A_HEAD_DIM:]).astype(BF16)
        row_max = next_max


def _band_prompt(q, k, v, rb_rows):
    b, s, _ = q.shape
    assert s % BAND_Q == 0 and s >= BAND_K
    head = pl.BlockSpec((None, s, A_HEAD_DIM), lambda h, i: (i, 0, h))
    blocks = 4 * _nbytes((s, A_HEAD_DIM), BF16)
    scratch = 3 * _nbytes((BAND_Q, BAND_K), F32)
    temps = 16 * _nbytes((BAND_Q, TOEPLITZ_W), F32)
    return pl.pallas_call(
        functools.partial(_band_body, n_tiles=s // BAND_Q),
        grid=(A_HEADS, b),
        in_specs=[pl.BlockSpec((None, 1, TOEPLITZ_W), lambda h, i: (h, 0, 0)), head, head, head],
        out_specs=head,
        out_shape=jax.ShapeDtypeStruct(q.shape, BF16),
        scratch_shapes=[pltpu.VMEM((BAND_Q, BAND_K), F32)] * 3,
        compiler_params=pltpu.CompilerParams(
            dimension_semantics=("arbitrary", "arbitrary"),
            vmem_limit_bytes=_vmem_limit(blocks, scratch, temps)),
        name="band_prompt",
    )(rb_rows, q, k, v)


def _band_sample_body(rb_ref, q_ref, kn_ref, vn_ref, ck_ref, cv_ref, o_ref):
    t_new = q_ref.shape[0]
    n_cache = ck_ref.shape[0]
    for h in range(A_HEADS):
        cols = slice(h * A_HEAD_DIM, (h + 1) * A_HEAD_DIM)
        bias = _toeplitz(rb_ref[h], t_new)
        q = q_ref[:, cols]
        s_c = _dot_nt(q, ck_ref[:, h, :].astype(BF16)) + bias[:, BAND_PAST - n_cache:BAND_PAST]
        s_n = _dot_nt(q, kn_ref[:, cols]) + bias[:, BAND_PAST:BAND_PAST + t_new]
        m = jnp.maximum(jnp.max(s_c, axis=-1, keepdims=True), jnp.max(s_n, axis=-1, keepdims=True))
        p_c = jnp.exp2(s_c - m)
        p_n = jnp.exp2(s_n - m)
        l = jnp.sum(p_c, axis=-1, keepdims=True) + jnp.sum(p_n, axis=-1, keepdims=True)
        o = _dot(p_c.astype(BF16), cv_ref[:, h, :].astype(BF16)) + _dot(p_n.astype(BF16), vn_ref[:, cols])
        o_ref[:, cols] = (o / l).astype(BF16)


def _band_sample(q, k_new, v_new, cache_k, cache_v, layer, rb_rows):
    b, t, w = q.shape
    a = cache_k.shape[2]
    assert t <= BAND_Q and a <= BAND_PAST
    new = pl.BlockSpec((None, t, w), lambda i: (i, 0, 0))
    cache = pl.BlockSpec((None, None, a, A_HEADS, A_HEAD_DIM), lambda i: (layer, i, 0, 0, 0))
    blocks = 4 * _nbytes((t, w), BF16) + 2 * _nbytes((a, w), F32)
    temps = 2 * _nbytes((a, w), F32) + 16 * _nbytes((t, TOEPLITZ_W), F32)
    return pl.pallas_call(
        _band_sample_body,
        grid=(b,),
        in_specs=[_resident(rb_rows.shape), new, new, new, cache, cache],
        out_specs=new,
        out_shape=jax.ShapeDtypeStruct(q.shape, BF16),
        compiler_params=pltpu.CompilerParams(
            dimension_semantics=("arbitrary",),
            vmem_limit_bytes=_vmem_limit(blocks, _nbytes(rb_rows.shape, F32), temps)),
        name="band_sample",
    )(rb_rows, q, k_new, v_new, cache_k, cache_v)


def _mla_body(qn_ref, qr_ref, kn_ref, kr_ref, v_ref, o_ref, sa_ref, sb_ref, *, n_tiles):
    tile = MLA_TILE
    r = lax.broadcasted_iota(jnp.int32, (tile, tile), 0)
    c = lax.broadcasted_iota(jnp.int32, (tile, tile), 1)
    chunk_bits = CHUNK.bit_length() - 1
    diag_visible = lax.shift_right_logical(c, chunk_bits) <= lax.shift_right_logical(r, chunk_bits)
    ones = jnp.ones((MLA_WIDE, LANES), BF16)
    buffers = (sa_ref, sb_ref)

    def row_max(s):
        return jnp.max(s, axis=-1, keepdims=True)

    for i in range(n_tiles):
        r0 = i * tile
        q = jnp.concatenate([qn_ref[r0:r0 + tile, :], qr_ref[r0:r0 + tile, :]], axis=1)
        blocks = [(c0, min(MLA_WIDE, r0 - c0)) for c0 in range(0, r0, MLA_WIDE)] + [(r0, tile)]

        def issue(k):
            c0, w = blocks[k]
            keys = jnp.concatenate([kn_ref[c0:c0 + w, :], kr_ref[c0:c0 + w, :]], axis=1)
            s = _dot_nt(q, keys)
            buffers[k % 2][:, 0:w] = s
            return row_max(s)

        m = jnp.full((tile, 1), NEG, F32)
        acc = jnp.zeros((tile, 2 * LANES), F32)
        s_max = issue(0)
        for k, (c0, w) in enumerate(blocks):
            last = k == len(blocks) - 1
            next_max = None if last else issue(k + 1)
            s = buffers[k % 2][:, 0:w]
            if last:
                s = jnp.where(diag_visible, s, NEG)
                s_max = row_max(s)
            m_new = jnp.maximum(m, s_max)
            p = jnp.exp2(s - m_new).astype(BF16)
            values = jnp.concatenate([v_ref[c0:c0 + w, :], ones[0:w]], axis=1)
            acc = jnp.exp2(m - m_new) * acc + _dot(p, values)
            m, s_max = m_new, next_max
        o_ref[r0:r0 + tile, :] = (acc[:, 0:V_DIM] / acc[:, V_DIM:]).astype(BF16)


def _mla_prompt(qn, qr, kn, kr, v):
    b, s, _ = qn.shape
    assert s % MLA_TILE == 0
    head = pl.BlockSpec((None, s, LANES), lambda i, h: (i, 0, h))
    shared = pl.BlockSpec((None, s, LANES), lambda i, h: (i, 0, 0))
    blocks = 6 * _nbytes((s, LANES), BF16)
    scratch = 2 * _nbytes((MLA_TILE, MLA_WIDE), F32)
    temps = 6 * _nbytes((MLA_TILE, MLA_WIDE), F32)
    return pl.pallas_call(
        functools.partial(_mla_body, n_tiles=s // MLA_TILE),
        grid=(b, MLA_HEADS),
        in_specs=[head, head, head, shared, head],
        out_specs=head,
        out_shape=jax.ShapeDtypeStruct(qn.shape, BF16),
        scratch_shapes=[pltpu.VMEM((MLA_TILE, MLA_WIDE), F32)] * 2,
        compiler_params=pltpu.CompilerParams(
            dimension_semantics=("arbitrary", "arbitrary"),
            vmem_limit_bytes=_vmem_limit(blocks, scratch, temps)),
        name="mla_prompt",
    )(qn, qr, kn, kr, v)


def _mla_sample_body(qn_ref, qr_ref, cn_ref, krn_ref, cc_ref, ckr_ref, wuk_ref, wuv_ref, o_ref):
    t_new = qn_ref.shape[0]
    q_lat = jnp.concatenate(
        [_dot(qn_ref[:, h * NOPE_DIM:(h + 1) * NOPE_DIM], wuk_ref[h]).astype(BF16) for h in range(MLA_HEADS)], axis=0)
    q_rot = jnp.concatenate(
        [qr_ref[:, (h + 1) * LANES - ROPE_DIM:(h + 1) * LANES] for h in range(MLA_HEADS)], axis=0)
    cache = cc_ref[...].astype(BF16)
    new = cn_ref[...].astype(BF16)
    s_c = _dot_nt(q_lat, cache) + _dot_nt(q_rot, ckr_ref[...].astype(BF16))
    s_n = _dot_nt(q_lat, new) + _dot_nt(q_rot, krn_ref[:, LANES - ROPE_DIM:LANES])
    m = jnp.maximum(jnp.max(s_c, axis=-1, keepdims=True), jnp.max(s_n, axis=-1, keepdims=True))
    p_c = jnp.exp2(s_c - m)
    p_n = jnp.exp2(s_n - m)
    l = jnp.sum(p_c, axis=-1, keepdims=True) + jnp.sum(p_n, axis=-1, keepdims=True)
    o_lat = ((_dot(p_c.astype(BF16), cache) + _dot(p_n.astype(BF16), new)) / l).astype(BF16)
    for h in range(MLA_HEADS):
        o_ref[:, h * V_DIM:(h + 1) * V_DIM] = _dot(o_lat[h * t_new:(h + 1) * t_new, :], wuv_ref[h]).astype(BF16)


def _mla_sample(qn, qr, ckv_new, kr_new, cache_ckv, cache_kr, layer, wuk, wuv):
    b, t, w = qn.shape
    past = cache_ckv.shape[2]

    def per_batch(shape):
        return pl.BlockSpec((None,) + shape, lambda i: (i, 0, 0))

    def cached(shape):
        return pl.BlockSpec((None, None) + shape, lambda i: (layer, i, 0, 0))

    blocks = (2 * _nbytes((t, w), BF16) + _nbytes((t, KV_LORA), F32) + _nbytes((t, LANES), BF16)
              + _nbytes((past, KV_LORA), F32) + _nbytes((past, ROPE_DIM), F32) + _nbytes((t, w), BF16))
    resident = _nbytes(wuk.shape, BF16) + _nbytes(wuv.shape, BF16)
    temps = _nbytes((past, KV_LORA), F32) + 4 * _nbytes((MLA_HEADS * t, past), F32)
    return pl.pallas_call(
        _mla_sample_body,
        grid=(b,),
        in_specs=[per_batch((t, w)), per_batch((t, w)), per_batch((t, KV_LORA)), per_batch((t, LANES)),
                  cached((past, KV_LORA)), cached((past, ROPE_DIM)),
                  _resident(wuk.shape), _resident(wuv.shape)],
        out_specs=per_batch((t, w)),
        out_shape=jax.ShapeDtypeStruct(qn.shape, BF16),
        compiler_params=pltpu.CompilerParams(
            dimension_semantics=("arbitrary",),
            vmem_limit_bytes=_vmem_limit(blocks, resident, temps)),
        name="mla_sample",
    )(qn, qr, ckv_new, kr_new, cache_ckv, cache_kr, wuk, wuv)


def _merge_body(h_ref, u_ref, oa_ref, ob_ref, wga_ref, wgb_ref, woa_ref, wob_ref, wout_ref, o_ref):
    j = pl.program_id(1)
    slab = h_ref.shape[0]
    d = o_ref.shape[1]

    @pl.when(j == 0)
    def _():
        o_ref[...] = jnp.zeros_like(o_ref)

    u = u_ref[...]
    m = (jax.nn.sigmoid(_dot(u, wga_ref[...])) * _dot(oa_ref[...], woa_ref[...])
         + jax.nn.sigmoid(_dot(u, wgb_ref[...])) * _dot(ob_ref[...], wob_ref[...])).astype(BF16)
    chunk = _col_tile(d)
    for c in range(0, d, chunk):
        o_ref[:, c:c + chunk] += _dot(m, wout_ref[:, c:c + chunk])
    o_ref[pl.ds(pl.multiple_of(j * slab, slab), slab), :] += h_ref[...]


def _merge(h, u, oa, ob, wga, wgb, woa, wob, wout):
    n, d = h.shape
    tm = min(MERGE_ROW_TILE, n)
    tn = MERGE_COL_TILE
    n_steps = d // tn
    slab = tm // n_steps
    assert d % tn == 0 and tm % n_steps == 0 and slab % 8 == 0
    blocks = (_nbytes((slab, d), F32) + _nbytes((tm, d), F32) + _nbytes((tm, d), BF16)
              + 2 * _nbytes((tm, A_WIDTH), BF16) + 2 * _nbytes((d, tn), BF16)
              + 2 * _nbytes((A_WIDTH, tn), BF16) + _nbytes((tn, d), BF16))
    temps = 8 * _nbytes((tm, max(tn, _col_tile(d))), F32)
    return pl.pallas_call(
        _merge_body,
        grid=(n // tm, n_steps),
        in_specs=[
            pl.BlockSpec((slab, d), lambda i, j: (i * n_steps + j, 0)),
            pl.BlockSpec((tm, d), lambda i, j: (i, 0)),
            pl.BlockSpec((tm, A_WIDTH), lambda i, j: (i, 0)),
            pl.BlockSpec((tm, MLA_WIDTH), lambda i, j: (i, 0)),
            pl.BlockSpec((d, tn), lambda i, j: (0, j)),
            pl.BlockSpec((d, tn), lambda i, j: (0, j)),
            pl.BlockSpec((A_WIDTH, tn), lambda i, j: (0, j)),
            pl.BlockSpec((MLA_WIDTH, tn), lambda i, j: (0, j)),
            pl.BlockSpec((tn, d), lambda i, j: (j, 0)),
        ],
        out_specs=pl.BlockSpec((tm, d), lambda i, j: (i, 0)),
        out_shape=jax.ShapeDtypeStruct((n, d), F32),
        compiler_params=pltpu.CompilerParams(
            dimension_semantics=("arbitrary", "arbitrary"),
            vmem_limit_bytes=_vmem_limit(blocks, 0, temps)),
        name="merge",
    )(h, u, oa, ob, wga, wgb, woa, wob, wout)


def _ple_body(h_ref, p_ref, gp_ref, wpg_ref, wp_ref, gf_ref, y_ref, *, final_norm):
    h = h_ref[...]
    gate = jax.nn.sigmoid(_dot(_rms(h, gp_ref[...]).astype(BF16), wpg_ref[...]))
    h = h + gate * _dot(p_ref[...].astype(BF16), wp_ref[...])
    y_ref[...] = _rms(h, gf_ref[...]) if final_norm else h


def _ple(h, p, layer, gp, wpg, wp, gf, *, final_norm):
    n, d = h.shape
    e = p.shape[2]
    tm = min(ROW_TILE, n)
    blocks = 2 * _nbytes((tm, d), F32) + _nbytes((tm, e), F32)
    resident = _nbytes(wpg.shape, BF16) + _nbytes(wp.shape, BF16)
    temps = 4 * _nbytes((tm, d), F32)
    return pl.pallas_call(
        functools.partial(_ple_body, final_norm=final_norm),
        grid=(n // tm,),
        in_specs=[
            pl.BlockSpec((tm, d), lambda i: (i, 0)),
            pl.BlockSpec((None, tm, e), lambda i: (layer, i, 0)),
            _resident((1, d)),
            _resident(wpg.shape),
            _resident(wp.shape),
            _resident((1, d)),
        ],
        out_specs=pl.BlockSpec((tm, d), lambda i: (i, 0)),
        out_shape=jax.ShapeDtypeStruct((n, d), F32),
        compiler_params=pltpu.CompilerParams(
            dimension_semantics=("arbitrary",),
            vmem_limit_bytes=_vmem_limit(blocks, resident, temps)),
        name="ple",
    )(h, p, gp, wpg, wp, gf)


def _rope_table(pos):
    half = ROPE_DIM // 2
    inv = ROPE_THETA ** (-jnp.arange(half, dtype=F32) / half)
    ang = pos.astype(F32)[:, None] * inv[None, :]
    cos, sin = jnp.cos(ang), jnp.sin(ang)
    return jnp.concatenate([cos, cos, -sin, sin], axis=1)


def _rope_columns(w):
    half = ROPE_DIM // 2
    x1, x2 = w[..., :half], w[..., half:]
    return jnp.concatenate([x1, x2, x2, x1], axis=-1)


def _layer_weights(i, d, g_ffn1, g_mix, w_in, g_cq, w_uq, w_uk, w_uv, g_ckv, rel_bias, w_oa, w_ob, w_out, g_ffn2,
                   g_ple, w_ple_gate, w_ple):
    bf = lambda a: a.astype(BF16)
    row = lambda a: a.reshape(1, -1).astype(F32)
    w = w_in[i]
    c0 = 3 * A_WIDTH
    c1 = c0 + Q_LORA
    c2 = c1 + KV_LORA
    c3 = c2 + ROPE_DIM
    uq = w_uq[i].reshape(Q_LORA, MLA_HEADS, NOPE_DIM + ROPE_DIM)
    rel = jnp.clip(jnp.arange(TOEPLITZ_W) - (BAND_PAST + BAND_Q - 1), -REL_CLIP, REL_CLIP) + REL_CLIP
    return dict(
        g_ffn1=row(g_ffn1[i]),
        g_ffn2=row(g_ffn2[i]),
        g_mix=row(g_mix[i]),
        inproj=(
            bf(w[:, 0:c0]),
            bf(w[:, c0:c1]),
            bf(jnp.concatenate([w[:, c1:c2], _rope_columns(w[:, c2:c3])], axis=1)),
            row(g_cq[i]),
            row(g_ckv[i]),
            bf(jnp.concatenate([uq[:, :, :NOPE_DIM].reshape(Q_LORA, MLA_WIDTH),
                                _rope_columns(uq[:, :, NOPE_DIM:]).reshape(Q_LORA, MLA_WIDTH)], axis=1)),
            bf(jnp.concatenate([jnp.transpose(w_uk[i], (2, 0, 1)).reshape(KV_LORA, MLA_WIDTH),
                                jnp.transpose(w_uv[i], (1, 0, 2)).reshape(KV_LORA, MLA_WIDTH)], axis=1)),
        ),
        rb_rows=rel_bias[i][:, rel].reshape(A_HEADS, 1, TOEPLITZ_W).astype(F32),
        w_uk=bf(w_uk[i]),
        w_uv=bf(w_uv[i]),
        merge=(bf(w[:, c3:c3 + d]), bf(w[:, c3 + d:c3 + 2 * d]), bf(w_oa[i]), bf(w_ob[i]), bf(w_out[i])),
        ple=(row(g_ple[i]), bf(w_ple_gate[i]), bf(w_ple[i])),
    )


def kernel(x_prompt, x_sample, cache_a_k, cache_a_v, cache_mla_ckv, cache_mla_krope, p_prompt, p_sample, g_ffn1, w_ffn1_gate, w_ffn1_up, w_ffn1_down, g_mix, w_in, g_cq, w_uq, w_uk, w_uv, g_ckv, rel_bias, w_oa, w_ob, w_out, g_ffn2, w_ffn2_gate, w_ffn2_up, w_ffn2_down, g_ple, w_ple_gate, w_ple, g_final):
    b, s, d = x_prompt.shape
    bs, t, _ = x_sample.shape
    depth = g_ffn1.shape[0]
    past = cache_mla_ckv.shape[2]
    a_len = cache_a_k.shape[2]
    a_keep = min(BAND_PAST, s)

    rot_p = _rope_table(jnp.arange(s))
    rot_s = jnp.tile(_rope_table(past + jnp.arange(t)), (bs, 1))
    g_fin = g_final.reshape(1, d).astype(F32)

    hp = x_prompt.reshape(b * s, d)
    hs = x_sample.reshape(bs * t, d)
    outs = [[] for _ in range(8)]
    for i in range(depth):
        lw = _layer_weights(i, d, g_ffn1, g_mix, w_in, g_cq, w_uq, w_uk, w_uv, g_ckv, rel_bias, w_oa, w_ob, w_out,
                            g_ffn2, g_ple, w_ple_gate, w_ple)

        hp, hs = _ffn_both(hp, hs, lw["g_ffn1"], w_ffn1_gate, w_ffn1_up, w_ffn1_down, i)

        qa, ka, va, qn, qr, kn, vm, krp, ckv, kr, ka_tail, va_tail, u = _inproj(
            hp, lw["g_mix"], rot_p, *lw["inproj"], rows_per_seq=s, tail_rows=a_keep)
        seq = lambda a: a.reshape(b, s, a.shape[-1])
        oa = _band_prompt(seq(qa), seq(ka), seq(va), lw["rb_rows"])
        ob = _mla_prompt(seq(qn), seq(qr), seq(kn), seq(krp), seq(vm))
        hp = _merge(hp, u, oa.reshape(b * s, A_WIDTH), ob.reshape(b * s, MLA_WIDTH), *lw["merge"])
        outs[0].append(ka_tail.reshape(b, a_keep, A_HEADS, A_HEAD_DIM))
        outs[1].append(va_tail.reshape(b, a_keep, A_HEADS, A_HEAD_DIM))
        outs[2].append(ckv.reshape(b, s, KV_LORA))
        outs[3].append(kr.reshape(b, s, ROPE_DIM))

        qa, ka, va, qn, qr, _, _, krp, ckv, kr, ka_new, va_new, u = _inproj(
            hs, lw["g_mix"], rot_s, *lw["inproj"], rows_per_seq=t, tail_rows=t)
        new = lambda a: a.reshape(bs, t, a.shape[-1])
        oa = _band_sample(new(qa), new(ka), new(va), cache_a_k, cache_a_v, i, lw["rb_rows"])
        ob = _mla_sample(new(qn), new(qr), new(ckv), new(krp), cache_mla_ckv, cache_mla_krope, i,
                         lw["w_uk"], lw["w_uv"])
        hs = _merge(hs, u, oa.reshape(bs * t, A_WIDTH), ob.reshape(bs * t, MLA_WIDTH), *lw["merge"])
        outs[4].append(ka_new.reshape(bs, t, A_HEADS, A_HEAD_DIM))
        outs[5].append(va_new.reshape(bs, t, A_HEADS, A_HEAD_DIM))
        outs[6].append(ckv.reshape(bs, t, KV_LORA))
        outs[7].append(kr.reshape(bs, t, ROPE_DIM))

        hp, hs = _ffn_both(hp, hs, lw["g_ffn2"], w_ffn2_gate, w_ffn2_up, w_ffn2_down, i)
        last = i == depth - 1
        hp = _ple(hp, p_prompt.reshape(depth, b * s, -1), i, *lw["ple"], g_fin, final_norm=last)
        hs = _ple(hs, p_sample.reshape(depth, bs * t, -1), i, *lw["ple"], g_fin, final_norm=last)

    stacked = [o[0][None] if depth == 1 else jnp.stack(o) for o in outs]
    return (hp.reshape(b, s, d), hs.reshape(bs, t, d), *stacked)
```

```python
import functools

import jax
import jax.numpy as jnp
from jax import lax
from jax.experimental import pallas as pl
from jax.experimental.pallas import tpu as pltpu

F32 = jnp.float32
BF16 = jnp.bfloat16

CHUNK = 64
BAND_PAST = 512
A_HEADS = 8
A_HEAD_DIM = 128
A_WIDTH = A_HEADS * A_HEAD_DIM
REL_CLIP = 128
MLA_HEADS = 8
Q_LORA = 768
KV_LORA = 512
NOPE_DIM = 128
ROPE_DIM = 64
V_DIM = 128
MLA_WIDTH = MLA_HEADS * V_DIM
LOG2E = 1.4426950408889634
MLA_SCALE = (NOPE_DIM + ROPE_DIM) ** -0.5 * LOG2E
A_SCALE = A_HEAD_DIM ** -0.5 * LOG2E
ROPE_THETA = 10000.0
EPS = 1e-6
NEG = -1e30

LANES = 128
V7X_VMEM_BYTES = 64 * 1024 * 1024
VMEM_COMPILER_RESERVE = 4 * 1024 * 1024
VMEM_CEILING = V7X_VMEM_BYTES - VMEM_COMPILER_RESERVE
VMEM_ESTIMATE_SLACK = 4 * 1024 * 1024

ROW_TILE = 512
MERGE_ROW_TILE = 1024
MERGE_COL_TILE = 512
FFN_ROW_TILE = 1024
INPROJ_ROW_TILE = 256
COL_TILE = 512
BAND_Q = 4 * CHUNK
BAND_K = BAND_PAST + BAND_Q
TOEPLITZ_W = BAND_K + BAND_Q
MLA_TILE = 512
MLA_WIDE = 1024


def _dot(a, b):
    return jnp.dot(a, b, preferred_element_type=F32)


def _dot_nt(a, b):
    return lax.dot_general(a, b, (((1,), (1,)), ((), ())), preferred_element_type=F32)


def _rms(x, g):
    return x * lax.rsqrt(jnp.mean(x * x, axis=-1, keepdims=True) + EPS) * g


def _vmem_limit(block_bytes, scratch_bytes, temp_bytes):
    return int(min(2 * block_bytes + scratch_bytes + temp_bytes + VMEM_ESTIMATE_SLACK, VMEM_CEILING))


def _nbytes(shape, dtype):
    n = 1
    for s in shape:
        n *= s
    return n * jnp.dtype(dtype).itemsize


def _resident(shape):
    nd = len(shape)
    return pl.BlockSpec(shape, lambda *_: (0,) * nd, pipeline_mode=pl.Buffered(1))


def _col_tile(n):
    for t in (COL_TILE, 256, LANES):
        if n % t == 0:
            return t
    raise ValueError(f"width {n} is not a multiple of {LANES}")


def _ffn_body(x_ref, g_ref, wg_ref, wu_ref, wd_ref, o_ref, u_ref):
    @pl.when(pl.program_id(1) == 0)
    def _():
        x = x_ref[...]
        u_ref[...] = _rms(x, g_ref[...]).astype(BF16)
        o_ref[...] = x

    u = u_ref[...]
    gate = _dot(u, wg_ref[...])
    up = _dot(u, wu_ref[...])
    half_act = (gate * jax.nn.sigmoid(gate) * (0.5 * up)).astype(BF16)
    d = o_ref.shape[1]
    chunk = _col_tile(d)
    for c in range(0, d, chunk):
        o_ref[:, c:c + chunk] += _dot(half_act, wd_ref[:, c:c + chunk])


def _ffn(x, g, wg, wu, wd):
    n, d = x.shape
    f = wg.shape[1]
    tm = min(FFN_ROW_TILE, n)
    tf = _col_tile(f)
    blocks = _nbytes((tm, d), F32) * 2 + 3 * _nbytes((d, tf), BF16)
    scratch = _nbytes((tm, d), BF16)
    temps = 6 * _nbytes((tm, tf), F32)
    return pl.pallas_call(
        _ffn_body,
        grid=(n // tm, f // tf),
        in_specs=[
            pl.BlockSpec((tm, d), lambda i, j: (i, 0)),
            pl.BlockSpec((1, d), lambda i, j: (0, 0)),
            pl.BlockSpec((d, tf), lambda i, j: (0, j)),
            pl.BlockSpec((d, tf), lambda i, j: (0, j)),
            pl.BlockSpec((tf, d), lambda i, j: (j, 0)),
        ],
        out_specs=pl.BlockSpec((tm, d), lambda i, j: (i, 0)),
        out_shape=jax.ShapeDtypeStruct((n, d), F32),
        scratch_shapes=[pltpu.VMEM((tm, d), BF16)],
        compiler_params=pltpu.CompilerParams(
            dimension_semantics=("arbitrary", "arbitrary"),
            vmem_limit_bytes=_vmem_limit(blocks, scratch, temps)),
        name="ffn",
    )(x, g, wg, wu, wd)


def _ffn_cast_body(x_ref, g_ref, wg_ref, wu_ref, wd_ref, o_ref, wg_o, wu_o, wd_o, u_ref):
    @pl.when(pl.program_id(1) == 0)
    def _():
        x = x_ref[...]
        u_ref[...] = _rms(x, g_ref[...]).astype(BF16)
        o_ref[...] = x

    wg = wg_ref[...].astype(BF16)
    wu = wu_ref[...].astype(BF16)
    wd = wd_ref[...].astype(BF16)
    wg_o[...] = wg
    wu_o[...] = wu
    wd_o[...] = wd
    u = u_ref[...]
    gate = _dot(u, wg)
    up = _dot(u, wu)
    half_act = (gate * jax.nn.sigmoid(gate) * (0.5 * up)).astype(BF16)
    o_ref[...] += _dot(half_act, wd)


def _ffn_cast(x, g, wg, wu, wd, layer):
    n, d = x.shape
    f = wg.shape[2]
    assert n <= FFN_ROW_TILE
    tf = _col_tile(f)
    blocks = (2 * _nbytes((n, d), F32) + 3 * _nbytes((d, tf), F32) + 3 * _nbytes((d, tf), BF16))
    scratch = _nbytes((n, d), BF16)
    temps = 6 * _nbytes((n, tf), F32) + 3 * _nbytes((d, tf), BF16) + _nbytes((n, d), F32)
    return pl.pallas_call(
        _ffn_cast_body,
        grid=(1, f // tf),
        in_specs=[
            pl.BlockSpec((n, d), lambda i, j: (0, 0)),
            pl.BlockSpec((1, d), lambda i, j: (0, 0)),
            pl.BlockSpec((None, d, tf), lambda i, j: (layer, 0, j)),
            pl.BlockSpec((None, d, tf), lambda i, j: (layer, 0, j)),
            pl.BlockSpec((None, tf, d), lambda i, j: (layer, j, 0)),
        ],
        out_specs=[
            pl.BlockSpec((n, d), lambda i, j: (0, 0)),
            pl.BlockSpec((d, tf), lambda i, j: (0, j)),
            pl.BlockSpec((d, tf), lambda i, j: (0, j)),
            pl.BlockSpec((tf, d), lambda i, j: (j, 0)),
        ],
        out_shape=[
            jax.ShapeDtypeStruct((n, d), F32),
            jax.ShapeDtypeStruct((d, f), BF16),
            jax.ShapeDtypeStruct((d, f), BF16),
            jax.ShapeDtypeStruct((f, d), BF16),
        ],
        scratch_shapes=[pltpu.VMEM((n, d), BF16)],
        compiler_params=pltpu.CompilerParams(
            dimension_semantics=("arbitrary", "arbitrary"),
            vmem_limit_bytes=_vmem_limit(blocks, scratch, temps)),
        name="ffn_cast",
    )(x, g, wg, wu, wd)


def _ffn_both(h_many, h_few, g, wg, wu, wd, layer):
    if h_few.shape[0] <= FFN_ROW_TILE:
        h_few, wg, wu, wd = _ffn_cast(h_few, g, wg, wu, wd, layer)
    else:
        wg, wu, wd = wg[layer].astype(BF16), wu[layer].astype(BF16), wd[layer].astype(BF16)
        h_few = _ffn(h_few, g, wg, wu, wd)
    return _ffn(h_many, g, wg, wu, wd), h_few


def _inproj_body(x_ref, g_ref, rot_ref, wa_ref, wc_ref, wk_ref, gcq_ref, gckv_ref, wuq_ref, wkv_ref,
                 qa_o, ka_o, va_o, qn_o, qr_o, kn_o, vm_o, krp_o, ckv_o, kr_o, kat_o, vat_o, u_o):
    u = _rms(x_ref[...], g_ref[...]).astype(BF16)
    u_o[...] = u

    rot = rot_ref[...]
    half = LANES // 2

    cqn = _rms(_dot(u, wc_ref[...]), gcq_ref[...]).astype(BF16)
    qn_o[...] = (_dot(cqn, wuq_ref[:, 0:MLA_WIDTH]) * MLA_SCALE).astype(BF16)
    t = _dot(cqn, wuq_ref[:, MLA_WIDTH:2 * MLA_WIDTH]) * jnp.concatenate([rot] * MLA_HEADS, axis=1)
    qr_o[...] = ((t + pltpu.roll(t, half, axis=1)) * MLA_SCALE).astype(BF16)

    zc = _dot(u, wk_ref[...])
    ckvn = _rms(zc[:, 0:KV_LORA], gckv_ref[...])
    ckv_o[...] = ckvn
    t = zc[:, KV_LORA:KV_LORA + LANES] * rot
    kr = t + pltpu.roll(t, half, axis=1)
    lane = lax.broadcasted_iota(jnp.int32, kr.shape, 1)
    krp_o[...] = jnp.where(lane >= half, kr, 0.0).astype(BF16)
    kr_o[...] = kr[:, 0:ROPE_DIM]

    kv = _dot(ckvn.astype(BF16), wkv_ref[...])
    kn_o[...] = kv[:, 0:MLA_WIDTH].astype(BF16)
    vm_o[...] = kv[:, MLA_WIDTH:2 * MLA_WIDTH].astype(BF16)

    qa_o[...] = (_dot(u, wa_ref[:, 0:A_WIDTH]) * A_SCALE).astype(BF16)
    ka = _dot(u, wa_ref[:, A_WIDTH:2 * A_WIDTH])
    ka_o[...] = ka.astype(BF16)
    kat_o[...] = ka
    va = _dot(u, wa_ref[:, 2 * A_WIDTH:3 * A_WIDTH])
    va_o[...] = va.astype(BF16)
    vat_o[...] = va


def _inproj(x, g, rot, wa, wc, wk, gcq, gckv, wuq, wkv, *, rows_per_seq, tail_rows):
    n, d = x.shape
    tm = min(INPROJ_ROW_TILE, n)
    n_tiles = n // tm
    tab_tiles = rot.shape[0] // tm
    if tm <= rows_per_seq:
        tiles_per_seq = rows_per_seq // tm
        tail_tiles = tail_rows // tm
        n_seq = n // rows_per_seq

        def tail_map(i):
            return (i // tiles_per_seq) * tail_tiles + jnp.maximum(i % tiles_per_seq - (tiles_per_seq - tail_tiles), 0), 0

        tail_n = n_seq * tail_rows
    else:
        assert tail_rows == rows_per_seq

        def tail_map(i):
            return i, 0

        tail_n = n

    def row(i):
        return i, 0

    def tab(i):
        return i % tab_tiles, 0

    wide = pl.BlockSpec((tm, A_WIDTH), row)
    out_shapes = [jax.ShapeDtypeStruct((n, A_WIDTH), BF16)] * 7 + [
        jax.ShapeDtypeStruct((n, LANES), BF16),
        jax.ShapeDtypeStruct((n, KV_LORA), F32),
        jax.ShapeDtypeStruct((n, ROPE_DIM), F32),
        jax.ShapeDtypeStruct((tail_n, A_WIDTH), F32),
        jax.ShapeDtypeStruct((tail_n, A_WIDTH), F32),
        jax.ShapeDtypeStruct((n, d), BF16),
    ]
    out_specs = [wide] * 7 + [
        pl.BlockSpec((tm, LANES), row),
        pl.BlockSpec((tm, KV_LORA), row),
        pl.BlockSpec((tm, ROPE_DIM), row),
        pl.BlockSpec((tm, A_WIDTH), tail_map),
        pl.BlockSpec((tm, A_WIDTH), tail_map),
        pl.BlockSpec((tm, d), row),
    ]
    weights = [wa, wc, wk, gcq, gckv, wuq, wkv]
    blocks = (_nbytes((tm, d), F32) + _nbytes((tm, d), BF16) + 7 * _nbytes((tm, A_WIDTH), BF16)
              + 3 * _nbytes((tm, A_WIDTH), F32)
              + 4 * _nbytes((tm, LANES), F32))
    resident = sum(_nbytes(w.shape, w.dtype) for w in weights)
    temps = 6 * _nbytes((tm, 2 * A_WIDTH), F32)
    return pl.pallas_call(
        _inproj_body,
        grid=(n_tiles,),
        in_specs=[
            pl.BlockSpec((tm, d), row),
            _resident((1, d)),
            pl.BlockSpec((tm, LANES), tab),
        ] + [_resident(w.shape) for w in weights],
        out_specs=out_specs,
        out_shape=out_shapes,
        compiler_params=pltpu.CompilerParams(
            dimension_semantics=("arbitrary",),
            vmem_limit_bytes=_vmem_limit(blocks, resident, temps)),
        name="inproj",
    )(x, g, rot, *weights)


def _toeplitz(row, nrows):
    x = jnp.broadcast_to(row * LOG2E, (nrows, TOEPLITZ_W))
    left = (BAND_Q - 1) - lax.broadcasted_iota(jnp.int32, (nrows, TOEPLITZ_W), 0)
    for b in range((BAND_Q - 1).bit_length()):
        rolled = pltpu.roll(x, TOEPLITZ_W - (1 << b), axis=1)
        x = jnp.where(((left >> b) & 1) == 1, rolled, x)
    return x


def _band_body(rb_ref, q_ref, k_ref, v_ref, o_ref, t_ref, sa_ref, sb_ref, *, n_tiles):
    @pl.when(pl.program_id(1) == 0)
    def _():
        t = _toeplitz(rb_ref[...], BAND_Q)[:, 0:BAND_K]
        i = lax.broadcasted_iota(jnp.int32, (BAND_Q, BAND_K), 0)
        j = lax.broadcasted_iota(jnp.int32, (BAND_Q, BAND_K), 1)
        first = lax.shift_left(lax.shift_right_logical(i, CHUNK.bit_length() - 1), CHUNK.bit_length() - 1)
        visible = (j >= first) & (j < first + BAND_PAST + CHUNK)
        t_ref[...] = jnp.where(visible, t, NEG)

    ones = jnp.ones((BAND_K, LANES), BF16)
    buffers = (sa_ref, sb_ref)
    past_tiles = BAND_PAST // BAND_Q

    def band(t):
        return max(t - past_tiles, 0) * BAND_Q, min(t + 1, past_tiles + 1) * BAND_Q

    def issue(t):
        k0, nk = band(t)
        s = _dot_nt(q_ref[t * BAND_Q:(t + 1) * BAND_Q, :], k_ref[k0:k0 + nk, :]) + t_ref[:, BAND_K - nk:BAND_K]
        buffers[t % 2][:, 0:nk] = s
        return jnp.max(s, axis=-1, keepdims=True)

    row_max = issue(0)
    for t in range(n_tiles):
        next_max = issue(t + 1) if t + 1 < n_tiles else None
        k0, nk = band(t)
        p = jnp.exp2(buffers[t % 2][:, 0:nk] - row_max).astype(BF16)
        acc = _dot(p, jnp.concatenate([v_ref[k0:k0 + nk, :], ones[0:nk]], axis=1))
        o_ref[t * BAND_Q:(t + 1) * BAND_Q, :] = (acc[:, 0:A_HEAD_DIM] / acc[:, A_HEAD_DIM:]).astype(BF16)
        row_max = next_max


def _band_prompt(q, k, v, rb_rows):
    b, s, _ = q.shape
    assert s % BAND_Q == 0 and s >= BAND_K
    head = pl.BlockSpec((None, s, A_HEAD_DIM), lambda h, i: (i, 0, h))
    blocks = 4 * _nbytes((s, A_HEAD_DIM), BF16)
    scratch = 3 * _nbytes((BAND_Q, BAND_K), F32)
    temps = 16 * _nbytes((BAND_Q, TOEPLITZ_W), F32)
    return pl.pallas_call(
        functools.partial(_band_body, n_tiles=s // BAND_Q),
        grid=(A_HEADS, b),
        in_specs=[pl.BlockSpec((None, 1, TOEPLITZ_W), lambda h, i: (h, 0, 0)), head, head, head],
        out_specs=head,
        out_shape=jax.ShapeDtypeStruct(q.shape, BF16),
        scratch_shapes=[pltpu.VMEM((BAND_Q, BAND_K), F32)] * 3,
        compiler_params=pltpu.CompilerParams(
            dimension_semantics=("arbitrary", "arbitrary"),
            vmem_limit_bytes=_vmem_limit(blocks, scratch, temps)),
        name="band_prompt",
    )(rb_rows, q, k, v)


def _band_sample_body(rb_ref, q_ref, kn_ref, vn_ref, ck_ref, cv_ref, o_ref):
    t_new = q_ref.shape[0]
    n_cache = ck_ref.shape[0]
    for h in range(A_HEADS):
        cols = slice(h * A_HEAD_DIM, (h + 1) * A_HEAD_DIM)
        bias = _toeplitz(rb_ref[h], t_new)
        q = q_ref[:, cols]
        s_c = _dot_nt(q, ck_ref[:, h, :].astype(BF16)) + bias[:, BAND_PAST - n_cache:BAND_PAST]
        s_n = _dot_nt(q, kn_ref[:, cols]) + bias[:, BAND_PAST:BAND_PAST + t_new]
        m = jnp.maximum(jnp.max(s_c, axis=-1, keepdims=True), jnp.max(s_n, axis=-1, keepdims=True))
        p_c = jnp.exp2(s_c - m)
        p_n = jnp.exp2(s_n - m)
        l = jnp.sum(p_c, axis=-1, keepdims=True) + jnp.sum(p_n, axis=-1, keepdims=True)
        o = _dot(p_c.astype(BF16), cv_ref[:, h, :].astype(BF16)) + _dot(p_n.astype(BF16), vn_ref[:, cols])
        o_ref[:, cols] = (o / l).astype(BF16)


def _band_sample(q, k_new, v_new, cache_k, cache_v, layer, rb_rows):
    b, t, w = q.shape
    a = cache_k.shape[2]
    assert t <= BAND_Q and a <= BAND_PAST
    new = pl.BlockSpec((None, t, w), lambda i: (i, 0, 0))
    cache = pl.BlockSpec((None, None, a, A_HEADS, A_HEAD_DIM), lambda i: (layer, i, 0, 0, 0))
    blocks = 4 * _nbytes((t, w), BF16) + 2 * _nbytes((a, w), F32)
    temps = 2 * _nbytes((a, w), F32) + 16 * _nbytes((t, TOEPLITZ_W), F32)
    return pl.pallas_call(
        _band_sample_body,
        grid=(b,),
        in_specs=[_resident(rb_rows.shape), new, new, new, cache, cache],
        out_specs=new,
        out_shape=jax.ShapeDtypeStruct(q.shape, BF16),
        compiler_params=pltpu.CompilerParams(
            dimension_semantics=("arbitrary",),
            vmem_limit_bytes=_vmem_limit(blocks, _nbytes(rb_rows.shape, F32), temps)),
        name="band_sample",
    )(rb_rows, q, k_new, v_new, cache_k, cache_v)


def _mla_body(qn_ref, qr_ref, kn_ref, kr_ref, v_ref, o_ref, sa_ref, sb_ref, *, n_tiles):
    tile = MLA_TILE
    half = tile // 2
    chunk_bits = CHUNK.bit_length() - 1

    def visible(row0, n_rows, n_cols):
        r = lax.broadcasted_iota(jnp.int32, (n_rows, n_cols), 0) + row0
        c = lax.broadcasted_iota(jnp.int32, (n_rows, n_cols), 1)
        return lax.shift_right_logical(c, chunk_bits) <= lax.shift_right_logical(r, chunk_bits)

    diag_parts = ((0, half, visible(0, half, half)), (half, tile, visible(half, half, tile)))
    ones = jnp.ones((MLA_WIDE, LANES), BF16)
    buffers = (sa_ref, sb_ref)

    def row_max(s):
        return jnp.max(s, axis=-1, keepdims=True)

    for i in range(n_tiles):
        r0 = i * tile
        q = jnp.concatenate([qn_ref[r0:r0 + tile, :], qr_ref[r0:r0 + tile, :]], axis=1)
        blocks = [(c0, min(MLA_WIDE, r0 - c0)) for c0 in range(0, r0, MLA_WIDE)] + [(r0, tile)]

        def issue(k):
            c0, w = blocks[k]
            keys = jnp.concatenate([kn_ref[c0:c0 + w, :], kr_ref[c0:c0 + w, :]], axis=1)
            if k == len(blocks) - 1:
                buffers[k % 2][0:half, 0:half] = _dot_nt(q[0:half], keys[0:half])
                buffers[k % 2][half:tile, 0:w] = _dot_nt(q[half:tile], keys)
                return None
            s = _dot_nt(q, keys)
            buffers[k % 2][:, 0:w] = s
            return row_max(s)

        def fold(m, acc, s, s_max, c0, w):
            m_new = jnp.maximum(m, s_max)
            p = jnp.exp2(s - m_new).astype(BF16)
            values = jnp.concatenate([v_ref[c0:c0 + w, :], ones[0:w]], axis=1)
            return m_new, jnp.exp2(m - m_new) * acc + _dot(p, values)

        m = jnp.full((tile, 1), NEG, F32)
        acc = jnp.zeros((tile, 2 * LANES), F32)
        s_max = issue(0)
        for k, (c0, w) in enumerate(blocks[:-1]):
            next_max = issue(k + 1)
            m, acc = fold(m, acc, buffers[k % 2][:, 0:w], s_max, c0, w)
            s_max = next_max
        diag = buffers[(len(blocks) - 1) % 2]
        for first, w, mask in diag_parts:
            rows = slice(first, first + half)
            s = jnp.where(mask, diag[rows, 0:w], NEG)
            _, part = fold(m[rows], acc[rows], s, row_max(s), r0, w)
            o_ref[r0 + first:r0 + first + half, :] = (part[:, 0:V_DIM] / part[:, V_DIM:]).astype(BF16)


def _mla_prompt(qn, qr, kn, kr, v):
    b, s, _ = qn.shape
    assert s % MLA_TILE == 0
    head = pl.BlockSpec((None, s, LANES), lambda i, h: (i, 0, h))
    shared = pl.BlockSpec((None, s, LANES), lambda i, h: (i, 0, 0))
    blocks = 6 * _nbytes((s, LANES), BF16)
    scratch = 2 * _nbytes((MLA_TILE, MLA_WIDE), F32)
    temps = 6 * _nbytes((MLA_TILE, MLA_WIDE), F32)
    return pl.pallas_call(
        functools.partial(_mla_body, n_tiles=s // MLA_TILE),
        grid=(b, MLA_HEADS),
        in_specs=[head, head, head, shared, head],
        out_specs=head,
        out_shape=jax.ShapeDtypeStruct(qn.shape, BF16),
        scratch_shapes=[pltpu.VMEM((MLA_TILE, MLA_WIDE), F32)] * 2,
        compiler_params=pltpu.CompilerParams(
            dimension_semantics=("arbitrary", "arbitrary"),
            vmem_limit_bytes=_vmem_limit(blocks, scratch, temps)),
        name="mla_prompt",
    )(qn, qr, kn, kr, v)


def _mla_sample_body(qn_ref, qr_ref, cn_ref, krn_ref, cc_ref, ckr_ref, wuk_ref, wuv_ref, o_ref):
    t_new = qn_ref.shape[0]
    q_lat = jnp.concatenate(
        [_dot(qn_ref[:, h * NOPE_DIM:(h + 1) * NOPE_DIM], wuk_ref[h]).astype(BF16) for h in range(MLA_HEADS)], axis=0)
    q_rot = jnp.concatenate(
        [qr_ref[:, (h + 1) * LANES - ROPE_DIM:(h + 1) * LANES] for h in range(MLA_HEADS)], axis=0)
    cache = cc_ref[...].astype(BF16)
    new = cn_ref[...].astype(BF16)
    s_c = _dot_nt(q_lat, cache) + _dot_nt(q_rot, ckr_ref[...].astype(BF16))
    s_n = _dot_nt(q_lat, new) + _dot_nt(q_rot, krn_ref[:, LANES - ROPE_DIM:LANES])
    m = jnp.maximum(jnp.max(s_c, axis=-1, keepdims=True), jnp.max(s_n, axis=-1, keepdims=True))
    p_c = jnp.exp2(s_c - m)
    p_n = jnp.exp2(s_n - m)
    l = jnp.sum(p_c, axis=-1, keepdims=True) + jnp.sum(p_n, axis=-1, keepdims=True)
    o_lat = ((_dot(p_c.astype(BF16), cache) + _dot(p_n.astype(BF16), new)) / l).astype(BF16)
    for h in range(MLA_HEADS):
        o_ref[:, h * V_DIM:(h + 1) * V_DIM] = _dot(o_lat[h * t_new:(h + 1) * t_new, :], wuv_ref[h]).astype(BF16)


def _mla_sample(qn, qr, ckv_new, kr_new, cache_ckv, cache_kr, layer, wuk, wuv):
    b, t, w = qn.shape
    past = cache_ckv.shape[2]

    def per_batch(shape):
        return pl.BlockSpec((None,) + shape, lambda i: (i, 0, 0))

    def cached(shape):
        return pl.BlockSpec((None, None) + shape, lambda i: (layer, i, 0, 0))

    blocks = (2 * _nbytes((t, w), BF16) + _nbytes((t, KV_LORA), F32) + _nbytes((t, LANES), BF16)
              + _nbytes((past, KV_LORA), F32) + _nbytes((past, ROPE_DIM), F32) + _nbytes((t, w), BF16))
    resident = _nbytes(wuk.shape, BF16) + _nbytes(wuv.shape, BF16)
    temps = _nbytes((past, KV_LORA), F32) + 4 * _nbytes((MLA_HEADS * t, past), F32)
    return pl.pallas_call(
        _mla_sample_body,
        grid=(b,),
        in_specs=[per_batch((t, w)), per_batch((t, w)), per_batch((t, KV_LORA)), per_batch((t, LANES)),
                  cached((past, KV_LORA)), cached((past, ROPE_DIM)),
                  _resident(wuk.shape), _resident(wuv.shape)],
        out_specs=per_batch((t, w)),
        out_shape=jax.ShapeDtypeStruct(qn.shape, BF16),
        compiler_params=pltpu.CompilerParams(
            dimension_semantics=("arbitrary",),
            vmem_limit_bytes=_vmem_limit(blocks, resident, temps)),
        name="mla_sample",
    )(qn, qr, ckv_new, kr_new, cache_ckv, cache_kr, wuk, wuv)


def _merge_body(h_ref, u_ref, oa_ref, ob_ref, wga_ref, wgb_ref, woa_ref, wob_ref, wout_ref, o_ref):
    j = pl.program_id(1)
    slab = h_ref.shape[0]
    d = o_ref.shape[1]

    @pl.when(j == 0)
    def _():
        o_ref[...] = jnp.zeros_like(o_ref)

    u = u_ref[...]
    m = (jax.nn.sigmoid(_dot(u, wga_ref[...])) * _dot(oa_ref[...], woa_ref[...])
         + jax.nn.sigmoid(_dot(u, wgb_ref[...])) * _dot(ob_ref[...], wob_ref[...])).astype(BF16)
    chunk = _col_tile(d)
    for c in range(0, d, chunk):
        o_ref[:, c:c + chunk] += _dot(m, wout_ref[:, c:c + chunk])
    o_ref[pl.ds(pl.multiple_of(j * slab, slab), slab), :] += h_ref[...]


def _merge(h, u, oa, ob, wga, wgb, woa, wob, wout):
    n, d = h.shape
    tm = min(MERGE_ROW_TILE, n)
    tn = MERGE_COL_TILE
    n_steps = d // tn
    slab = tm // n_steps
    assert d % tn == 0 and tm % n_steps == 0 and slab % 8 == 0
    blocks = (_nbytes((slab, d), F32) + _nbytes((tm, d), F32) + _nbytes((tm, d), BF16)
              + 2 * _nbytes((tm, A_WIDTH), BF16) + 2 * _nbytes((d, tn), BF16)
              + 2 * _nbytes((A_WIDTH, tn), BF16) + _nbytes((tn, d), BF16))
    temps = 8 * _nbytes((tm, max(tn, _col_tile(d))), F32)
    return pl.pallas_call(
        _merge_body,
        grid=(n // tm, n_steps),
        in_specs=[
            pl.BlockSpec((slab, d), lambda i, j: (i * n_steps + j, 0)),
            pl.BlockSpec((tm, d), lambda i, j: (i, 0)),
            pl.BlockSpec((tm, A_WIDTH), lambda i, j: (i, 0)),
            pl.BlockSpec((tm, MLA_WIDTH), lambda i, j: (i, 0)),
            pl.BlockSpec((d, tn), lambda i, j: (0, j)),
            pl.BlockSpec((d, tn), lambda i, j: (0, j)),
            pl.BlockSpec((A_WIDTH, tn), lambda i, j: (0, j)),
            pl.BlockSpec((MLA_WIDTH, tn), lambda i, j: (0, j)),
            pl.BlockSpec((tn, d), lambda i, j: (j, 0)),
        ],
        out_specs=pl.BlockSpec((tm, d), lambda i, j: (i, 0)),
        out_shape=jax.ShapeDtypeStruct((n, d), F32),
        compiler_params=pltpu.CompilerParams(
            dimension_semantics=("arbitrary", "arbitrary"),
            vmem_limit_bytes=_vmem_limit(blocks, 0, temps)),
        name="merge",
    )(h, u, oa, ob, wga, wgb, woa, wob, wout)


def _ple_body(h_ref, p_ref, gp_ref, wpg_ref, wp_ref, gf_ref, y_ref, *, final_norm):
    h = h_ref[...]
    gate = jax.nn.sigmoid(_dot(_rms(h, gp_ref[...]).astype(BF16), wpg_ref[...]))
    h = h + gate * _dot(p_ref[...].astype(BF16), wp_ref[...])
    y_ref[...] = _rms(h, gf_ref[...]) if final_norm else h


def _ple(h, p, layer, gp, wpg, wp, gf, *, final_norm):
    n, d = h.shape
    e = p.shape[2]
    tm = min(ROW_TILE, n)
    blocks = 2 * _nbytes((tm, d), F32) + _nbytes((tm, e), F32)
    resident = _nbytes(wpg.shape, BF16) + _nbytes(wp.shape, BF16)
    temps = 4 * _nbytes((tm, d), F32)
    return pl.pallas_call(
        functools.partial(_ple_body, final_norm=final_norm),
        grid=(n // tm,),
        in_specs=[
            pl.BlockSpec((tm, d), lambda i: (i, 0)),
            pl.BlockSpec((None, tm, e), lambda i: (layer, i, 0)),
            _resident((1, d)),
            _resident(wpg.shape),
            _resident(wp.shape),
            _resident((1, d)),
        ],
        out_specs=pl.BlockSpec((tm, d), lambda i: (i, 0)),
        out_shape=jax.ShapeDtypeStruct((n, d), F32),
        compiler_params=pltpu.CompilerParams(
            dimension_semantics=("arbitrary",),
            vmem_limit_bytes=_vmem_limit(blocks, resident, temps)),
        name="ple",
    )(h, p, gp, wpg, wp, gf)


def _rope_table(pos):
    half = ROPE_DIM // 2
    inv = ROPE_THETA ** (-jnp.arange(half, dtype=F32) / half)
    ang = pos.astype(F32)[:, None] * inv[None, :]
    cos, sin = jnp.cos(ang), jnp.sin(ang)
    return jnp.concatenate([cos, cos, -sin, sin], axis=1)


def _rope_columns(w):
    half = ROPE_DIM // 2
    x1, x2 = w[..., :half], w[..., half:]
    return jnp.concatenate([x1, x2, x2, x1], axis=-1)


def _layer_weights(i, d, g_ffn1, g_mix, w_in, g_cq, w_uq, w_uk, w_uv, g_ckv, rel_bias, w_oa, w_ob, w_out, g_ffn2,
                   g_ple, w_ple_gate, w_ple):
    bf = lambda a: a.astype(BF16)
    row = lambda a: a.reshape(1, -1).astype(F32)
    w = w_in[i]
    c0 = 3 * A_WIDTH
    c1 = c0 + Q_LORA
    c2 = c1 + KV_LORA
    c3 = c2 + ROPE_DIM
    uq = w_uq[i].reshape(Q_LORA, MLA_HEADS, NOPE_DIM + ROPE_DIM)
    rel = jnp.clip(jnp.arange(TOEPLITZ_W) - (BAND_PAST + BAND_Q - 1), -REL_CLIP, REL_CLIP) + REL_CLIP
    return dict(
        g_ffn1=row(g_ffn1[i]),
        g_ffn2=row(g_ffn2[i]),
        g_mix=row(g_mix[i]),
        inproj=(
            bf(w[:, 0:c0]),
            bf(w[:, c0:c1]),
            bf(jnp.concatenate([w[:, c1:c2], _rope_columns(w[:, c2:c3])], axis=1)),
            row(g_cq[i]),
            row(g_ckv[i]),
            bf(jnp.concatenate([uq[:, :, :NOPE_DIM].reshape(Q_LORA, MLA_WIDTH),
                                _rope_columns(uq[:, :, NOPE_DIM:]).reshape(Q_LORA, MLA_WIDTH)], axis=1)),
            bf(jnp.concatenate([jnp.transpose(w_uk[i], (2, 0, 1)).reshape(KV_LORA, MLA_WIDTH),
                                jnp.transpose(w_uv[i], (1, 0, 2)).reshape(KV_LORA, MLA_WIDTH)], axis=1)),
        ),
        rb_rows=rel_bias[i][:, rel].reshape(A_HEADS, 1, TOEPLITZ_W).astype(F32),
        w_uk=bf(w_uk[i]),
        w_uv=bf(w_uv[i]),
        merge=(bf(w[:, c3:c3 + d]), bf(w[:, c3 + d:c3 + 2 * d]), bf(w_oa[i]), bf(w_ob[i]), bf(w_out[i])),
        ple=(row(g_ple[i]), bf(w_ple_gate[i]), bf(w_ple[i])),
    )


def kernel(x_prompt, x_sample, cache_a_k, cache_a_v, cache_mla_ckv, cache_mla_krope, p_prompt, p_sample, g_ffn1, w_ffn1_gate, w_ffn1_up, w_ffn1_down, g_mix, w_in, g_cq, w_uq, w_uk, w_uv, g_ckv, rel_bias, w_oa, w_ob, w_out, g_ffn2, w_ffn2_gate, w_ffn2_up, w_ffn2_down, g_ple, w_ple_gate, w_ple, g_final):
    b, s, d = x_prompt.shape
    bs, t, _ = x_sample.shape
    depth = g_ffn1.shape[0]
    past = cache_mla_ckv.shape[2]
    a_len = cache_a_k.shape[2]
    a_keep = min(BAND_PAST, s)

    rot_p = _rope_table(jnp.arange(s))
    rot_s = jnp.tile(_rope_table(past + jnp.arange(t)), (bs, 1))
    g_fin = g_final.reshape(1, d).astype(F32)

    hp = x_prompt.reshape(b * s, d)
    hs = x_sample.reshape(bs * t, d)
    outs = [[] for _ in range(8)]
    for i in range(depth):
        lw = _layer_weights(i, d, g_ffn1, g_mix, w_in, g_cq, w_uq, w_uk, w_uv, g_ckv, rel_bias, w_oa, w_ob, w_out,
                            g_ffn2, g_ple, w_ple_gate, w_ple)

        hp, hs = _ffn_both(hp, hs, lw["g_ffn1"], w_ffn1_gate, w_ffn1_up, w_ffn1_down, i)

        qa, ka, va, qn, qr, kn, vm, krp, ckv, kr, ka_tail, va_tail, u = _inproj(
            hp, lw["g_mix"], rot_p, *lw["inproj"], rows_per_seq=s, tail_rows=a_keep)
        seq = lambda a: a.reshape(b, s, a.shape[-1])
        oa = _band_prompt(seq(qa), seq(ka), seq(va), lw["rb_rows"])
        ob = _mla_prompt(seq(qn), seq(qr), seq(kn), seq(krp), seq(vm))
        hp = _merge(hp, u, oa.reshape(b * s, A_WIDTH), ob.reshape(b * s, MLA_WIDTH), *lw["merge"])
        outs[0].append(ka_tail.reshape(b, a_keep, A_HEADS, A_HEAD_DIM))
        outs[1].append(va_tail.reshape(b, a_keep, A_HEADS, A_HEAD_DIM))
        outs[2].append(ckv.reshape(b, s, KV_LORA))
        outs[3].append(kr.reshape(b, s, ROPE_DIM))

        qa, ka, va, qn, qr, _, _, krp, ckv, kr, ka_new, va_new, u = _inproj(
            hs, lw["g_mix"], rot_s, *lw["inproj"], rows_per_seq=t, tail_rows=t)
        new = lambda a: a.reshape(bs, t, a.shape[-1])
        oa = _band_sample(new(qa), new(ka), new(va), cache_a_k, cache_a_v, i, lw["rb_rows"])
        ob = _mla_sample(new(qn), new(qr), new(ckv), new(krp), cache_mla_ckv, cache_mla_krope, i,
                         lw["w_uk"], lw["w_uv"])
        hs = _merge(hs, u, oa.reshape(bs * t, A_WIDTH), ob.reshape(bs * t, MLA_WIDTH), *lw["merge"])
        outs[4].append(ka_new.reshape(bs, t, A_HEADS, A_HEAD_DIM))
        outs[5].append(va_new.reshape(bs, t, A_HEADS, A_HEAD_DIM))
        outs[6].append(ckv.reshape(bs, t, KV_LORA))
        outs[7].append(kr.reshape(bs, t, ROPE_DIM))

        hp, hs = _ffn_both(hp, hs, lw["g_ffn2"], w_ffn2_gate, w_ffn2_up, w_ffn2_down, i)
        last = i == depth - 1
        hp = _ple(hp, p_prompt.reshape(depth, b * s, -1), i, *lw["ple"], g_fin, final_norm=last)
        hs = _ple(hs, p_sample.reshape(depth, bs * t, -1), i, *lw["ple"], g_fin, final_norm=last)

    stacked = [o[0][None] if depth == 1 else jnp.stack(o) for o in outs]
    return (hp.reshape(b, s, d), hs.reshape(bs, t, d), *stacked)
```
